```python
import math, functools
import jax, jax.numpy as jnp
from jax import lax
import numpy as np

D_MODEL = 1024
BATCH = 2
SEQ = 8192
DEPTH = 1
DEC_BATCH = 32
DEC_SEQ = 8
PAST_LEN = 8192
PAGE_SIZE = 128

N_META = 16
HG_HEADS = 4
HG_DK = 128
HG_DV = 128
HG_CHUNK = 64
HG_WK = HG_HEADS * HG_DK
HG_WV = HG_HEADS * HG_DV
DA_HEADS = 4
DA_DH = 64
DA_DV = 2 * DA_DH
DA_WQK = DA_HEADS * 2 * DA_DH
DA_WV = DA_HEADS * DA_DV
Q_BLOCK = 128
N_BUCKETS = 32
MAX_DISTANCE = 128
N_GROUPS = 4
EXPERTS_PER_GROUP = 4
N_EXPERTS = N_GROUPS * EXPERTS_PER_GROUP
EXPERT_FF = 256
TOP_K = 2
EPS = 1e-6
IN_SPLITS = (HG_WK, HG_WK, HG_WV, HG_WV, DA_WQK, DA_WQK, DA_WV, D_MODEL, D_MODEL)
IN_WIDTH = HG_WK * 2 + HG_WV * 2 + DA_WQK * 2 + DA_WV + 2 * D_MODEL

kernel_name = "hgrn2_diffattn_hier_moe_step"


def rmsnorm(x, w):
    xf = x.astype(jnp.float32)
    y = xf * lax.rsqrt(jnp.mean(xf * xf, axis=-1, keepdims=True) + EPS)
    return (y * w.astype(jnp.float32)).astype(x.dtype)


def split_cols(z):
    outs, start = [], 0
    for width in IN_SPLITS:
        outs.append(z[..., start:start + width])
        start += width
    return outs


def t5_bucket(dist):
    n = jnp.maximum(dist, 0)
    max_exact = N_BUCKETS // 2
    nf = jnp.maximum(n, 1).astype(jnp.float32)
    large = max_exact + (jnp.log(nf / max_exact) / math.log(MAX_DISTANCE / max_exact)
                         * (N_BUCKETS - max_exact)).astype(jnp.int32)
    large = jnp.minimum(large, N_BUCKETS - 1)
    return jnp.where(n < max_exact, n, large)


def hgrn_chunk(S, q, k, v, logf):
    C = q.shape[1]
    b = jnp.cumsum(logf, axis=1)
    o_inter = jnp.einsum('bchk,bhkv->bchv', q * jnp.exp(b), S)
    causal = jnp.tril(jnp.ones((C, C), dtype=bool))[None, :, :, None, None]
    decay = jnp.exp(jnp.where(causal, b[:, :, None] - b[:, None, :], -jnp.inf))
    scores = jnp.sum(q[:, :, None] * k[:, None, :] * decay, axis=-1)
    o_intra = jnp.einsum('btsh,bshv->bthv', scores, v)
    b_last = b[:, -1]
    k_dec = k * jnp.exp(b_last[:, None] - b)
    S_new = jnp.exp(b_last)[..., None] * S + jnp.einsum('bchk,bchv->bhkv', k_dec, v)
    return S_new, o_inter + o_intra


def hgrn_prompt(q, k, v, logf):
    B = q.shape[0]
    S0 = jnp.zeros((B, HG_HEADS, HG_DK, HG_DV), jnp.float32)
    S_meta, o_meta = hgrn_chunk(S0, q[:, :N_META], k[:, :N_META], v[:, :N_META], logf[:, :N_META])

    def to_chunks(a):
        r = a[:, N_META:]
        n_chunks = r.shape[1] // HG_CHUNK
        return jnp.moveaxis(r.reshape(B, n_chunks, HG_CHUNK, *r.shape[2:]), 1, 0)

    def step(S, xs):
        return hgrn_chunk(S, *xs)

    S_fin, o_chunks = lax.scan(step, S_meta, (to_chunks(q), to_chunks(k), to_chunks(v), to_chunks(logf)))
    o_real = jnp.moveaxis(o_chunks, 0, 1).reshape(B, -1, HG_HEADS, HG_DV)
    return jnp.concatenate([o_meta, o_real], axis=1), S_fin


def hgrn_sample(q, k, v, logf, state):
    S_new, o = hgrn_chunk(state.astype(jnp.float32), q, k, v, logf)
    return o, S_new


def diff_attend(q, k, v, q_pos, k_pos, lam, rel_bias):
    logits = jnp.einsum('bqhmd,bkhmd->bhmqk', q, k).astype(jnp.float32) * (DA_DH ** -0.5)
    dist = q_pos[:, None] - k_pos[None, :]
    bias = jnp.transpose(rel_bias[t5_bucket(dist)].astype(jnp.float32), (2, 0, 1))
    logits = jnp.where((dist >= 0)[None, None, None], logits + bias[None, :, None], -1e30)
    p = jax.nn.softmax(logits, axis=-1)
    attn = p[:, :, 0] - lam * p[:, :, 1]
    return jnp.einsum('bhqk,bkhv->bqhv', attn, v.astype(jnp.float32))


def prompt_attn(q, k, v, lam, rel_bias):
    B, T = q.shape[:2]
    L = T - N_META
    k_pos = jnp.arange(T)
    meta_pos = jnp.arange(N_META)
    o_meta = diff_attend(q[:, :N_META], k[:, :N_META], v[:, :N_META], meta_pos, meta_pos, lam, rel_bias)
    n_blocks = L // Q_BLOCK
    q_blocks = jnp.moveaxis(q[:, N_META:].reshape(B, n_blocks, Q_BLOCK, *q.shape[2:]), 1, 0)
    pos_blocks = (N_META + jnp.arange(L)).reshape(n_blocks, Q_BLOCK)
    o_blocks = lax.map(lambda xs: diff_attend(xs[0], k, v, xs[1], k_pos, lam, rel_bias),
                       (q_blocks, pos_blocks))
    o_real = jnp.moveaxis(o_blocks, 0, 1).reshape(B, L, DA_HEADS, DA_DV)
    return jnp.concatenate([o_meta, o_real], axis=1)


def sample_attn(q, k, v, lam, ck, cv, page_table, rel_bias):
    DB, S = q.shape[:2]
    past_k = ck[page_table].reshape(DB, PAST_LEN, DA_HEADS, 2, DA_DH)
    past_v = cv[page_table].reshape(DB, PAST_LEN, DA_HEADS, DA_DV)
    k_all = jnp.concatenate([past_k.astype(k.dtype), k], axis=1)
    v_all = jnp.concatenate([past_v.astype(v.dtype), v], axis=1)
    q_pos = PAST_LEN + jnp.arange(S)
    k_pos = jnp.arange(PAST_LEN + S)
    return diff_attend(q, k_all, v_all, q_pos, k_pos, lam, rel_bias)


def hier_moe(x, w_rg, b_rg, w_re, b_re, w_gate, w_up, w_down):
    B, T, _ = x.shape
    g_logits = (x @ w_rg).astype(jnp.float32) + b_rg.astype(jnp.float32)
    g_prob = jax.nn.softmax(g_logits, axis=-1)
    g_idx = jnp.argmax(g_logits, axis=-1)
    p_g = jnp.take_along_axis(g_prob, g_idx[..., None], axis=-1)
    e_logits = ((x @ w_re).astype(jnp.float32) + b_re.astype(jnp.float32)).reshape(
        B, T, N_GROUPS, EXPERTS_PER_GROUP)
    e_logits = jnp.take_along_axis(e_logits, g_idx[..., None, None], axis=2)[:, :, 0]
    e_prob = jax.nn.softmax(e_logits, axis=-1)
    top_p, top_i = lax.top_k(e_prob, TOP_K)
    gate = p_g * top_p / jnp.sum(top_p, axis=-1, keepdims=True)
    eid = g_idx[..., None] * EXPERTS_PER_GROUP + top_i
    combine = jnp.sum(jax.nn.one_hot(eid, N_EXPERTS, dtype=jnp.float32) * gate[..., None], axis=-2)
    h = jax.nn.silu(jnp.einsum('btd,edf->btef', x, w_gate)) * jnp.einsum('btd,edf->btef', x, w_up)
    return jnp.einsum('btef,efd->btd', h * combine[..., None].astype(h.dtype), w_down)


def hybrid_layer(x, lb, lam_init, hgrn_core, attn_core,
                 norm1_w, w_in, hg_onorm_w, w_hg_out, q_norm_w, k_norm_w, da_lambda,
                 da_onorm_w, w_da_out, w_o, norm2_w, w_router_group, b_router_group,
                 w_router_expert, b_router_expert, w_gate, w_up, w_down):
    B, T, _ = x.shape
    dt = x.dtype
    z = rmsnorm(x, norm1_w) @ w_in
    hq, hf, hi, hg, dq, dk, dv, ga, gb = split_cols(z)
    q_a = jax.nn.silu(hq.astype(jnp.float32)).reshape(B, T, HG_HEADS, HG_DK)
    f = lb + (1.0 - lb) * jax.nn.sigmoid(hf.astype(jnp.float32))
    k_a = (1.0 - f).reshape(B, T, HG_HEADS, HG_DK)
    logf = jnp.log(f).reshape(B, T, HG_HEADS, HG_DK)
    v_a = hi.astype(jnp.float32).reshape(B, T, HG_HEADS, HG_DV)
    o_a, s_new = hgrn_core(q_a, k_a, v_a, logf)
    o_a = rmsnorm(o_a, hg_onorm_w) * jax.nn.silu(hg.astype(jnp.float32)).reshape(B, T, HG_HEADS, HG_DV)
    y_a = o_a.reshape(B, T, HG_WV).astype(dt) @ w_hg_out
    q_b = rmsnorm(dq.reshape(B, T, DA_HEADS, 2, DA_DH), q_norm_w)
    k_b = rmsnorm(dk.reshape(B, T, DA_HEADS, 2, DA_DH), k_norm_w)
    v_b = dv.reshape(B, T, DA_HEADS, DA_DV)
    lp = da_lambda.astype(jnp.float32)
    lam = jnp.exp(jnp.sum(lp[0] * lp[1])) - jnp.exp(jnp.sum(lp[2] * lp[3])) + lam_init
    o_b = attn_core(q_b, k_b, v_b, lam)
    o_b = rmsnorm(o_b, da_onorm_w) * (1.0 - lam_init)
    y_b = o_b.reshape(B, T, DA_WV).astype(dt) @ w_da_out
    merged = jax.nn.sigmoid(ga) * y_a + jax.nn.sigmoid(gb) * y_b
    h = x + merged @ w_o
    out = h + hier_moe(rmsnorm(h, norm2_w), w_router_group, b_router_group,
                       w_router_expert, b_router_expert, w_gate, w_up, w_down)
    return out, k_b.reshape(B, T, DA_HEADS, 2 * DA_DH), v_b, s_new


def setup_inputs(seed: int = 0) -> dict:
    key = jax.random.key(seed)
    ks = jax.random.split(key, 32)
    n_pages = PAST_LEN // PAGE_SIZE
    n_used = DEC_BATCH * n_pages
    n_pool = n_used + n_used // 4
    f32 = jnp.float32

    def nrm(k, shape, scale):
        return jax.random.normal(k, shape, f32) * scale

    page_table = jax.random.permutation(ks[5], n_pool)[:n_used].reshape(DEC_BATCH, n_pages).astype(jnp.int32)
    return {
        "x_prompt": nrm(ks[0], (BATCH, SEQ, D_MODEL), 1.0),
        "x_sample": nrm(ks[1], (DEC_BATCH, DEC_SEQ, D_MODEL), 1.0),
        "cache_k": nrm(ks[2], (DEPTH, n_pool, PAGE_SIZE, DA_HEADS, 2 * DA_DH), 1.0),
        "cache_v": nrm(ks[3], (DEPTH, n_pool, PAGE_SIZE, DA_HEADS, DA_DV), 1.0),
        "state_hgrn": nrm(ks[4], (DEPTH, DEC_BATCH, HG_HEADS, HG_DK, HG_DV), 1.0),
        "page_table": page_table,
        "meta_tokens": nrm(ks[6], (N_META, D_MODEL), 1.0),
        "rel_bias": nrm(ks[7], (N_BUCKETS, DA_HEADS), 0.5),
        "hg_lb_logits": nrm(ks[8], (DEPTH + 1, HG_WK), 0.5),
        "norm1_w": 1.0 + nrm(ks[9], (DEPTH, D_MODEL), 0.1),
        "w_in": nrm(ks[10], (DEPTH, D_MODEL, IN_WIDTH), D_MODEL ** -0.5),
        "hg_onorm_w": 1.0 + nrm(ks[11], (DEPTH, HG_DV), 0.1),
        "w_hg_out": nrm(ks[12], (DEPTH, HG_WV, D_MODEL), HG_WV ** -0.5),
        "q_norm_w": 1.0 + nrm(ks[13], (DEPTH, DA_DH), 0.1),
        "k_norm_w": 1.0 + nrm(ks[14], (DEPTH, DA_DH), 0.1),
        "da_lambda": nrm(ks[15], (DEPTH, 4, DA_DH), 0.1),
        "da_onorm_w": 1.0 + nrm(ks[16], (DEPTH, DA_DV), 0.1),
        "w_da_out": nrm(ks[17], (DEPTH, DA_WV, D_MODEL), DA_WV ** -0.5),
        "w_o": nrm(ks[18], (DEPTH, D_MODEL, D_MODEL), D_MODEL ** -0.5),
        "norm2_w": 1.0 + nrm(ks[19], (DEPTH, D_MODEL), 0.1),
        "w_router_group": nrm(ks[20], (DEPTH, D_MODEL, N_GROUPS), D_MODEL ** -0.5),
        "b_router_group": nrm(ks[21], (DEPTH, N_GROUPS), 0.01),
        "w_router_expert": nrm(ks[22], (DEPTH, D_MODEL, N_EXPERTS), D_MODEL ** -0.5),
        "b_router_expert": nrm(ks[23], (DEPTH, N_EXPERTS), 0.01),
        "w_gate": nrm(ks[24], (DEPTH, N_EXPERTS, D_MODEL, EXPERT_FF), D_MODEL ** -0.5),
        "w_up": nrm(ks[25], (DEPTH, N_EXPERTS, D_MODEL, EXPERT_FF), D_MODEL ** -0.5),
        "w_down": nrm(ks[26], (DEPTH, N_EXPERTS, EXPERT_FF, D_MODEL), EXPERT_FF ** -0.5),
    }


def reference(x_prompt, x_sample, cache_k, cache_v, state_hgrn, page_table, meta_tokens, rel_bias,
              hg_lb_logits, norm1_w, w_in, hg_onorm_w, w_hg_out, q_norm_w, k_norm_w, da_lambda,
              da_onorm_w, w_da_out, w_o, norm2_w, w_router_group, b_router_group,
              w_router_expert, b_router_expert, w_gate, w_up, w_down):
    lb_all = jnp.cumsum(jax.nn.softmax(hg_lb_logits.astype(jnp.float32), axis=0), axis=0)
    xp = jnp.concatenate([jnp.broadcast_to(meta_tokens.astype(x_prompt.dtype)[None],
                                           (x_prompt.shape[0], N_META, D_MODEL)), x_prompt], axis=1)
    xs = x_sample
    kp_rows, vp_rows, sp_rows, ks_rows, vs_rows, ss_rows = [], [], [], [], [], []
    for l in range(DEPTH):
        lam_init = 0.8 - 0.6 * math.exp(-0.3 * l)
        lw = (norm1_w[l], w_in[l], hg_onorm_w[l], w_hg_out[l], q_norm_w[l], k_norm_w[l], da_lambda[l],
              da_onorm_w[l], w_da_out[l], w_o[l], norm2_w[l], w_router_group[l], b_router_group[l],
              w_router_expert[l], b_router_expert[l], w_gate[l], w_up[l], w_down[l])
        xp, kp, vp, sp = hybrid_layer(xp, lb_all[l], lam_init, hgrn_prompt,
                                      functools.partial(prompt_attn, rel_bias=rel_bias), *lw)
        xs, kk, vv, ss = hybrid_layer(xs, lb_all[l], lam_init,
                                      functools.partial(hgrn_sample, state=state_hgrn[l]),
                                      functools.partial(sample_attn, ck=cache_k[l], cv=cache_v[l],
                                                        page_table=page_table, rel_bias=rel_bias), *lw)
        kp_rows.append(kp); vp_rows.append(vp); sp_rows.append(sp)
        ks_rows.append(kk); vs_rows.append(vv); ss_rows.append(ss)
    y_prompt = xp[:, N_META:]
    y_sample = xs
    new_k_prompt = jnp.stack(kp_rows)
    new_v_prompt = jnp.stack(vp_rows)
    new_state_prompt = jnp.stack(sp_rows)
    new_k_sample = jnp.stack(ks_rows)
    new_v_sample = jnp.stack(vs_rows)
    new_state_sample = jnp.stack(ss_rows)
    return (y_prompt, y_sample, new_k_prompt, new_v_prompt, new_state_prompt, new_k_sample, new_v_sample, new_state_sample)
```

```python
import functools
import math

import jax
import jax.numpy as jnp
from jax import lax
from jax.experimental import pallas as pl
from jax.experimental.pallas import tpu as pltpu

F32 = jnp.float32
BF16 = jnp.bfloat16

EPS = 1e-6
N_META = 16
HEADS = 4
HEAD_W = 128
DA_DH = 64
N_BUCKETS = 32
MAX_DISTANCE = 128
N_GROUPS = 4
EXPERTS_PER_GROUP = 4
N_EXPERTS = N_GROUPS * EXPERTS_PER_GROUP
MASK_VALUE = -1e30
HG_CHUNK = 128
HG_BASE = 16
ATT_BLOCK = 256
PAGES_PER_STEP = 8
VMEM_LIMIT = 56 * 1024 * 1024


def _dot(a, b):
    return jnp.dot(a, b, preferred_element_type=F32)


def _dot_nt(a, b):
    return lax.dot_general(a, b, (((1,), (1,)), ((), ())), preferred_element_type=F32)


def _dot_tn(a, b):
    return lax.dot_general(a, b, (((0,), (0,)), ((), ())), preferred_element_type=F32)


def _sigmoid(x):
    return 1.0 / (1.0 + jnp.exp(-x))


def _rms_rows(x, w):
    return x * lax.rsqrt(jnp.mean(x * x, axis=-1, keepdims=True) + EPS) * w


def _const_spec(shape):
    zeros = (0,) * len(shape)
    return pl.BlockSpec(shape, lambda *_: zeros)


def _inproj_kernel(x_ref, n1_ref, w_ref, g_ref, qw_ref, kw_ref,
                   zh_ref, q_ref, kf_ref, kb_ref, vf_ref, vb_ref):
    xn = _rms_rows(x_ref[...], n1_ref[...]).astype(BF16)
    zh_ref[...] = _dot(xn, w_ref[:, 0:2048])
    g = g_ref[...]

    def group_norm(z, w):
        sq = z * z
        hi = sq.astype(BF16)
        lo = (sq - hi.astype(F32)).astype(BF16)
        ms = _dot(hi, g) + _dot(lo, g)
        return z * lax.rsqrt(ms + EPS) * w

    q_ref[...] = group_norm(_dot(xn, w_ref[:, 2048:2560]), qw_ref[...]).astype(BF16)
    kn = group_norm(_dot(xn, w_ref[:, 2560:3072]), kw_ref[...])
    kf_ref[...] = kn
    kb_ref[...] = kn.astype(BF16)
    zv = _dot(xn, w_ref[:, 3072:3584])
    vf_ref[...] = zv
    vb_ref[...] = zv.astype(BF16)


def _inproj(x2d, n1, w_a, gmat, qw, kw, tm):
    n, d = x2d.shape
    row = lambda width: pl.BlockSpec((tm, width), lambda i: (i, 0))
    return pl.pallas_call(
        _inproj_kernel,
        grid=(n // tm,),
        in_specs=[row(d), _const_spec((1, d)), _const_spec(w_a.shape), _const_spec(gmat.shape),
                  _const_spec((1, 512)), _const_spec((1, 512))],
        out_specs=[row(2048), row(512), row(512), row(512), row(512), row(512)],
        out_shape=[jax.ShapeDtypeStruct((n, 2048), F32), jax.ShapeDtypeStruct((n, 512), BF16),
                   jax.ShapeDtypeStruct((n, 512), F32), jax.ShapeDtypeStruct((n, 512), BF16),
                   jax.ShapeDtypeStruct((n, 512), F32), jax.ShapeDtypeStruct((n, 512), BF16)],
        compiler_params=pltpu.CompilerParams(dimension_semantics=("parallel",),
                                             vmem_limit_bytes=VMEM_LIMIT),
        name="inproj",
    )(x2d, n1, w_a, gmat, qw, kw)


def _hgrn_kernel(z_ref, lbl_ref, onw_ref, s0_ref, o_ref, sout_ref, st_scr, *, n_valid):
    C = HG_CHUNK
    W = HEADS * HEAD_W
    c = pl.program_id(1)

    @pl.when(c == 0)
    def _():
        for h in range(HEADS):
            st_scr[h] = s0_ref[0, h].T

    z = z_ref[0]
    if n_valid < C:
        z = jnp.concatenate([z, jnp.zeros((C - n_valid, z.shape[1]), F32)], axis=0)
    lg = lbl_ref[...]
    e = jnp.exp(lg - jnp.max(lg, axis=0, keepdims=True))
    lb = e[0:1] / jnp.sum(e, axis=0, keepdims=True)

    hq, hf, hi, hg = z[:, 0:W], z[:, W:2 * W], z[:, 2 * W:3 * W], z[:, 3 * W:4 * W]
    q = hq * _sigmoid(hq)
    f = lb + (1.0 - lb) * _sigmoid(hf)
    k = 1.0 - f
    logf = jnp.log(f)
    if n_valid < C:
        valid = lax.broadcasted_iota(jnp.int32, (C, W), 0) < n_valid
        k = jnp.where(valid, k, 0.0)
        logf = jnp.where(valid, logf, 0.0)
    v_bf = hi.astype(BF16)

    row = lax.broadcasted_iota(jnp.int32, (C, C), 0)
    col = lax.broadcasted_iota(jnp.int32, (C, C), 1)
    tri = jnp.where(col <= row, 1.0, 0.0).astype(BF16)
    x1 = logf.astype(BF16)
    r1 = logf - x1.astype(F32)
    x2 = r1.astype(BF16)
    x3 = (r1 - x2.astype(F32)).astype(BF16)
    b = _dot(tri, x1) + _dot(tri, x2) + _dot(tri, x3)
    b_last = b[C - 1:C]

    def block_row(width, r):
        b3 = b.reshape(C // width, width, W)
        return jnp.broadcast_to(b3[:, r:r + 1, :], b3.shape).reshape(C, W)

    def next_block_first(width):
        b3 = b.reshape(C // width, width, W)
        nxt = jnp.concatenate([b3[1:, 0:1, :], b_last[None]], axis=0)
        return jnp.broadcast_to(nxt, b3.shape).reshape(C, W)

    def shr(x, width):
        return lax.shift_right_logical(x, int(math.log2(width)))

    mid = block_row(HG_BASE, HG_BASE // 2)
    levels = [((q * jnp.exp(b - mid)).astype(BF16), (k * jnp.exp(mid - b)).astype(BF16),
               (shr(row, HG_BASE) == shr(col, HG_BASE)) & (col <= row))]
    width = HG_BASE
    while width < C:
        rq = block_row(width, 0)
        rk = next_block_first(width)
        levels.append(((q * jnp.exp(b - rq)).astype(BF16), (k * jnp.exp(rk - b)).astype(BF16),
                       (shr(row, width) == shr(col, width) + 1)
                       & (shr(row, 2 * width) == shr(col, 2 * width))))
        width *= 2

    qb = (q * jnp.exp(b)).astype(BF16)
    kd = (k * jnp.exp(b_last - b)).astype(BF16)
    decay = jnp.exp(b_last)
    gate = hg * _sigmoid(hg)
    onw = onw_ref[...]

    for h in range(HEADS):
        hs = slice(h * HEAD_W, (h + 1) * HEAD_W)
        st = st_scr[h]
        scores = jnp.zeros((C, C), F32)
        for qe, ke, mask in levels:
            scores = jnp.where(mask, _dot_nt(qe[:, hs], ke[:, hs]), scores)
        o = _dot_nt(qb[:, hs], st.astype(BF16)) + _dot(scores.astype(BF16), v_bf[:, hs])
        st_scr[h] = st * decay[:, hs] + _dot_tn(v_bf[:, hs], kd[:, hs])
        on = _rms_rows(o, onw[:, hs]) * gate[:, hs]
        o_ref[0, :, hs] = on[:n_valid].astype(o_ref.dtype)

    @pl.when(c == pl.num_programs(1) - 1)
    def _():
        for h in range(HEADS):
            sout_ref[0, h] = st_scr[h].T


def _hgrn(zh, lbl, onw, s0, *, shared_state, out_dtype):
    nseq, t, _ = zh.shape
    rows = min(t, HG_CHUNK)
    assert t % rows == 0 and rows % 8 == 0
    s_map = (lambda b, c: (0, 0, 0, 0)) if shared_state else (lambda b, c: (b, 0, 0, 0))
    return pl.pallas_call(
        functools.partial(_hgrn_kernel, n_valid=rows),
        grid=(nseq, t // rows),
        in_specs=[pl.BlockSpec((1, rows, 2048), lambda b, c: (b, c, 0)),
                  _const_spec(lbl.shape), _const_spec((1, 512)),
                  pl.BlockSpec((1, HEADS, HEAD_W, HEAD_W), s_map)],
        out_specs=[pl.BlockSpec((1, rows, 512), lambda b, c: (b, c, 0)),
                   pl.BlockSpec((1, HEADS, HEAD_W, HEAD_W), lambda b, c: (b, 0, 0, 0))],
        out_shape=[jax.ShapeDtypeStruct((nseq, t, 512), out_dtype),
                   jax.ShapeDtypeStruct((nseq, HEADS, HEAD_W, HEAD_W), F32)],
        scratch_shapes=[pltpu.VMEM((HEADS, HEAD_W, HEAD_W), F32)],
        compiler_params=pltpu.CompilerParams(dimension_semantics=("parallel", "arbitrary"),
                                             vmem_limit_bytes=VMEM_LIMIT),
        name="hgrn",
    )(zh, lbl, onw, s0)


def _lambda_value(lp, lam_init):
    s1 = jnp.sum(lp[0:1] * lp[1:2], axis=1, keepdims=True)
    s2 = jnp.sum(lp[2:3] * lp[3:4], axis=1, keepdims=True)
    return jnp.exp(s1) - jnp.exp(s2) + lam_init


def _softmax_update(s, v_bf, m_ref, l_ref, acc_ref, first):
    m_cur = jnp.max(s, axis=1, keepdims=True)
    if first:
        m_new = m_cur
        p = jnp.exp(s - m_new)
        l_ref[...] = jnp.sum(p, axis=1, keepdims=True)
        acc_ref[...] = _dot(p.astype(BF16), v_bf)
    else:
        m_prev = m_ref[...]
        m_new = jnp.maximum(m_prev, m_cur)
        alpha = jnp.exp(m_prev - m_new)
        p = jnp.exp(s - m_new)
        l_ref[...] = alpha * l_ref[...] + jnp.sum(p, axis=1, keepdims=True)
        acc_ref[...] = alpha * acc_ref[...] + _dot(p.astype(BF16), v_bf)
    m_ref[...] = m_new


def _diff_out(a1, a2, lam, onw, lam_init):
    o = a1 - lam * a2
    return _rms_rows(o, onw) * (1.0 - lam_init)


def _pattn_kernel(q_ref, k_ref, v_ref, km_ref, vm_ref, tb_ref, tm_ref, lam_ref, onw_ref, o_ref,
                  qz_scr, m_scr, l_scr, acc_scr, *, lam_init):
    blk = ATT_BLOCK
    i = pl.program_id(1)
    lane = lax.broadcasted_iota(jnp.int32, (blk, HEAD_W), 1)
    for h in range(HEADS):
        qh = q_ref[0, :, h * HEAD_W:(h + 1) * HEAD_W]
        zero = jnp.zeros_like(qh)
        qz_scr[h] = jnp.concatenate([jnp.where(lane < DA_DH, qh, zero),
                                     jnp.where(lane >= DA_DH, qh, zero)], axis=0)

    def update(h, k_bf, v_bf, bias, first=False):
        s = _dot_nt(qz_scr[h], k_bf)
        if bias is not None:
            s = s + bias
        _softmax_update(s, v_bf, m_scr.at[h], l_scr.at[h], acc_scr.at[h], first)

    def key_block(j, table):
        off = pl.multiple_of(j * blk, blk)
        for h in range(HEADS):
            hs = slice(h * HEAD_W, (h + 1) * HEAD_W)
            update(h, k_ref[0, pl.ds(off, blk), hs], v_ref[0, pl.ds(off, blk), hs],
                   None if table is None else tb_ref[table, h])

    meta_sel = jnp.minimum(i, 1)
    for h in range(HEADS):
        hs = slice(h * HEAD_W, (h + 1) * HEAD_W)
        update(h, km_ref[:, hs], vm_ref[:, hs], tm_ref[meta_sel, h], first=True)

    def far_body(j, carry):
        key_block(j, None)
        return carry

    lax.fori_loop(0, jnp.maximum(i - 1, 0), far_body, 0)

    @pl.when(i >= 1)
    def _():
        key_block(i - 1, 1)

    key_block(i, 0)

    lam = _lambda_value(lam_ref[...], lam_init)
    onw = onw_ref[...]
    for h in range(HEADS):
        a = acc_scr[h] / l_scr[h]
        o_ref[0, :, h * HEAD_W:(h + 1) * HEAD_W] = _diff_out(
            a[:blk], a[blk:], lam, onw, lam_init).astype(o_ref.dtype)


def _pattn(q, k, v, km, vm, tb, tmeta, lam_p, onw, lam_init):
    bsz, t, w = q.shape
    blk = ATT_BLOCK
    assert t % blk == 0
    kv_spec = pl.BlockSpec((1, t, w), lambda b, i: (b, 0, 0))
    return pl.pallas_call(
        functools.partial(_pattn_kernel, lam_init=lam_init),
        grid=(bsz, t // blk),
        in_specs=[pl.BlockSpec((1, blk, w), lambda b, i: (b, i, 0)), kv_spec, kv_spec,
                  _const_spec(km.shape), _const_spec(vm.shape), _const_spec(tb.shape),
                  _const_spec(tmeta.shape), _const_spec(lam_p.shape), _const_spec(onw.shape)],
        out_specs=pl.BlockSpec((1, blk, w), lambda b, i: (b, i, 0)),
        out_shape=jax.ShapeDtypeStruct((bsz, t, w), BF16),
        scratch_shapes=[pltpu.VMEM((HEADS, 2 * blk, HEAD_W), BF16),
                        pltpu.VMEM((HEADS, 2 * blk, 1), F32),
                        pltpu.VMEM((HEADS, 2 * blk, 1), F32),
                        pltpu.VMEM((HEADS, 2 * blk, HEAD_W), F32)],
        compiler_params=pltpu.CompilerParams(dimension_semantics=("parallel", "arbitrary"),
                                             vmem_limit_bytes=VMEM_LIMIT),
        name="prompt_attn",
    )(q, k, v, km, vm, tb, tmeta, lam_p, onw)


def _sattn_kernel(pt_ref, qr_ref, kn_ref, vn_ref, tl_ref, tn_ref, lam_ref, onw_ref, *rest,
                  lam_init, n_tok):
    del pt_ref
    pg = PAGES_PER_STEP
    k_refs, v_refs = rest[:pg], rest[pg:2 * pg]
    o_ref, m_scr, l_scr, acc_scr = rest[2 * pg:]
    g = pl.program_id(1)
    qr = qr_ref[0]

    @pl.when(g == 0)
    def _():
        s = _dot_nt(qr, kn_ref[0]) + tn_ref[...]
        _softmax_update(s, vn_ref[0], m_scr, l_scr, acc_scr, True)

    is_last = g == pl.num_programs(1) - 1
    parts = []
    for u in range(pg):
        s = _dot_nt(qr, k_refs[u][0].astype(BF16))
        if u == pg - 1:
            s = s + jnp.where(is_last, tl_ref[...], 0.0)
        parts.append(s)
    s_all = jnp.concatenate(parts, axis=1)
    v_all = jnp.concatenate([v_refs[u][0].astype(BF16) for u in range(pg)], axis=0)
    _softmax_update(s_all, v_all, m_scr, l_scr, acc_scr, False)

    @pl.when(is_last)
    def _():
        lam = _lambda_value(lam_ref[...], lam_init)
        onw = onw_ref[...]
        a = acc_scr[...] / l_scr[...]
        for h in range(HEADS):
            hs = slice(h * HEAD_W, (h + 1) * HEAD_W)
            r1 = (2 * h) * n_tok
            r2 = (2 * h + 1) * n_tok
            o_ref[0, :, hs] = _diff_out(a[r1:r1 + n_tok, hs], a[r2:r2 + n_tok, hs], lam, onw,
                                        lam_init).astype(o_ref.dtype)


def _sattn(page_table, qrows, knew, vnew, tlast, tnew, lam_p, onw, ck, cv, lam_init, n_tok):
    nb, n_pages = page_table.shape
    pg = PAGES_PER_STEP
    assert n_pages % pg == 0
    nrow = qrows.shape[1]
    psize, w = ck.shape[1], ck.shape[2]
    pt_flat = page_table.reshape(-1)

    def page_spec(u):
        return pl.BlockSpec((1, psize, w), lambda b, g, pt: (pt[b * n_pages + g * pg + u], 0, 0))

    per_b = lambda shape: pl.BlockSpec((1,) + shape, lambda b, g, pt: (b, 0, 0))
    const = lambda shape: pl.BlockSpec(shape, lambda b, g, pt: (0,) * len(shape))
    grid_spec = pltpu.PrefetchScalarGridSpec(
        num_scalar_prefetch=1,
        grid=(nb, n_pages // pg),
        in_specs=[per_b((nrow, w)), per_b((psize, w)), per_b((psize, w)),
                  const(tlast.shape), const(tnew.shape), const(lam_p.shape), const(onw.shape)]
                 + [page_spec(u) for u in range(pg)] + [page_spec(u) for u in range(pg)],
        out_specs=per_b((n_tok, w)),
        scratch_shapes=[pltpu.VMEM((nrow, 1), F32), pltpu.VMEM((nrow, 1), F32),
                        pltpu.VMEM((nrow, w), F32)],
    )
    return pl.pallas_call(
        functools.partial(_sattn_kernel, lam_init=lam_init, n_tok=n_tok),
        grid_spec=grid_spec,
        out_shape=jax.ShapeDtypeStruct((nb, n_tok, w), F32),
        compiler_params=pltpu.CompilerParams(dimension_semantics=("parallel", "arbitrary"),
                                             vmem_limit_bytes=VMEM_LIMIT),
        name="sample_attn",
    )(pt_flat, qrows, knew, vnew, tlast, tnew, lam_p, onw, *([ck] * pg), *([cv] * pg))


def _merge_kernel(x_ref, oa_ref, ob_ref, n1_ref, wg_ref, wha_ref, wda_ref, wo_ref, h_ref):
    x = x_ref[...]
    d = x.shape[1]
    xn = _rms_rows(x, n1_ref[...]).astype(BF16)
    gates = _dot(xn, wg_ref[...])
    ya = _dot(oa_ref[...], wha_ref[...])
    yb = _dot(ob_ref[...], wda_ref[...])
    merged = _sigmoid(gates[:, :d]) * ya + _sigmoid(gates[:, d:]) * yb
    h_ref[...] = x + _dot(merged.astype(BF16), wo_ref[...])


def _merge(x2d, oa, ob, n1, w_g, w_ha, w_da, w_o, tm):
    n, d = x2d.shape
    row = lambda width: pl.BlockSpec((tm, width), lambda i: (i, 0))
    return pl.pallas_call(
        _merge_kernel,
        grid=(n // tm,),
        in_specs=[row(d), row(512), row(512), _const_spec((1, d)), _const_spec(w_g.shape),
                  _const_spec(w_ha.shape), _const_spec(w_da.shape), _const_spec(w_o.shape)],
        out_specs=row(d),
        out_shape=jax.ShapeDtypeStruct((n, d), F32),
        compiler_params=pltpu.CompilerParams(dimension_semantics=("parallel",),
                                             vmem_limit_bytes=VMEM_LIMIT),
        name="merge",
    )(x2d, oa, ob, n1, w_g, w_ha, w_da, w_o)


def _moe_kernel(h_ref, n2_ref, wr_hi_ref, wr_lo_ref, br_ref, wgu_ref, wd_ref, y_ref):
    h = h_ref[...]
    tm = h.shape[0]
    hn = _rms_rows(h, n2_ref[...])
    hn_hi = hn.astype(BF16)
    hn_lo = (hn - hn_hi.astype(F32)).astype(BF16)
    logits = (_dot(hn_hi, wr_hi_ref[...]) + _dot(hn_lo, wr_hi_ref[...])
              + _dot(hn_hi, wr_lo_ref[...]) + br_ref[...])
    lane = lax.broadcasted_iota(jnp.int32, logits.shape, 1)
    big = jnp.int32(1 << 20)
    ninf = -jnp.inf

    def first_lane(mask):
        return jnp.min(jnp.where(mask, lane, big), axis=1, keepdims=True)

    is_g = lane < N_GROUPS
    g_max = jnp.max(jnp.where(is_g, logits, ninf), axis=1, keepdims=True)
    g_idx = first_lane(is_g & (logits == g_max))
    g_sum = jnp.sum(jnp.where(is_g, jnp.exp(logits - g_max), 0.0), axis=1, keepdims=True)
    p_g = 1.0 / g_sum
    e_lane = lane - N_GROUPS
    in_group = ((e_lane >= 0) & (e_lane < N_EXPERTS)
                & (lax.shift_right_arithmetic(e_lane, 2) == g_idx))
    e_max = jnp.max(jnp.where(in_group, logits, ninf), axis=1, keepdims=True)
    e_exp = jnp.where(in_group, jnp.exp(logits - e_max), 0.0)
    e_prob = e_exp / jnp.sum(e_exp, axis=1, keepdims=True)
    p1 = jnp.max(jnp.where(in_group, e_prob, -1.0), axis=1, keepdims=True)
    i1 = first_lane(in_group & (e_prob == p1))
    rest = in_group & (lane != i1)
    p2 = jnp.max(jnp.where(rest, e_prob, -1.0), axis=1, keepdims=True)
    i2 = first_lane(rest & (e_prob == p2))
    denom = p1 + p2
    combine = (jnp.where(lane == i1, p_g * p1 / denom, 0.0)
               + jnp.where(lane == i2, p_g * p2 / denom, 0.0))

    ff = wd_ref.shape[1]
    acc = jnp.zeros_like(h)
    for ex in range(N_EXPERTS):
        gu = _dot(hn_hi, wgu_ref[ex])
        gt, up = gu[:, :ff], gu[:, ff:]
        hh = gt * _sigmoid(gt) * up * combine[:, N_GROUPS + ex:N_GROUPS + ex + 1]
        acc = acc + _dot(hh.astype(BF16), wd_ref[ex])
    y_ref[...] = h + acc


def _moe(h2d, n2, wr_hi, wr_lo, br, wgu, wd, tm):
    n, d = h2d.shape
    row = pl.BlockSpec((tm, d), lambda i: (i, 0))
    single = lambda shape: pl.BlockSpec(shape, lambda i: (0,) * len(shape),
                                        pipeline_mode=pl.Buffered(1))
    return pl.pallas_call(
        _moe_kernel,
        grid=(n // tm,),
        in_specs=[row, _const_spec((1, d)), _const_spec(wr_hi.shape), _const_spec(wr_lo.shape),
                  _const_spec(br.shape), single(wgu.shape), single(wd.shape)],
        out_specs=row,
        out_shape=jax.ShapeDtypeStruct((n, d), F32),
        compiler_params=pltpu.CompilerParams(dimension_semantics=("parallel",),
                                             vmem_limit_bytes=VMEM_LIMIT),
        name="moe",
    )(h2d, n2, wr_hi, wr_lo, br, wgu, wd)


def _t5_bias(dist, rel_bias):
    n = jnp.maximum(dist, 0)
    max_exact = N_BUCKETS // 2
    nf = jnp.maximum(n, 1).astype(F32)
    large = max_exact + (jnp.log(nf / max_exact) / math.log(MAX_DISTANCE / max_exact)
                         * (N_BUCKETS - max_exact)).astype(jnp.int32)
    large = jnp.minimum(large, N_BUCKETS - 1)
    bucket = jnp.where(n < max_exact, n, large)
    rb = rel_bias.astype(F32)
    bias = jnp.moveaxis(rb[bucket], -1, 0)
    far = rb[N_BUCKETS - 1].reshape((HEADS,) + (1,) * dist.ndim)
    return jnp.where(dist >= 0, bias - far, MASK_VALUE)


def kernel(x_prompt, x_sample, cache_k, cache_v, state_hgrn, page_table, meta_tokens, rel_bias,
           hg_lb_logits, norm1_w, w_in, hg_onorm_w, w_hg_out, q_norm_w, k_norm_w, da_lambda,
           da_onorm_w, w_da_out, w_o, norm2_w, w_router_group, b_router_group,
           w_router_expert, b_router_expert, w_gate, w_up, w_down):
    bsz, seq, d = x_prompt.shape
    nb, n_tok, _ = x_sample.shape
    depth = w_in.shape[0]
    assert depth == 1 and hg_lb_logits.shape[0] == 2
    n_pages = page_table.shape[1]
    psize = cache_k.shape[2]
    past_len = n_pages * psize
    blk = ATT_BLOCK
    assert blk >= MAX_DISTANCE and psize >= MAX_DISTANCE and n_tok <= 8
    lam_init = 0.8 - 0.6 * math.exp(-0.3 * 0)
    w_attn = HEADS * HEAD_W

    w_in0 = w_in[0]
    w_a = w_in0[:, :3584].astype(BF16)
    w_g = w_in0[:, 3584:].astype(BF16)
    n1 = norm1_w[0].reshape(1, d)
    n2 = norm2_w[0].reshape(1, d)
    qw = (jnp.tile(q_norm_w[0], 2 * HEADS) * (DA_DH ** -0.5)).reshape(1, w_attn)
    kw = jnp.tile(k_norm_w[0], 2 * HEADS).reshape(1, w_attn)
    gi = jnp.arange(w_attn) // DA_DH
    gmat = jnp.where(gi[:, None] == gi[None, :], 1.0 / DA_DH, 0.0).astype(BF16)
    hg_onw = jnp.tile(hg_onorm_w[0], HEADS).reshape(1, w_attn)
    da_onw = da_onorm_w[0].reshape(1, HEAD_W)
    lam_p = da_lambda[0].astype(F32)
    w_ha = w_hg_out[0].astype(BF16)
    w_da = w_da_out[0].astype(BF16)
    w_o0 = w_o[0].astype(BF16)
    wr = jnp.concatenate([w_router_group[0], w_router_expert[0],
                          jnp.zeros((d, 128 - N_GROUPS - N_EXPERTS), F32)], axis=1)
    wr_hi = wr.astype(BF16)
    wr_lo = (wr - wr_hi.astype(F32)).astype(BF16)
    br = jnp.concatenate([b_router_group[0], b_router_expert[0],
                          jnp.zeros((128 - N_GROUPS - N_EXPERTS,), F32)]).reshape(1, 128)
    wgu = jnp.concatenate([w_gate[0], w_up[0]], axis=2).astype(BF16)
    wd = w_down[0].astype(BF16)

    n_small = N_META + nb * n_tok
    x_small = jnp.concatenate([meta_tokens.astype(F32), x_sample.reshape(nb * n_tok, d)], axis=0)
    zh_s, q_s, kf_s, kb_s, vf_s, vb_s = _inproj(x_small, n1, w_a, gmat, qw, kw, n_small)

    _, s_meta = _hgrn(zh_s[:N_META][None], hg_lb_logits, hg_onw,
                      jnp.zeros((1, HEADS, HEAD_W, HEAD_W), F32), shared_state=False, out_dtype=F32)
    oa_s, s_sample = _hgrn(zh_s[N_META:].reshape(nb, n_tok, 2048), hg_lb_logits, hg_onw,
                           state_hgrn[0], shared_state=False, out_dtype=F32)

    x_p = x_prompt.reshape(bsz * seq, d)
    zh_p, q_p, kf_p, kb_p, vf_p, vb_p = _inproj(x_p, n1, w_a, gmat, qw, kw, 512)
    oa_p, s_prompt = _hgrn(zh_p.reshape(bsz, seq, 2048), hg_lb_logits, hg_onw, s_meta,
                           shared_state=True, out_dtype=BF16)

    r = jnp.arange(blk)
    tb = jnp.stack([_t5_bias(r[:, None] - r[None, :], rel_bias),
                    _t5_bias(blk + r[:, None] - r[None, :], rel_bias)])
    tb = jnp.concatenate([tb, tb], axis=2)
    mcol = jnp.arange(128)
    t_meta0 = _t5_bias(N_META + r[:, None] - mcol[None, :], rel_bias)
    t_meta = jnp.stack([t_meta0, jnp.zeros_like(t_meta0)])
    t_meta = jnp.where(mcol < N_META, t_meta, MASK_VALUE)
    t_meta = jnp.concatenate([t_meta, t_meta], axis=2)
    pad_meta = lambda a: jnp.concatenate([a, jnp.zeros((128 - N_META, w_attn), a.dtype)], axis=0)
    ob_p = _pattn(q_p.reshape(bsz, seq, w_attn), kb_p.reshape(bsz, seq, w_attn),
                  vb_p.reshape(bsz, seq, w_attn), pad_meta(kb_s[:N_META]), pad_meta(vb_s[:N_META]),
                  tb, t_meta, lam_p, da_onw, lam_init)

    q_tok = q_s[N_META:].reshape(nb, n_tok, HEADS, 2, DA_DH)
    eye = jnp.eye(2 * HEADS, dtype=BF16).reshape(HEADS, 2, HEADS, 2)
    qrows = jnp.einsum('bshmd,hmgn->bhmsgnd', q_tok, eye).reshape(nb, 2 * HEADS * n_tok, w_attn)
    pad_new = lambda a: jnp.concatenate(
        [a.reshape(nb, n_tok, w_attn), jnp.zeros((nb, psize - n_tok, w_attn), a.dtype)], axis=1)
    n_rows = 2 * HEADS * n_tok
    s_idx = jnp.tile(jnp.arange(n_tok), 2 * HEADS)
    head_of_row = jnp.repeat(jnp.arange(HEADS), 2 * n_tok)
    key_r = jnp.arange(psize)
    dist_last = (past_len + s_idx[:, None]) - (past_len - psize + key_r[None, :])
    dist_new = jnp.where(key_r[None, :] < n_tok, s_idx[:, None] - key_r[None, :], -1)
    pick = lambda t: t[head_of_row, jnp.arange(n_rows)]
    t_last = pick(_t5_bias(dist_last, rel_bias))
    t_new = pick(_t5_bias(dist_new, rel_bias))
    ob_s = _sattn(page_table, qrows, pad_new(kb_s[N_META:]), pad_new(vb_s[N_META:]), t_last, t_new,
                  lam_p, da_onw, cache_k[0].reshape(-1, psize, w_attn),
                  cache_v[0].reshape(-1, psize, w_attn), lam_init, n_tok)

    h_p = _merge(x_p, oa_p.reshape(bsz * seq, w_attn), ob_p.reshape(bsz * seq, w_attn),
                 n1, w_g, w_ha, w_da, w_o0, 512)
    y_p = _moe(h_p, n2, wr_hi, wr_lo, br, wgu, wd, 512)
    x_s = x_small[N_META:]
    h_s = _merge(x_s, oa_s.reshape(nb * n_tok, w_attn).astype(BF16),
                 ob_s.reshape(nb * n_tok, w_attn).astype(BF16), n1, w_g, w_ha, w_da, w_o0,
                 nb * n_tok)
    y_s = _moe(h_s, n2, wr_hi, wr_lo, br, wgu, wd, nb * n_tok)

    def with_meta(meta_rows, real):
        meta_b = jnp.broadcast_to(meta_rows[None], (bsz, N_META, w_attn))
        full = jnp.concatenate([meta_b, real.reshape(bsz, seq, w_attn)], axis=1)
        return full.reshape(1, bsz, seq + N_META, HEADS, HEAD_W)

    return (y_p.reshape(bsz, seq, d),
            y_s.reshape(nb, n_tok, d),
            with_meta(kf_s[:N_META], kf_p),
            with_meta(vf_s[:N_META], vf_p),
            s_prompt[None],
            kf_s[N_META:].reshape(1, nb, n_tok, HEADS, HEAD_W),
            vf_s[N_META:].reshape(1, nb, n_tok, HEADS, HEAD_W),
            s_sample[None])
```

```python
import functools
import math

import jax
import jax.numpy as jnp
from jax import lax
from jax.experimental import pallas as pl
from jax.experimental.pallas import tpu as pltpu

F32 = jnp.float32
BF16 = jnp.bfloat16

EPS = 1e-6
N_META = 16
HEADS = 4
HEAD_W = 128
DA_DH = 64
N_BUCKETS = 32
MAX_DISTANCE = 128
N_GROUPS = 4
EXPERTS_PER_GROUP = 4
N_EXPERTS = N_GROUPS * EXPERTS_PER_GROUP
MASK_VALUE = -1e30
HG_CHUNK = 128
HG_BASE = 16
ATT_BLOCK = 256
PAGES_PER_STEP = 8
VMEM_LIMIT = 56 * 1024 * 1024


def _dot(a, b):
    return jnp.dot(a, b, preferred_element_type=F32)


def _dot_nt(a, b):
    return lax.dot_general(a, b, (((1,), (1,)), ((), ())), preferred_element_type=F32)


def _dot_tn(a, b):
    return lax.dot_general(a, b, (((0,), (0,)), ((), ())), preferred_element_type=F32)


def _sigmoid(x):
    return 1.0 / (1.0 + jnp.exp(-x))


def _rms_rows(x, w):
    return x * lax.rsqrt(jnp.mean(x * x, axis=-1, keepdims=True) + EPS) * w


def _const_spec(shape):
    zeros = (0,) * len(shape)
    return pl.BlockSpec(shape, lambda *_: zeros)


def _inproj_kernel(x_ref, n1_ref, w_ref, g_ref, qw_ref, kw_ref,
                   zh_ref, q_ref, kf_ref, kb_ref, vf_ref, vb_ref, *, transposed):
    xn = _rms_rows(x_ref[...], n1_ref[...]).astype(BF16)
    zh_ref[...] = _dot(xn, w_ref[:, 0:2048])
    g = g_ref[...]

    def group_norm(z, w):
        sq = z * z
        hi = sq.astype(BF16)
        lo = (sq - hi.astype(F32)).astype(BF16)
        ms = _dot(hi, g) + _dot(lo, g)
        return z * lax.rsqrt(ms + EPS) * w

    def put(ref, val):
        if transposed:
            vt = val.T
            for c in range(ref.shape[0]):
                ref[c] = vt[:, c * ATT_BLOCK:(c + 1) * ATT_BLOCK].astype(BF16)
        else:
            ref[...] = val.astype(BF16)

    put(q_ref, group_norm(_dot(xn, w_ref[:, 2048:2560]), qw_ref[...]))
    kn = group_norm(_dot(xn, w_ref[:, 2560:3072]), kw_ref[...])
    kf_ref[...] = kn
    kb_ref[...] = kn.astype(BF16)
    zv = _dot(xn, w_ref[:, 3072:3584])
    vf_ref[...] = zv
    put(vb_ref, zv)


def _inproj(x2d, n1, w_a, gmat, qw, kw, tm, transposed):
    n, d = x2d.shape
    row = lambda width: pl.BlockSpec((tm, width), lambda i: (i, 0))
    if transposed:
        assert tm % ATT_BLOCK == 0
        per = tm // ATT_BLOCK
        t_spec = pl.BlockSpec((per, 512, ATT_BLOCK), lambda i: (i, 0, 0))
        t_shape = jax.ShapeDtypeStruct((n // ATT_BLOCK, 512, ATT_BLOCK), BF16)
    else:
        t_spec, t_shape = row(512), jax.ShapeDtypeStruct((n, 512), BF16)
    return pl.pallas_call(
        functools.partial(_inproj_kernel, transposed=transposed),
        grid=(n // tm,),
        in_specs=[row(d), _const_spec((1, d)), _const_spec(w_a.shape), _const_spec(gmat.shape),
                  _const_spec((1, 512)), _const_spec((1, 512))],
        out_specs=[row(2048), t_spec, row(512), row(512), row(512), t_spec],
        out_shape=[jax.ShapeDtypeStruct((n, 2048), F32), t_shape,
                   jax.ShapeDtypeStruct((n, 512), F32), jax.ShapeDtypeStruct((n, 512), BF16),
                   jax.ShapeDtypeStruct((n, 512), F32), t_shape],
        compiler_params=pltpu.CompilerParams(dimension_semantics=("parallel",),
                                             vmem_limit_bytes=VMEM_LIMIT),
        name="inproj",
    )(x2d, n1, w_a, gmat, qw, kw)


def _hgrn_kernel(z_ref, lbl_ref, onw_ref, s0_ref, o_ref, sout_ref, st_scr, *, n_valid):
    C = HG_CHUNK
    W = HEADS * HEAD_W
    c = pl.program_id(1)

    @pl.when(c == 0)
    def _():
        for h in range(HEADS):
            st_scr[h] = s0_ref[0, h].T

    z = z_ref[0]
    if n_valid < C:
        z = jnp.concatenate([z, jnp.zeros((C - n_valid, z.shape[1]), F32)], axis=0)
    lg = lbl_ref[...]
    e = jnp.exp(lg - jnp.max(lg, axis=0, keepdims=True))
    lb = e[0:1] / jnp.sum(e, axis=0, keepdims=True)

    hq, hf, hi, hg = z[:, 0:W], z[:, W:2 * W], z[:, 2 * W:3 * W], z[:, 3 * W:4 * W]
    q = hq * _sigmoid(hq)
    f = lb + (1.0 - lb) * _sigmoid(hf)
    k = 1.0 - f
    logf = jnp.log(f)
    if n_valid < C:
        valid = lax.broadcasted_iota(jnp.int32, (C, W), 0) < n_valid
        k = jnp.where(valid, k, 0.0)
        logf = jnp.where(valid, logf, 0.0)
    v_bf = hi.astype(BF16)

    row = lax.broadcasted_iota(jnp.int32, (C, C), 0)
    col = lax.broadcasted_iota(jnp.int32, (C, C), 1)
    tri = jnp.where(col <= row, 1.0, 0.0).astype(BF16)
    x1 = logf.astype(BF16)
    r1 = logf - x1.astype(F32)
    x2 = r1.astype(BF16)
    x3 = (r1 - x2.astype(F32)).astype(BF16)
    b = _dot(tri, x1) + _dot(tri, x2) + _dot(tri, x3)
    b_last = b[C - 1:C]

    def block_row(width, r):
        b3 = b.reshape(C // width, width, W)
        return jnp.broadcast_to(b3[:, r:r + 1, :], b3.shape).reshape(C, W)

    def next_block_first(width):
        b3 = b.reshape(C // width, width, W)
        nxt = jnp.concatenate([b3[1:, 0:1, :], b_last[None]], axis=0)
        return jnp.broadcast_to(nxt, b3.shape).reshape(C, W)

    def shr(x, width):
        return lax.shift_right_logical(x, int(math.log2(width)))

    mid = block_row(HG_BASE, HG_BASE // 2)
    levels = [((q * jnp.exp(b - mid)).astype(BF16), (k * jnp.exp(mid - b)).astype(BF16),
               (shr(row, HG_BASE) == shr(col, HG_BASE)) & (col <= row))]
    width = HG_BASE
    while width < C:
        rq = block_row(width, 0)
        rk = next_block_first(width)
        levels.append(((q * jnp.exp(b - rq)).astype(BF16), (k * jnp.exp(rk - b)).astype(BF16),
                       (shr(row, width) == shr(col, width) + 1)
                       & (shr(row, 2 * width) == shr(col, 2 * width))))
        width *= 2

    qb = (q * jnp.exp(b)).astype(BF16)
    kd = (k * jnp.exp(b_last - b)).astype(BF16)
    decay = jnp.exp(b_last)
    gate = hg * _sigmoid(hg)
    onw = onw_ref[...]

    for h in range(HEADS):
        hs = slice(h * HEAD_W, (h + 1) * HEAD_W)
        st = st_scr[h]
        scores = jnp.zeros((C, C), F32)
        for qe, ke, mask in levels:
            scores = jnp.where(mask, _dot_nt(qe[:, hs], ke[:, hs]), scores)
        o = _dot_nt(qb[:, hs], st.astype(BF16)) + _dot(scores.astype(BF16), v_bf[:, hs])
        st_scr[h] = st * decay[:, hs] + _dot_tn(v_bf[:, hs], kd[:, hs])
        on = _rms_rows(o, onw[:, hs]) * gate[:, hs]
        o_ref[0, :, hs] = on[:n_valid].astype(o_ref.dtype)

    @pl.when(c == pl.num_programs(1) - 1)
    def _():
        for h in range(HEADS):
            sout_ref[0, h] = st_scr[h].T


def _hgrn(zh, lbl, onw, s0, *, shared_state, out_dtype):
    nseq, t, _ = zh.shape
    rows = min(t, HG_CHUNK)
    assert t % rows == 0 and rows % 8 == 0
    s_map = (lambda b, c: (0, 0, 0, 0)) if shared_state else (lambda b, c: (b, 0, 0, 0))
    return pl.pallas_call(
        functools.partial(_hgrn_kernel, n_valid=rows),
        grid=(nseq, t // rows),
        in_specs=[pl.BlockSpec((1, rows, 2048), lambda b, c: (b, c, 0)),
                  _const_spec(lbl.shape), _const_spec((1, 512)),
                  pl.BlockSpec((1, HEADS, HEAD_W, HEAD_W), s_map)],
        out_specs=[pl.BlockSpec((1, rows, 512), lambda b, c: (b, c, 0)),
                   pl.BlockSpec((1, HEADS, HEAD_W, HEAD_W), lambda b, c: (b, 0, 0, 0))],
        out_shape=[jax.ShapeDtypeStruct((nseq, t, 512), out_dtype),
                   jax.ShapeDtypeStruct((nseq, HEADS, HEAD_W, HEAD_W), F32)],
        scratch_shapes=[pltpu.VMEM((HEADS, HEAD_W, HEAD_W), F32)],
        compiler_params=pltpu.CompilerParams(dimension_semantics=("parallel", "arbitrary"),
                                             vmem_limit_bytes=VMEM_LIMIT),
        name="hgrn",
    )(zh, lbl, onw, s0)


def _lambda_value(lp, lam_init):
    s1 = jnp.sum(lp[0:1] * lp[1:2], axis=1, keepdims=True)
    s2 = jnp.sum(lp[2:3] * lp[3:4], axis=1, keepdims=True)
    return jnp.exp(s1) - jnp.exp(s2) + lam_init


def _softmax_update(s, v_bf, m_ref, l_ref, acc_ref, first):
    m_cur = jnp.max(s, axis=1, keepdims=True)
    if first:
        m_new = m_cur
        p = jnp.exp(s - m_new)
        l_ref[...] = jnp.sum(p, axis=1, keepdims=True)
        acc_ref[...] = _dot(p.astype(BF16), v_bf)
    else:
        m_prev = m_ref[...]
        m_new = jnp.maximum(m_prev, m_cur)
        alpha = jnp.exp(m_prev - m_new)
        p = jnp.exp(s - m_new)
        l_ref[...] = alpha * l_ref[...] + jnp.sum(p, axis=1, keepdims=True)
        acc_ref[...] = alpha * acc_ref[...] + _dot(p.astype(BF16), v_bf)
    m_ref[...] = m_new


def _softmax_update_t(s, vt_bf, m_ref, l_ref, acc_ref, first):
    m_cur = jnp.max(s, axis=0, keepdims=True)
    if first:
        m_new = m_cur
        p = jnp.exp(s - m_new)
        l_ref[...] = jnp.sum(p, axis=0, keepdims=True)
        acc_ref[...] = _dot(vt_bf, p.astype(BF16))
    else:
        m_prev = m_ref[...]
        m_new = jnp.maximum(m_prev, m_cur)
        alpha = jnp.exp(m_prev - m_new)
        p = jnp.exp(s - m_new)
        l_ref[...] = alpha * l_ref[...] + jnp.sum(p, axis=0, keepdims=True)
        acc_ref[...] = alpha * acc_ref[...] + _dot(vt_bf, p.astype(BF16))
    m_ref[...] = m_new


def _pattn_kernel(qt_ref, k_ref, vt_ref, km_ref, vmt_ref, tb_ref, tm_ref, lam_ref, onw_ref, o_ref,
                  qz_scr, m_scr, l_scr, acc_scr, *, lam_init):
    blk = ATT_BLOCK
    i = pl.program_id(1)
    sub = lax.broadcasted_iota(jnp.int32, (HEAD_W, blk), 0)
    for h in range(HEADS):
        qh = qt_ref[0, h * HEAD_W:(h + 1) * HEAD_W, :]
        zero = jnp.zeros_like(qh)
        qz_scr[h] = jnp.concatenate([jnp.where(sub < DA_DH, qh, zero),
                                     jnp.where(sub >= DA_DH, qh, zero)], axis=1)

    def update(h, k_bf, vt_bf, bias, first=False):
        s = _dot(k_bf, qz_scr[h])
        if bias is not None:
            s = s + bias
        _softmax_update_t(s, vt_bf, m_scr.at[h], l_scr.at[h], acc_scr.at[h], first)

    def key_block(j, table):
        off = pl.multiple_of(j * blk, blk)
        for h in range(HEADS):
            hs = slice(h * HEAD_W, (h + 1) * HEAD_W)
            update(h, k_ref[0, pl.ds(off, blk), hs], vt_ref[j, hs, :],
                   None if table is None else tb_ref[table, h])

    meta_sel = jnp.minimum(i, 1)
    for h in range(HEADS):
        hs = slice(h * HEAD_W, (h + 1) * HEAD_W)
        update(h, km_ref[:, hs], vmt_ref[hs, :], tm_ref[meta_sel, h], first=True)

    def far_body(j, carry):
        key_block(j, None)
        return carry

    lax.fori_loop(0, jnp.maximum(i - 1, 0), far_body, 0)

    @pl.when(i >= 1)
    def _():
        key_block(i - 1, 1)

    key_block(i, 0)

    lam = _lambda_value(lam_ref[...], lam_init)
    onw = onw_ref[...]
    for h in range(HEADS):
        a = acc_scr[h] / l_scr[h]
        ot = a[:, :blk] - lam * a[:, blk:]
        ot = ot * lax.rsqrt(jnp.mean(ot * ot, axis=0, keepdims=True) + EPS) * onw * (1.0 - lam_init)
        o_ref[0, :, h * HEAD_W:(h + 1) * HEAD_W] = ot.T.astype(o_ref.dtype)


def _pattn(qt, k, vt, km, vmt, tb, tmeta, lam_p, onw_col, lam_init):
    bsz, t, w = k.shape
    blk = ATT_BLOCK
    nblk = t // blk
    assert t % blk == 0
    return pl.pallas_call(
        functools.partial(_pattn_kernel, lam_init=lam_init),
        grid=(bsz, nblk),
        in_specs=[pl.BlockSpec((1, w, blk), lambda b, i: (b * nblk + i, 0, 0)),
                  pl.BlockSpec((1, t, w), lambda b, i: (b, 0, 0)),
                  pl.BlockSpec((nblk, w, blk), lambda b, i: (b, 0, 0)),
                  _const_spec(km.shape), _const_spec(vmt.shape), _const_spec(tb.shape),
                  _const_spec(tmeta.shape), _const_spec(lam_p.shape), _const_spec(onw_col.shape)],
        out_specs=pl.BlockSpec((1, blk, w), lambda b, i: (b, i, 0)),
        out_shape=jax.ShapeDtypeStruct((bsz, t, w), BF16),
        scratch_shapes=[pltpu.VMEM((HEADS, HEAD_W, 2 * blk), BF16),
                        pltpu.VMEM((HEADS, 1, 2 * blk), F32),
                        pltpu.VMEM((HEADS, 1, 2 * blk), F32),
                        pltpu.VMEM((HEADS, HEAD_W, 2 * blk), F32)],
        compiler_params=pltpu.CompilerParams(dimension_semantics=("parallel", "arbitrary"),
                                             vmem_limit_bytes=VMEM_LIMIT),
        name="prompt_attn",
    )(qt, k, vt, km, vmt, tb, tmeta, lam_p, onw_col)


def _sattn_kernel(pt_ref, q_ref, kn_ref, vn_ref, tl_ref, tn_ref, tmask_ref, lam_ref, onw_ref, *rest,
                  lam_init, n_tok):
    del pt_ref
    pg = PAGES_PER_STEP
    k_refs, v_refs = rest[:pg], rest[pg:2 * pg]
    o_ref, m_scr, l_scr, acc_scr = rest[2 * pg:]
    g = pl.program_id(1)
    q = q_ref[0]

    @pl.when(g == 0)
    def _():
        s = _dot_nt(q, kn_ref[0]) + tn_ref[...]
        _softmax_update(s, vn_ref[0], m_scr, l_scr, acc_scr, True)

    is_last = g == pl.num_programs(1) - 1
    tmask = tmask_ref[...]
    parts = []
    for u in range(pg):
        s = _dot_nt(q, k_refs[u][...].astype(BF16))
        parts.append(s + (jnp.where(is_last, tl_ref[...], tmask) if u == pg - 1 else tmask))
    s_all = jnp.concatenate(parts, axis=1)
    v_all = jnp.concatenate([v_refs[u][...].astype(BF16) for u in range(pg)], axis=0)
    _softmax_update(s_all, v_all, m_scr, l_scr, acc_scr, False)

    @pl.when(is_last)
    def _():
        lam = _lambda_value(lam_ref[...], lam_init)
        onw = onw_ref[...]
        a = acc_scr[...] / l_scr[...]
        for h in range(HEADS):
            r1 = (2 * h) * n_tok
            r2 = (2 * h + 1) * n_tok
            o = a[r1:r1 + n_tok] - lam * a[r2:r2 + n_tok]
            o_ref[0, :, h * HEAD_W:(h + 1) * HEAD_W] = (
                _rms_rows(o, onw) * (1.0 - lam_init)).astype(o_ref.dtype)


def _sattn(page_table, qall, knew, vnew, tlast, tnew, tmask, lam_p, onw, ck2, cv2, lam_init, n_tok):
    nb, n_pages = page_table.shape
    pg = PAGES_PER_STEP
    assert n_pages % pg == 0
    nrow = qall.shape[1]
    prow = knew.shape[1]
    pt_flat = page_table.reshape(-1)

    def page_spec(u):
        return pl.BlockSpec((prow, HEAD_W), lambda b, g, pt: (pt[b * n_pages + g * pg + u], 0))

    per_b = lambda shape: pl.BlockSpec((1,) + shape, lambda b, g, pt: (b, 0, 0))
    const = lambda shape: pl.BlockSpec(shape, lambda b, g, pt: (0,) * len(shape))
    grid_spec = pltpu.PrefetchScalarGridSpec(
        num_scalar_prefetch=1,
        grid=(nb, n_pages // pg),
        in_specs=[per_b((nrow, HEAD_W)), per_b((prow, HEAD_W)), per_b((prow, HEAD_W)),
                  const(tlast.shape), const(tnew.shape), const(tmask.shape),
                  const(lam_p.shape), const(onw.shape)]
                 + [page_spec(u) for u in range(pg)] + [page_spec(u) for u in range(pg)],
        out_specs=per_b((n_tok, HEADS * HEAD_W)),
        scratch_shapes=[pltpu.VMEM((nrow, 1), F32), pltpu.VMEM((nrow, 1), F32),
                        pltpu.VMEM((nrow, HEAD_W), F32)],
    )
    return pl.pallas_call(
        functools.partial(_sattn_kernel, lam_init=lam_init, n_tok=n_tok),
        grid_spec=grid_spec,
        out_shape=jax.ShapeDtypeStruct((nb, n_tok, HEADS * HEAD_W), F32),
        compiler_params=pltpu.CompilerParams(dimension_semantics=("parallel", "arbitrary"),
                                             vmem_limit_bytes=VMEM_LIMIT),
        name="sample_attn",
    )(pt_flat, qall, knew, vnew, tlast, tnew, tmask, lam_p, onw, *([ck2] * pg), *([cv2] * pg))


def _merge_kernel(x_ref, oa_ref, ob_ref, n1_ref, wg_ref, wha_ref, wda_ref, wo_ref, h_ref):
    x = x_ref[...]
    d = x.shape[1]
    xn = _rms_rows(x, n1_ref[...]).astype(BF16)
    gates = _dot(xn, wg_ref[...])
    ya = _dot(oa_ref[...], wha_ref[...])
    yb = _dot(ob_ref[...], wda_ref[...])
    merged = _sigmoid(gates[:, :d]) * ya + _sigmoid(gates[:, d:]) * yb
    h_ref[...] = x + _dot(merged.astype(BF16), wo_ref[...])


def _merge(x2d, oa, ob, n1, w_g, w_ha, w_da, w_o, tm):
    n, d = x2d.shape
    row = lambda width: pl.BlockSpec((tm, width), lambda i: (i, 0))
    return pl.pallas_call(
        _merge_kernel,
        grid=(n // tm,),
        in_specs=[row(d), row(512), row(512), _const_spec((1, d)), _const_spec(w_g.shape),
                  _const_spec(w_ha.shape), _const_spec(w_da.shape), _const_spec(w_o.shape)],
        out_specs=row(d),
        out_shape=jax.ShapeDtypeStruct((n, d), F32),
        compiler_params=pltpu.CompilerParams(dimension_semantics=("parallel",),
                                             vmem_limit_bytes=VMEM_LIMIT),
        name="merge",
    )(x2d, oa, ob, n1, w_g, w_ha, w_da, w_o)


def _moe_kernel(h_ref, n2_ref, wr_hi_ref, wr_lo_ref, br_ref, wgu_ref, wd_ref, y_ref):
    h = h_ref[...]
    tm = h.shape[0]
    hn = _rms_rows(h, n2_ref[...])
    hn_hi = hn.astype(BF16)
    hn_lo = (hn - hn_hi.astype(F32)).astype(BF16)
    logits = (_dot(hn_hi, wr_hi_ref[...]) + _dot(hn_lo, wr_hi_ref[...])
              + _dot(hn_hi, wr_lo_ref[...]) + br_ref[...])
    lane = lax.broadcasted_iota(jnp.int32, logits.shape, 1)
    big = jnp.int32(1 << 20)
    ninf = -jnp.inf

    def first_lane(mask):
        return jnp.min(jnp.where(mask, lane, big), axis=1, keepdims=True)

    is_g = lane < N_GROUPS
    g_max = jnp.max(jnp.where(is_g, logits, ninf), axis=1, keepdims=True)
    g_idx = first_lane(is_g & (logits == g_max))
    g_sum = jnp.sum(jnp.where(is_g, jnp.exp(logits - g_max), 0.0), axis=1, keepdims=True)
    p_g = 1.0 / g_sum
    e_lane = lane - N_GROUPS
    in_group = ((e_lane >= 0) & (e_lane < N_EXPERTS)
                & (lax.shift_right_arithmetic(e_lane, 2) == g_idx))
    e_max = jnp.max(jnp.where(in_group, logits, ninf), axis=1, keepdims=True)
    e_exp = jnp.where(in_group, jnp.exp(logits - e_max), 0.0)
    e_prob = e_exp / jnp.sum(e_exp, axis=1, keepdims=True)
    p1 = jnp.max(jnp.where(in_group, e_prob, -1.0), axis=1, keepdims=True)
    i1 = first_lane(in_group & (e_prob == p1))
    rest = in_group & (lane != i1)
    p2 = jnp.max(jnp.where(rest, e_prob, -1.0), axis=1, keepdims=True)
    i2 = first_lane(rest & (e_prob == p2))
    denom = p1 + p2
    combine = (jnp.where(lane == i1, p_g * p1 / denom, 0.0)
               + jnp.where(lane == i2, p_g * p2 / denom, 0.0))

    ff = wd_ref.shape[1]
    acc = jnp.zeros_like(h)
    for ex in range(N_EXPERTS):
        gu = _dot(hn_hi, wgu_ref[ex])
        gt, up = gu[:, :ff], gu[:, ff:]
        hh = gt * _sigmoid(gt) * up * combine[:, N_GROUPS + ex:N_GROUPS + ex + 1]
        acc = acc + _dot(hh.astype(BF16), wd_ref[ex])
    y_ref[...] = h + acc


def _moe(h2d, n2, wr_hi, wr_lo, br, wgu, wd, tm):
    n, d = h2d.shape
    row = pl.BlockSpec((tm, d), lambda i: (i, 0))
    single = lambda shape: pl.BlockSpec(shape, lambda i: (0,) * len(shape),
                                        pipeline_mode=pl.Buffered(1))
    return pl.pallas_call(
        _moe_kernel,
        grid=(n // tm,),
        in_specs=[row, _const_spec((1, d)), _const_spec(wr_hi.shape), _const_spec(wr_lo.shape),
                  _const_spec(br.shape), single(wgu.shape), single(wd.shape)],
        out_specs=row,
        out_shape=jax.ShapeDtypeStruct((n, d), F32),
        compiler_params=pltpu.CompilerParams(dimension_semantics=("parallel",),
                                             vmem_limit_bytes=VMEM_LIMIT),
        name="moe",
    )(h2d, n2, wr_hi, wr_lo, br, wgu, wd)


def _t5_bias(dist, rel_bias):
    n = jnp.maximum(dist, 0)
    max_exact = N_BUCKETS // 2
    nf = jnp.maximum(n, 1).astype(F32)
    large = max_exact + (jnp.log(nf / max_exact) / math.log(MAX_DISTANCE / max_exact)
                         * (N_BUCKETS - max_exact)).astype(jnp.int32)
    large = jnp.minimum(large, N_BUCKETS - 1)
    bucket = jnp.where(n < max_exact, n, large)
    rb = rel_bias.astype(F32)
    onehot = (bucket[..., None] == jnp.arange(N_BUCKETS)).astype(F32)
    bias = jnp.moveaxis(jnp.dot(onehot, rb, precision=lax.Precision.HIGHEST), -1, 0)
    far = rb[N_BUCKETS - 1].reshape((HEADS,) + (1,) * dist.ndim)
    return jnp.where(dist >= 0, bias - far, MASK_VALUE)


def kernel(x_prompt, x_sample, cache_k, cache_v, state_hgrn, page_table, meta_tokens, rel_bias,
           hg_lb_logits, norm1_w, w_in, hg_onorm_w, w_hg_out, q_norm_w, k_norm_w, da_lambda,
           da_onorm_w, w_da_out, w_o, norm2_w, w_router_group, b_router_group,
           w_router_expert, b_router_expert, w_gate, w_up, w_down):
    bsz, seq, d = x_prompt.shape
    nb, n_tok, _ = x_sample.shape
    depth = w_in.shape[0]
    assert depth == 1 and hg_lb_logits.shape[0] == 2
    n_pages = page_table.shape[1]
    psize = cache_k.shape[2]
    past_len = n_pages * psize
    blk = ATT_BLOCK
    assert blk >= MAX_DISTANCE and psize >= MAX_DISTANCE and n_tok <= 8
    lam_init = 0.8 - 0.6 * math.exp(-0.3 * 0)
    w_attn = HEADS * HEAD_W

    w_in0 = w_in[0]
    w_a = w_in0[:, :3584].astype(BF16)
    w_g = w_in0[:, 3584:].astype(BF16)
    n1 = norm1_w[0].reshape(1, d)
    n2 = norm2_w[0].reshape(1, d)
    qw = (jnp.tile(q_norm_w[0], 2 * HEADS) * (DA_DH ** -0.5)).reshape(1, w_attn)
    kw = jnp.tile(k_norm_w[0], 2 * HEADS).reshape(1, w_attn)
    gi = jnp.arange(w_attn) // DA_DH
    gmat = jnp.where(gi[:, None] == gi[None, :], 1.0 / DA_DH, 0.0).astype(BF16)
    hg_onw = jnp.tile(hg_onorm_w[0], HEADS).reshape(1, w_attn)
    da_onw = da_onorm_w[0].reshape(1, HEAD_W)
    lam_p = da_lambda[0].astype(F32)
    w_ha = w_hg_out[0].astype(BF16)
    w_da = w_da_out[0].astype(BF16)
    w_o0 = w_o[0].astype(BF16)
    wr = jnp.concatenate([w_router_group[0], w_router_expert[0],
                          jnp.zeros((d, 128 - N_GROUPS - N_EXPERTS), F32)], axis=1)
    wr_hi = wr.astype(BF16)
    wr_lo = (wr - wr_hi.astype(F32)).astype(BF16)
    br = jnp.concatenate([b_router_group[0], b_router_expert[0],
                          jnp.zeros((128 - N_GROUPS - N_EXPERTS,), F32)]).reshape(1, 128)
    wgu = jnp.concatenate([w_gate[0], w_up[0]], axis=2).astype(BF16)
    wd = w_down[0].astype(BF16)

    n_small = N_META + nb * n_tok
    x_small = jnp.concatenate([meta_tokens.astype(F32), x_sample.reshape(nb * n_tok, d)], axis=0)
    zh_s, q_s, kf_s, kb_s, vf_s, vb_s = _inproj(x_small, n1, w_a, gmat, qw, kw, n_small, False)

    _, s_meta = _hgrn(zh_s[:N_META][None], hg_lb_logits, hg_onw,
                      jnp.zeros((1, HEADS, HEAD_W, HEAD_W), F32), shared_state=False, out_dtype=F32)
    oa_s, s_sample = _hgrn(zh_s[N_META:].reshape(nb, n_tok, 2048), hg_lb_logits, hg_onw,
                           state_hgrn[0], shared_state=False, out_dtype=F32)

    x_p = x_prompt.reshape(bsz * seq, d)
    zh_p, qt_p, kf_p, kb_p, vf_p, vt_p = _inproj(x_p, n1, w_a, gmat, qw, kw, 512, True)
    oa_p, s_prompt = _hgrn(zh_p.reshape(bsz, seq, 2048), hg_lb_logits, hg_onw, s_meta,
                           shared_state=True, out_dtype=BF16)

    r = jnp.arange(blk)
    tb = jnp.stack([_t5_bias(r[None, :] - r[:, None], rel_bias),
                    _t5_bias(blk + r[None, :] - r[:, None], rel_bias)])
    tb = jnp.concatenate([tb, tb], axis=3)
    mrow = jnp.arange(128)
    t_meta0 = _t5_bias(N_META + r[None, :] - mrow[:, None], rel_bias)
    t_meta = jnp.stack([t_meta0, jnp.zeros_like(t_meta0)])
    t_meta = jnp.where(mrow[:, None] < N_META, t_meta, MASK_VALUE)
    t_meta = jnp.concatenate([t_meta, t_meta], axis=3)
    pad_meta = lambda a: jnp.concatenate([a, jnp.zeros((128 - N_META, w_attn), a.dtype)], axis=0)
    ob_p = _pattn(qt_p, kb_p.reshape(bsz, seq, w_attn), vt_p, pad_meta(kb_s[:N_META]),
                  pad_meta(vb_s[:N_META]).T, tb, t_meta, lam_p, da_onw.reshape(HEAD_W, 1), lam_init)

    q_tok = q_s[N_META:].reshape(nb, n_tok, HEADS, 1, 2, DA_DH)
    q_maps = q_tok * jnp.eye(2, dtype=BF16).reshape(1, 1, 1, 2, 2, 1)
    n_rows = 2 * HEADS * n_tok
    qall = jnp.transpose(q_maps, (0, 2, 3, 1, 4, 5)).reshape(nb, n_rows, HEAD_W)
    prow = psize * HEADS
    pad_new = lambda a: jnp.concatenate(
        [a.reshape(nb, n_tok * HEADS, HEAD_W),
         jnp.zeros((nb, prow - n_tok * HEADS, HEAD_W), a.dtype)], axis=1)
    s_idx = jnp.tile(jnp.arange(n_tok), 2 * HEADS)
    head_of_row = jnp.repeat(jnp.arange(HEADS), 2 * n_tok)
    key_r = jnp.arange(prow) // HEADS
    own_head = head_of_row[:, None] == (jnp.arange(prow) % HEADS)[None, :]
    dist_last = (past_len + s_idx[:, None]) - (past_len - psize + key_r[None, :])
    dist_new = jnp.where(key_r[None, :] < n_tok, s_idx[:, None] - key_r[None, :], -1)
    pick = lambda t: jnp.where(own_head, t[head_of_row, jnp.arange(n_rows)], MASK_VALUE)
    t_last = pick(_t5_bias(dist_last, rel_bias))
    t_new = pick(_t5_bias(dist_new, rel_bias))
    t_mask = jnp.where(own_head, 0.0, MASK_VALUE).astype(F32)
    ob_s = _sattn(page_table, qall, pad_new(kb_s[N_META:]), pad_new(vb_s[N_META:]), t_last, t_new,
                  t_mask, lam_p, da_onw, cache_k.reshape(-1, HEAD_W), cache_v.reshape(-1, HEAD_W),
                  lam_init, n_tok)

    h_p = _merge(x_p, oa_p.reshape(bsz * seq, w_attn), ob_p.reshape(bsz * seq, w_attn),
                 n1, w_g, w_ha, w_da, w_o0, 512)
    y_p = _moe(h_p, n2, wr_hi, wr_lo, br, wgu, wd, 512)
    x_s = x_small[N_META:]
    h_s = _merge(x_s, oa_s.reshape(nb * n_tok, w_attn).astype(BF16),
                 ob_s.reshape(nb * n_tok, w_attn).astype(BF16), n1, w_g, w_ha, w_da, w_o0,
                 nb * n_tok)
    y_s = _moe(h_s, n2, wr_hi, wr_lo, br, wgu, wd, nb * n_tok)

    def with_meta(meta_rows, real):
        meta_b = jnp.broadcast_to(meta_rows[None], (bsz, N_META, w_attn))
        full = jnp.concatenate([meta_b, real.reshape(bsz, seq, w_attn)], axis=1)
        return full.reshape(1, bsz, seq + N_META, HEADS, HEAD_W)

    return (y_p.reshape(bsz, seq, d),
            y_s.reshape(nb, n_tok, d),
            with_meta(kf_s[:N_META], kf_p),
            with_meta(vf_s[:N_META], vf_p),
            s_prompt[None],
            kf_s[N_META:].reshape(1, nb, n_tok, HEADS, HEAD_W),
            vf_s[N_META:].reshape(1, nb, n_tok, HEADS, HEAD_W),
            s_sample[None])
```

```python
import functools
import math

import jax
import jax.numpy as jnp
from jax import lax
from jax.experimental import pallas as pl
from jax.experimental.pallas import tpu as pltpu

F32 = jnp.float32
BF16 = jnp.bfloat16

EPS = 1e-6
N_META = 16
HEADS = 4
HEAD_W = 128
DA_DH = 64
N_BUCKETS = 32
MAX_DISTANCE = 128
N_GROUPS = 4
EXPERTS_PER_GROUP = 4
N_EXPERTS = N_GROUPS * EXPERTS_PER_GROUP
MASK_VALUE = -1e30
HG_CHUNK = 128
HG_BASE = 16
ATT_BLOCK = 256
VT_PAD = 16
VT_ROWS = HEADS * (HEAD_W + VT_PAD)
FAR_UNROLL = 4
PAGES_PER_STEP = 16
LOG2E = math.log2(math.e)
VMEM_LIMIT = 56 * 1024 * 1024


def _dot(a, b):
    return jnp.dot(a, b, preferred_element_type=F32)


def _dot_nt(a, b):
    return lax.dot_general(a, b, (((1,), (1,)), ((), ())), preferred_element_type=F32)


def _dot_tn(a, b):
    return lax.dot_general(a, b, (((0,), (0,)), ((), ())), preferred_element_type=F32)


def _sigmoid(x):
    return 1.0 / (1.0 + jnp.exp(-x))


def _rms_rows(x, w):
    return x * lax.rsqrt(jnp.mean(x * x, axis=-1, keepdims=True) + EPS) * w


def _const_spec(shape):
    zeros = (0,) * len(shape)
    return pl.BlockSpec(shape, lambda *_: zeros)


def _inproj_kernel(x_ref, n1_ref, w_ref, g_ref, qw_ref, kw_ref,
                   zh_ref, q_ref, kf_ref, kb_ref, vf_ref, vb_ref, *, transposed):
    xn = _rms_rows(x_ref[...], n1_ref[...]).astype(BF16)
    zh_ref[...] = _dot(xn, w_ref[:, 0:2048])
    g = g_ref[...]

    def group_norm(z, w):
        sq = z * z
        hi = sq.astype(BF16)
        lo = (sq - hi.astype(F32)).astype(BF16)
        ms = _dot(hi, g) + _dot(lo, g)
        return z * lax.rsqrt(ms + EPS) * w

    def put(ref, val, pad=0):
        if transposed:
            vt = val.T
            for c in range(ref.shape[0]):
                cols = slice(c * ATT_BLOCK, (c + 1) * ATT_BLOCK)
                if pad == 0:
                    ref[c] = vt[:, cols].astype(BF16)
                    continue
                for h in range(HEADS):
                    r0 = h * (HEAD_W + pad)
                    ref[c, r0:r0 + HEAD_W, :] = vt[h * HEAD_W:(h + 1) * HEAD_W, cols].astype(BF16)
                    ref[c, r0 + HEAD_W:r0 + HEAD_W + pad, :] = jnp.ones((pad, ATT_BLOCK), BF16)
        else:
            ref[...] = val.astype(BF16)

    put(q_ref, group_norm(_dot(xn, w_ref[:, 2048:2560]), qw_ref[...]))
    kn = group_norm(_dot(xn, w_ref[:, 2560:3072]), kw_ref[...])
    kf_ref[...] = kn
    kb_ref[...] = kn.astype(BF16)
    zv = _dot(xn, w_ref[:, 3072:3584])
    vf_ref[...] = zv
    put(vb_ref, zv, VT_PAD)


def _inproj(x2d, n1, w_a, gmat, qw, kw, tm, transposed):
    n, d = x2d.shape
    row = lambda width: pl.BlockSpec((tm, width), lambda i: (i, 0))
    if transposed:
        assert tm % ATT_BLOCK == 0
        per = tm // ATT_BLOCK
        t_spec = lambda rows: pl.BlockSpec((per, rows, ATT_BLOCK), lambda i: (i, 0, 0))
        t_shape = lambda rows: jax.ShapeDtypeStruct((n // ATT_BLOCK, rows, ATT_BLOCK), BF16)
    else:
        t_spec = lambda rows: row(512)
        t_shape = lambda rows: jax.ShapeDtypeStruct((n, 512), BF16)
    return pl.pallas_call(
        functools.partial(_inproj_kernel, transposed=transposed),
        grid=(n // tm,),
        in_specs=[row(d), _const_spec((1, d)), _const_spec(w_a.shape), _const_spec(gmat.shape),
                  _const_spec((1, 512)), _const_spec((1, 512))],
        out_specs=[row(2048), t_spec(512), row(512), row(512), row(512), t_spec(VT_ROWS)],
        out_shape=[jax.ShapeDtypeStruct((n, 2048), F32), t_shape(512),
                   jax.ShapeDtypeStruct((n, 512), F32), jax.ShapeDtypeStruct((n, 512), BF16),
                   jax.ShapeDtypeStruct((n, 512), F32), t_shape(VT_ROWS)],
        compiler_params=pltpu.CompilerParams(dimension_semantics=("parallel",),
                                             vmem_limit_bytes=VMEM_LIMIT),
        name="inproj",
    )(x2d, n1, w_a, gmat, qw, kw)


def _hgrn_kernel(z_ref, lbl_ref, onw_ref, s0_ref, o_ref, sout_ref, st_scr, *, n_valid):
    C = HG_CHUNK
    W = HEADS * HEAD_W
    c = pl.program_id(1)

    @pl.when(c == 0)
    def _():
        for h in range(HEADS):
            st_scr[h] = s0_ref[0, h].T

    z = z_ref[0]
    if n_valid < C:
        z = jnp.concatenate([z, jnp.zeros((C - n_valid, z.shape[1]), F32)], axis=0)
    lg = lbl_ref[...]
    e = jnp.exp(lg - jnp.max(lg, axis=0, keepdims=True))
    lb = e[0:1] / jnp.sum(e, axis=0, keepdims=True)

    hq, hf, hi, hg = z[:, 0:W], z[:, W:2 * W], z[:, 2 * W:3 * W], z[:, 3 * W:4 * W]
    q = hq * _sigmoid(hq)
    f = lb + (1.0 - lb) * _sigmoid(hf)
    k = 1.0 - f
    logf = jnp.log(f)
    if n_valid < C:
        valid = lax.broadcasted_iota(jnp.int32, (C, W), 0) < n_valid
        k = jnp.where(valid, k, 0.0)
        logf = jnp.where(valid, logf, 0.0)
    v_bf = hi.astype(BF16)

    row = lax.broadcasted_iota(jnp.int32, (C, C), 0)
    col = lax.broadcasted_iota(jnp.int32, (C, C), 1)
    tri = jnp.where(col <= row, 1.0, 0.0).astype(BF16)
    x1 = logf.astype(BF16)
    r1 = logf - x1.astype(F32)
    x2 = r1.astype(BF16)
    x3 = (r1 - x2.astype(F32)).astype(BF16)
    b = _dot(tri, x1) + _dot(tri, x2) + _dot(tri, x3)
    b_last = b[C - 1:C]

    def block_row(width, r):
        b3 = b.reshape(C // width, width, W)
        return jnp.broadcast_to(b3[:, r:r + 1, :], b3.shape).reshape(C, W)

    def next_block_first(width):
        b3 = b.reshape(C // width, width, W)
        nxt = jnp.concatenate([b3[1:, 0:1, :], b_last[None]], axis=0)
        return jnp.broadcast_to(nxt, b3.shape).reshape(C, W)

    def shr(x, width):
        return lax.shift_right_logical(x, int(math.log2(width)))

    mid = block_row(HG_BASE, HG_BASE // 2)
    levels = [((q * jnp.exp(b - mid)).astype(BF16), (k * jnp.exp(mid - b)).astype(BF16),
               (shr(row, HG_BASE) == shr(col, HG_BASE)) & (col <= row))]
    width = HG_BASE
    while width < C:
        rq = block_row(width, 0)
        rk = next_block_first(width)
        levels.append(((q * jnp.exp(b - rq)).astype(BF16), (k * jnp.exp(rk - b)).astype(BF16),
                       (shr(row, width) == shr(col, width) + 1)
                       & (shr(row, 2 * width) == shr(col, 2 * width))))
        width *= 2

    qb = (q * jnp.exp(b)).astype(BF16)
    kd = (k * jnp.exp(b_last - b)).astype(BF16)
    decay = jnp.exp(b_last)
    gate = hg * _sigmoid(hg)
    onw = onw_ref[...]

    for h in range(HEADS):
        hs = slice(h * HEAD_W, (h + 1) * HEAD_W)
        st = st_scr[h]
        scores = jnp.zeros((C, C), F32)
        for qe, ke, mask in levels:
            scores = jnp.where(mask, _dot_nt(qe[:, hs], ke[:, hs]), scores)
        o = _dot_nt(qb[:, hs], st.astype(BF16)) + _dot(scores.astype(BF16), v_bf[:, hs])
        st_scr[h] = st * decay[:, hs] + _dot_tn(v_bf[:, hs], kd[:, hs])
        on = _rms_rows(o, onw[:, hs]) * gate[:, hs]
        o_ref[0, :, hs] = on[:n_valid].astype(o_ref.dtype)

    @pl.when(c == pl.num_programs(1) - 1)
    def _():
        for h in range(HEADS):
            sout_ref[0, h] = st_scr[h].T


def _hgrn(zh, lbl, onw, s0, *, shared_state, out_dtype):
    nseq, t, _ = zh.shape
    rows = min(t, HG_CHUNK)
    assert t % rows == 0 and rows % 8 == 0
    s_map = (lambda b, c: (0, 0, 0, 0)) if shared_state else (lambda b, c: (b, 0, 0, 0))
    return pl.pallas_call(
        functools.partial(_hgrn_kernel, n_valid=rows),
        grid=(nseq, t // rows),
        in_specs=[pl.BlockSpec((1, rows, 2048), lambda b, c: (b, c, 0)),
                  _const_spec(lbl.shape), _const_spec((1, 512)),
                  pl.BlockSpec((1, HEADS, HEAD_W, HEAD_W), s_map)],
        out_specs=[pl.BlockSpec((1, rows, 512), lambda b, c: (b, c, 0)),
                   pl.BlockSpec((1, HEADS, HEAD_W, HEAD_W), lambda b, c: (b, 0, 0, 0))],
        out_shape=[jax.ShapeDtypeStruct((nseq, t, 512), out_dtype),
                   jax.ShapeDtypeStruct((nseq, HEADS, HEAD_W, HEAD_W), F32)],
        scratch_shapes=[pltpu.VMEM((HEADS, HEAD_W, HEAD_W), F32)],
        compiler_params=pltpu.CompilerParams(dimension_semantics=("parallel", "arbitrary"),
                                             vmem_limit_bytes=VMEM_LIMIT),
        name="hgrn",
    )(zh, lbl, onw, s0)


def _lambda_value(lp, lam_init):
    s1 = jnp.sum(lp[0:1] * lp[1:2], axis=1, keepdims=True)
    s2 = jnp.sum(lp[2:3] * lp[3:4], axis=1, keepdims=True)
    return jnp.exp(s1) - jnp.exp(s2) + lam_init


def _softmax_update(s, v_bf, m_ref, l_ref, acc_ref, first):
    m_cur = jnp.max(s, axis=1, keepdims=True)
    if first:
        m_new = m_cur
        p = jnp.exp2(s - m_new)
        l_ref[...] = jnp.sum(p, axis=1, keepdims=True)
        acc_ref[...] = _dot(p.astype(BF16), v_bf)
    else:
        m_prev = m_ref[...]
        m_new = jnp.maximum(m_prev, m_cur)
        alpha = jnp.exp2(m_prev - m_new)
        p = jnp.exp2(s - m_new)
        l_ref[...] = alpha * l_ref[...] + jnp.sum(p, axis=1, keepdims=True)
        acc_ref[...] = alpha * acc_ref[...] + _dot(p.astype(BF16), v_bf)
    m_ref[...] = m_new


def _pattn_kernel(qt_ref, k_ref, vt_ref, km_ref, vmt_ref, tb_ref, tm_ref, lam_ref, onw_ref, o_ref,
                  qz_scr, m_scr, acc_scr, *, lam_init):
    blk = ATT_BLOCK
    i = pl.program_id(1)
    sub = lax.broadcasted_iota(jnp.int32, (HEAD_W, blk), 0)
    for h in range(HEADS):
        qh = qt_ref[0, h * HEAD_W:(h + 1) * HEAD_W, :]
        zero = jnp.zeros_like(qh)
        qz_scr[h] = jnp.concatenate([jnp.where(sub < DA_DH, qh, zero),
                                     jnp.where(sub >= DA_DH, qh, zero)], axis=1)

    hv = lambda h: slice(h * (HEAD_W + VT_PAD), (h + 1) * (HEAD_W + VT_PAD))

    def key_blocks(blocks):
        def qk(j, table, h):
            hs = slice(h * HEAD_W, (h + 1) * HEAD_W)
            off = pl.multiple_of(j * blk, blk)
            s = _dot(k_ref[0, pl.ds(off, blk), hs], qz_scr[h])
            return s if table is None else s + tb_ref[table, h]

        def soft(h, s):
            m_prev = m_scr[h]
            m_new = jnp.maximum(m_prev, jnp.max(s, axis=0, keepdims=True))
            m_scr[h] = m_new
            return jnp.exp2(s - m_new).astype(BF16), jnp.exp2(m_prev - m_new)

        def pv(j, h, p, alpha):
            acc_scr[h] = alpha * acc_scr[h] + _dot(vt_ref[j, hv(h), :], p)

        stages = [(j, table, h) for j, table in blocks for h in range(HEADS)]
        s_cur = qk(*stages[0])
        pend = None
        for n, (j, table, h) in enumerate(stages):
            s_next = qk(*stages[n + 1]) if n + 1 < len(stages) else None
            if pend is not None:
                pv(stages[n - 1][0], stages[n - 1][2], *pend)
            pend = soft(h, s_cur)
            s_cur = s_next
        pv(stages[-1][0], stages[-1][2], *pend)

    meta_sel = jnp.minimum(i, 1)
    for h in range(HEADS):
        hs = slice(h * HEAD_W, (h + 1) * HEAD_W)
        s = _dot(km_ref[:, hs], qz_scr[h]) + tm_ref[meta_sel, h]
        m_new = jnp.max(s, axis=0, keepdims=True)
        m_scr[h] = m_new
        acc_scr[h] = _dot(vmt_ref[hv(h), :], jnp.exp2(s - m_new).astype(BF16))

    n_far = jnp.maximum(i - 1, 0)
    n_single = n_far % FAR_UNROLL

    def single_body(j, carry):
        key_blocks([(j, None)])
        return carry

    lax.fori_loop(0, n_single, single_body, 0)

    def far_body(t, carry):
        j = n_single + FAR_UNROLL * t
        key_blocks([(j + u, None) for u in range(FAR_UNROLL)])
        return carry

    lax.fori_loop(0, n_far // FAR_UNROLL, far_body, 0)

    @pl.when(i >= 1)
    def _():
        key_blocks([(i - 1, 1), (i, 0)])

    @pl.when(i == 0)
    def _():
        key_blocks([(0, 0)])

    lam = _lambda_value(lam_ref[...], lam_init)
    onw = onw_ref[...]
    for h in range(HEADS):
        acc = acc_scr[h]
        a = acc[:HEAD_W] / acc[HEAD_W:HEAD_W + 1]
        ot = a[:, :blk] - lam * a[:, blk:]
        ot = ot * lax.rsqrt(jnp.mean(ot * ot, axis=0, keepdims=True) + EPS) * onw * (1.0 - lam_init)
        o_ref[0, :, h * HEAD_W:(h + 1) * HEAD_W] = ot.T.astype(o_ref.dtype)


def _pattn(qt, k, vt, km, vmt, tb, tmeta, lam_p, onw_col, lam_init):
    bsz, t, w = k.shape
    blk = ATT_BLOCK
    nblk = t // blk
    assert t % blk == 0
    return pl.pallas_call(
        functools.partial(_pattn_kernel, lam_init=lam_init),
        grid=(bsz, nblk),
        in_specs=[pl.BlockSpec((1, w, blk), lambda b, i: (b * nblk + i, 0, 0)),
                  pl.BlockSpec((1, t, w), lambda b, i: (b, 0, 0)),
                  pl.BlockSpec((nblk, VT_ROWS, blk), lambda b, i: (b, 0, 0)),
                  _const_spec(km.shape), _const_spec(vmt.shape), _const_spec(tb.shape),
                  _const_spec(tmeta.shape), _const_spec(lam_p.shape), _const_spec(onw_col.shape)],
        out_specs=pl.BlockSpec((1, blk, w), lambda b, i: (b, i, 0)),
        out_shape=jax.ShapeDtypeStruct((bsz, t, w), BF16),
        scratch_shapes=[pltpu.VMEM((HEADS, HEAD_W, 2 * blk), BF16),
                        pltpu.VMEM((HEADS, 1, 2 * blk), F32),
                        pltpu.VMEM((HEADS, HEAD_W + VT_PAD, 2 * blk), F32)],
        compiler_params=pltpu.CompilerParams(dimension_semantics=("parallel", "arbitrary"),
                                             vmem_limit_bytes=VMEM_LIMIT),
        name="prompt_attn",
    )(qt, k, vt, km, vmt, tb, tmeta, lam_p, onw_col)


def _sattn_kernel(pt_ref, q_ref, kn_ref, vn_ref, tl_ref, tn_ref, tmask_ref, lam_ref, onw_ref, *rest,
                  lam_init, n_tok):
    del pt_ref
    pg = PAGES_PER_STEP
    k_refs, v_refs = rest[:pg], rest[pg:2 * pg]
    o_ref, m_scr, l_scr, acc_scr = rest[2 * pg:]
    g = pl.program_id(1)
    q = q_ref[0]

    @pl.when(g == 0)
    def _():
        s = _dot_nt(q, kn_ref[0]) + tn_ref[...]
        _softmax_update(s, vn_ref[0], m_scr, l_scr, acc_scr, True)

    is_last = g == pl.num_programs(1) - 1
    tmask = tmask_ref[...]
    ss = []
    for u in range(pg):
        s = _dot_nt(q, k_refs[u][...].astype(BF16))
        ss.append(s + (jnp.where(is_last, tl_ref[...], tmask) if u == pg - 1 else tmask))
    m_prev = m_scr[...]
    m_new = m_prev
    for s in ss:
        m_new = jnp.maximum(m_new, jnp.max(s, axis=1, keepdims=True))
    alpha = jnp.exp2(m_prev - m_new)
    l_new = alpha * l_scr[...]
    acc = alpha * acc_scr[...]
    for u in range(pg):
        p = jnp.exp2(ss[u] - m_new)
        l_new = l_new + jnp.sum(p, axis=1, keepdims=True)
        acc = acc + _dot(p.astype(BF16), v_refs[u][...].astype(BF16))
    m_scr[...] = m_new
    l_scr[...] = l_new
    acc_scr[...] = acc

    @pl.when(is_last)
    def _():
        lam = _lambda_value(lam_ref[...], lam_init)
        onw = onw_ref[...]
        a = acc_scr[...] / l_scr[...]
        for h in range(HEADS):
            r1 = (2 * h) * n_tok
            r2 = (2 * h + 1) * n_tok
            o = a[r1:r1 + n_tok] - lam * a[r2:r2 + n_tok]
            o_ref[0, :, h * HEAD_W:(h + 1) * HEAD_W] = (
                _rms_rows(o, onw) * (1.0 - lam_init)).astype(o_ref.dtype)


def _sattn(page_table, qall, knew, vnew, tlast, tnew, tmask, lam_p, onw, ck2, cv2, lam_init, n_tok):
    nb, n_pages = page_table.shape
    pg = PAGES_PER_STEP
    assert n_pages % pg == 0
    nrow = qall.shape[1]
    prow = knew.shape[1]
    pt_flat = page_table.reshape(-1)

    def page_spec(u):
        return pl.BlockSpec((prow, HEAD_W), lambda b, g, pt: (pt[b * n_pages + g * pg + u], 0))

    per_b = lambda shape: pl.BlockSpec((1,) + shape, lambda b, g, pt: (b, 0, 0))
    const = lambda shape: pl.BlockSpec(shape, lambda b, g, pt: (0,) * len(shape))
    grid_spec = pltpu.PrefetchScalarGridSpec(
        num_scalar_prefetch=1,
        grid=(nb, n_pages // pg),
        in_specs=[per_b((nrow, HEAD_W)), per_b((prow, HEAD_W)), per_b((prow, HEAD_W)),
                  const(tlast.shape), const(tnew.shape), const(tmask.shape),
                  const(lam_p.shape), const(onw.shape)]
                 + [page_spec(u) for u in range(pg)] + [page_spec(u) for u in range(pg)],
        out_specs=per_b((n_tok, HEADS * HEAD_W)),
        scratch_shapes=[pltpu.VMEM((nrow, 1), F32), pltpu.VMEM((nrow, 1), F32),
                        pltpu.VMEM((nrow, HEAD_W), F32)],
    )
    return pl.pallas_call(
        functools.partial(_sattn_kernel, lam_init=lam_init, n_tok=n_tok),
        grid_spec=grid_spec,
        out_shape=jax.ShapeDtypeStruct((nb, n_tok, HEADS * HEAD_W), F32),
        compiler_params=pltpu.CompilerParams(dimension_semantics=("parallel", "arbitrary"),
                                             vmem_limit_bytes=VMEM_LIMIT),
        name="sample_attn",
    )(pt_flat, qall, knew, vnew, tlast, tnew, tmask, lam_p, onw, *([ck2] * pg), *([cv2] * pg))


def _merge_kernel(x_ref, oa_ref, ob_ref, n1_ref, wg_ref, wha_ref, wda_ref, wo_ref, h_ref):
    x = x_ref[...]
    d = x.shape[1]
    xn = _rms_rows(x, n1_ref[...]).astype(BF16)
    gates = _dot(xn, wg_ref[...])
    ya = _dot(oa_ref[...], wha_ref[...])
    yb = _dot(ob_ref[...], wda_ref[...])
    merged = _sigmoid(gates[:, :d]) * ya + _sigmoid(gates[:, d:]) * yb
    h_ref[...] = x + _dot(merged.astype(BF16), wo_ref[...])


def _merge(x2d, oa, ob, n1, w_g, w_ha, w_da, w_o, tm):
    n, d = x2d.shape
    row = lambda width: pl.BlockSpec((tm, width), lambda i: (i, 0))
    return pl.pallas_call(
        _merge_kernel,
        grid=(n // tm,),
        in_specs=[row(d), row(512), row(512), _const_spec((1, d)), _const_spec(w_g.shape),
                  _const_spec(w_ha.shape), _const_spec(w_da.shape), _const_spec(w_o.shape)],
        out_specs=row(d),
        out_shape=jax.ShapeDtypeStruct((n, d), F32),
        compiler_params=pltpu.CompilerParams(dimension_semantics=("parallel",),
                                             vmem_limit_bytes=VMEM_LIMIT),
        name="merge",
    )(x2d, oa, ob, n1, w_g, w_ha, w_da, w_o)


def _moe_kernel(h_ref, n2_ref, wr_hi_ref, wr_lo_ref, br_ref, wgu_ref, wd_ref, y_ref):
    h = h_ref[...]
    tm = h.shape[0]
    hn = _rms_rows(h, n2_ref[...])
    hn_hi = hn.astype(BF16)
    hn_lo = (hn - hn_hi.astype(F32)).astype(BF16)
    logits = (_dot(hn_hi, wr_hi_ref[...]) + _dot(hn_lo, wr_hi_ref[...])
              + _dot(hn_hi, wr_lo_ref[...]) + br_ref[...])
    lane = lax.broadcasted_iota(jnp.int32, logits.shape, 1)
    big = jnp.int32(1 << 20)
    ninf = -jnp.inf

    def first_lane(mask):
        return jnp.min(jnp.where(mask, lane, big), axis=1, keepdims=True)

    is_g = lane < N_GROUPS
    g_max = jnp.max(jnp.where(is_g, logits, ninf), axis=1, keepdims=True)
    g_idx = first_lane(is_g & (logits == g_max))
    g_sum = jnp.sum(jnp.where(is_g, jnp.exp(logits - g_max), 0.0), axis=1, keepdims=True)
    p_g = 1.0 / g_sum
    e_lane = lane - N_GROUPS
    in_group = ((e_lane >= 0) & (e_lane < N_EXPERTS)
                & (lax.shift_right_arithmetic(e_lane, 2) == g_idx))
    e_max = jnp.max(jnp.where(in_group, logits, ninf), axis=1, keepdims=True)
    e_exp = jnp.where(in_group, jnp.exp(logits - e_max), 0.0)
    e_prob = e_exp / jnp.sum(e_exp, axis=1, keepdims=True)
    p1 = jnp.max(jnp.where(in_group, e_prob, -1.0), axis=1, keepdims=True)
    i1 = first_lane(in_group & (e_prob == p1))
    rest = in_group & (lane != i1)
    p2 = jnp.max(jnp.where(rest, e_prob, -1.0), axis=1, keepdims=True)
    i2 = first_lane(rest & (e_prob == p2))
    denom = p1 + p2
    combine = (jnp.where(lane == i1, p_g * p1 / denom, 0.0)
               + jnp.where(lane == i2, p_g * p2 / denom, 0.0))

    ff = wd_ref.shape[1]
    acc = jnp.zeros_like(h)
    for ex in range(N_EXPERTS):
        gu = _dot(hn_hi, wgu_ref[ex])
        gt, up = gu[:, :ff], gu[:, ff:]
        hh = gt * _sigmoid(gt) * up * combine[:, N_GROUPS + ex:N_GROUPS + ex + 1]
        acc = acc + _dot(hh.astype(BF16), wd_ref[ex])
    y_ref[...] = h + acc


def _moe(h2d, n2, wr_hi, wr_lo, br, wgu, wd, tm):
    n, d = h2d.shape
    row = pl.BlockSpec((tm, d), lambda i: (i, 0))
    single = lambda shape: pl.BlockSpec(shape, lambda i: (0,) * len(shape),
                                        pipeline_mode=pl.Buffered(1))
    return pl.pallas_call(
        _moe_kernel,
        grid=(n // tm,),
        in_specs=[row, _const_spec((1, d)), _const_spec(wr_hi.shape), _const_spec(wr_lo.shape),
                  _const_spec(br.shape), single(wgu.shape), single(wd.shape)],
        out_specs=row,
        out_shape=jax.ShapeDtypeStruct((n, d), F32),
        compiler_params=pltpu.CompilerParams(dimension_semantics=("parallel",),
                                             vmem_limit_bytes=VMEM_LIMIT),
        name="moe",
    )(h2d, n2, wr_hi, wr_lo, br, wgu, wd)


def _t5_bias(dist, rel_bias):
    n = jnp.maximum(dist, 0)
    max_exact = N_BUCKETS // 2
    nf = jnp.maximum(n, 1).astype(F32)
    large = max_exact + (jnp.log(nf / max_exact) / math.log(MAX_DISTANCE / max_exact)
                         * (N_BUCKETS - max_exact)).astype(jnp.int32)
    large = jnp.minimum(large, N_BUCKETS - 1)
    bucket = jnp.where(n < max_exact, n, large)
    rb = rel_bias.astype(F32)
    onehot = (bucket[..., None] == jnp.arange(N_BUCKETS)).astype(F32)
    bias = jnp.moveaxis(jnp.dot(onehot, rb, precision=lax.Precision.HIGHEST), -1, 0)
    far = rb[N_BUCKETS - 1].reshape((HEADS,) + (1,) * dist.ndim)
    return jnp.where(dist >= 0, (bias - far) * LOG2E, MASK_VALUE)


def kernel(x_prompt, x_sample, cache_k, cache_v, state_hgrn, page_table, meta_tokens, rel_bias,
           hg_lb_logits, norm1_w, w_in, hg_onorm_w, w_hg_out, q_norm_w, k_norm_w, da_lambda,
           da_onorm_w, w_da_out, w_o, norm2_w, w_router_group, b_router_group,
           w_router_expert, b_router_expert, w_gate, w_up, w_down):
    bsz, seq, d = x_prompt.shape
    nb, n_tok, _ = x_sample.shape
    depth = w_in.shape[0]
    assert depth == 1 and hg_lb_logits.shape[0] == 2
    n_pages = page_table.shape[1]
    psize = cache_k.shape[2]
    past_len = n_pages * psize
    blk = ATT_BLOCK
    assert blk >= MAX_DISTANCE and psize >= MAX_DISTANCE and n_tok <= 8
    lam_init = 0.8 - 0.6 * math.exp(-0.3 * 0)
    w_attn = HEADS * HEAD_W

    w_in0 = w_in[0]
    w_a = w_in0[:, :3584].astype(BF16)
    w_g = w_in0[:, 3584:].astype(BF16)
    n1 = norm1_w[0].reshape(1, d)
    n2 = norm2_w[0].reshape(1, d)
    qw = (jnp.tile(q_norm_w[0], 2 * HEADS) * (DA_DH ** -0.5 * LOG2E)).reshape(1, w_attn)
    kw = jnp.tile(k_norm_w[0], 2 * HEADS).reshape(1, w_attn)
    gi = jnp.arange(w_attn) // DA_DH
    gmat = jnp.where(gi[:, None] == gi[None, :], 1.0 / DA_DH, 0.0).astype(BF16)
    hg_onw = jnp.tile(hg_onorm_w[0], HEADS).reshape(1, w_attn)
    da_onw = da_onorm_w[0].reshape(1, HEAD_W)
    lam_p = da_lambda[0].astype(F32)
    w_ha = w_hg_out[0].astype(BF16)
    w_da = w_da_out[0].astype(BF16)
    w_o0 = w_o[0].astype(BF16)
    wr = jnp.concatenate([w_router_group[0], w_router_expert[0],
                          jnp.zeros((d, 128 - N_GROUPS - N_EXPERTS), F32)], axis=1)
    wr_hi = wr.astype(BF16)
    wr_lo = (wr - wr_hi.astype(F32)).astype(BF16)
    br = jnp.concatenate([b_router_group[0], b_router_expert[0],
                          jnp.zeros((128 - N_GROUPS - N_EXPERTS,), F32)]).reshape(1, 128)
    wgu = jnp.concatenate([w_gate[0], w_up[0]], axis=2).astype(BF16)
    wd = w_down[0].astype(BF16)

    n_small = N_META + nb * n_tok
    x_small = jnp.concatenate([meta_tokens.astype(F32), x_sample.reshape(nb * n_tok, d)], axis=0)
    zh_s, q_s, kf_s, kb_s, vf_s, vb_s = _inproj(x_small, n1, w_a, gmat, qw, kw, n_small, False)

    _, s_meta = _hgrn(zh_s[:N_META][None], hg_lb_logits, hg_onw,
                      jnp.zeros((1, HEADS, HEAD_W, HEAD_W), F32), shared_state=False, out_dtype=F32)
    oa_s, s_sample = _hgrn(zh_s[N_META:].reshape(nb, n_tok, 2048), hg_lb_logits, hg_onw,
                           state_hgrn[0], shared_state=False, out_dtype=F32)

    x_p = x_prompt.reshape(bsz * seq, d)
    zh_p, qt_p, kf_p, kb_p, vf_p, vt_p = _inproj(x_p, n1, w_a, gmat, qw, kw, 512, True)
    oa_p, s_prompt = _hgrn(zh_p.reshape(bsz, seq, 2048), hg_lb_logits, hg_onw, s_meta,
                           shared_state=True, out_dtype=BF16)

    r = jnp.arange(blk)
    tb = jnp.stack([_t5_bias(r[None, :] - r[:, None], rel_bias),
                    _t5_bias(blk + r[None, :] - r[:, None], rel_bias)])
    tb = jnp.concatenate([tb, tb], axis=3)
    mrow = jnp.arange(128)
    t_meta0 = _t5_bias(N_META + r[None, :] - mrow[:, None], rel_bias)
    t_meta = jnp.stack([t_meta0, jnp.zeros_like(t_meta0)])
    t_meta = jnp.where(mrow[:, None] < N_META, t_meta, MASK_VALUE)
    t_meta = jnp.concatenate([t_meta, t_meta], axis=3)
    pad_meta = lambda a: jnp.concatenate([a, jnp.zeros((128 - N_META, w_attn), a.dtype)], axis=0)
    vmt = pad_meta(vb_s[:N_META]).T.reshape(HEADS, HEAD_W, 128)
    vmt = jnp.concatenate([vmt, jnp.ones((HEADS, VT_PAD, 128), BF16)], axis=1).reshape(VT_ROWS, 128)
    ob_p = _pattn(qt_p, kb_p.reshape(bsz, seq, w_attn), vt_p, pad_meta(kb_s[:N_META]),
                  vmt, tb, t_meta, lam_p, da_onw.reshape(HEAD_W, 1), lam_init)

    q_tok = q_s[N_META:].reshape(nb, n_tok, HEADS, 1, 2, DA_DH)
    q_maps = q_tok * jnp.eye(2, dtype=BF16).reshape(1, 1, 1, 2, 2, 1)
    n_rows = 2 * HEADS * n_tok
    qall = jnp.transpose(q_maps, (0, 2, 3, 1, 4, 5)).reshape(nb, n_rows, HEAD_W)
    prow = psize * HEADS
    pad_new = lambda a: jnp.concatenate(
        [a.reshape(nb, n_tok * HEADS, HEAD_W),
         jnp.zeros((nb, prow - n_tok * HEADS, HEAD_W), a.dtype)], axis=1)
    s_idx = jnp.tile(jnp.arange(n_tok), 2 * HEADS)
    head_of_row = jnp.repeat(jnp.arange(HEADS), 2 * n_tok)
    key_r = jnp.arange(prow) // HEADS
    own_head = head_of_row[:, None] == (jnp.arange(prow) % HEADS)[None, :]
    dist_last = (past_len + s_idx[:, None]) - (past_len - psize + key_r[None, :])
    dist_new = jnp.where(key_r[None, :] < n_tok, s_idx[:, None] - key_r[None, :], -1)
    pick = lambda t: jnp.where(own_head, t[head_of_row, jnp.arange(n_rows)], MASK_VALUE)
    t_last = pick(_t5_bias(dist_last, rel_bias))
    t_new = pick(_t5_bias(dist_new, rel_bias))
    t_mask = jnp.where(own_head, 0.0, MASK_VALUE).astype(F32)
    ob_s = _sattn(page_table, qall, pad_new(kb_s[N_META:]), pad_new(vb_s[N_META:]), t_last, t_new,
                  t_mask, lam_p, da_onw, cache_k.reshape(-1, HEAD_W), cache_v.reshape(-1, HEAD_W),
                  lam_init, n_tok)

    h_p = _merge(x_p, oa_p.reshape(bsz * seq, w_attn), ob_p.reshape(bsz * seq, w_attn),
                 n1, w_g, w_ha, w_da, w_o0, 512)
    y_p = _moe(h_p, n2, wr_hi, wr_lo, br, wgu, wd, 512)
    x_s = x_small[N_META:]
    h_s = _merge(x_s, oa_s.reshape(nb * n_tok, w_attn).astype(BF16),
                 ob_s.reshape(nb * n_tok, w_attn).astype(BF16), n1, w_g, w_ha, w_da, w_o0,
                 nb * n_tok)
    y_s = _moe(h_s, n2, wr_hi, wr_lo, br, wgu, wd, nb * n_tok)

    def with_meta(meta_rows, real):
        meta_b = jnp.broadcast_to(meta_rows[None], (bsz, N_META, w_attn))
        full = jnp.concatenate([meta_b, real.reshape(bsz, seq, w_attn)], axis=1)
        return full.reshape(1, bsz, seq + N_META, HEADS, HEAD_W)

    return (y_p.reshape(bsz, seq, d),
            y_s.reshape(nb, n_tok, d),
            with_meta(kf_s[:N_META], kf_p),
            with_meta(vf_s[:N_META], vf_p),
            s_prompt[None],
            kf_s[N_META:].reshape(1, nb, n_tok, HEADS, HEAD_W),
            vf_s[N_META:].reshape(1, nb, n_tok, HEADS, HEAD_W),
            s_sample[None])
```

```python
import functools
import math

import jax
import jax.numpy as jnp
from jax import lax
from jax.experimental import pallas as pl
from jax.experimental.pallas import tpu as pltpu

F32 = jnp.float32
BF16 = jnp.bfloat16

EPS = 1e-6
N_META = 16
HEADS = 4
HEAD_W = 128
DA_DH = 64
N_BUCKETS = 32
MAX_DISTANCE = 128
N_GROUPS = 4
EXPERTS_PER_GROUP = 4
N_EXPERTS = N_GROUPS * EXPERTS_PER_GROUP
MASK_VALUE = -1e30
HG_CHUNK = 128
HG_STEP_CHUNKS = 4
HG_BASE = 16
ATT_BLOCK = 256
VT_PAD = 16
VT_ROWS = HEADS * (HEAD_W + VT_PAD)
FAR_UNROLL = 4
PAGES_PER_STEP = 16
LOG2E = math.log2(math.e)
VMEM_LIMIT = 56 * 1024 * 1024


def _dot(a, b):
    return jnp.dot(a, b, preferred_element_type=F32)


def _dot_nt(a, b):
    return lax.dot_general(a, b, (((1,), (1,)), ((), ())), preferred_element_type=F32)


def _dot_tn(a, b):
    return lax.dot_general(a, b, (((0,), (0,)), ((), ())), preferred_element_type=F32)


def _sigmoid(x):
    return 0.5 + 0.5 * jnp.tanh(0.5 * x)


def _silu(x):
    y = 0.5 * x
    return y + y * jnp.tanh(y)


def _rms_rows(x, w):
    return x * lax.rsqrt(jnp.mean(x * x, axis=-1, keepdims=True) + EPS) * w


def _const_spec(shape):
    zeros = (0,) * len(shape)
    return pl.BlockSpec(shape, lambda *_: zeros)


def _inproj_values(x_ref, n1_ref, w_ref, g_ref, qw_ref, kw_ref):
    xn = _rms_rows(x_ref[...], n1_ref[...]).astype(BF16)
    g = g_ref[...]

    def group_norm(z, w):
        sq = z * z
        hi = sq.astype(BF16)
        lo = (sq - hi.astype(F32)).astype(BF16)
        ms = _dot(hi, g) + _dot(lo, g)
        return z * lax.rsqrt(ms + EPS) * w

    zh = _dot(xn, w_ref[:, 0:2048])
    qn = group_norm(_dot(xn, w_ref[:, 2048:2560]), qw_ref[...])
    kn = group_norm(_dot(xn, w_ref[:, 2560:3072]), kw_ref[...])
    zv = _dot(xn, w_ref[:, 3072:3584])
    return zh, qn, kn, zv


def _inproj_small_kernel(x_ref, n1_ref, w_ref, g_ref, qw_ref, kw_ref,
                         zh_ref, q_ref, kf_ref, kb_ref, vf_ref, vb_ref):
    zh, qn, kn, zv = _inproj_values(x_ref, n1_ref, w_ref, g_ref, qw_ref, kw_ref)
    zh_ref[...] = zh
    q_ref[...] = qn.astype(BF16)
    kf_ref[...] = kn
    kb_ref[...] = kn.astype(BF16)
    vf_ref[...] = zv
    vb_ref[...] = zv.astype(BF16)


def _inproj_prompt_kernel(x_ref, n1_ref, w_ref, g_ref, qw_ref, kw_ref, kbase_ref, vbase_ref,
                          zh_ref, qt_ref, kf_ref, kb_ref, vf_ref, vt_ref):
    del kbase_ref, vbase_ref
    zh, qn, kn, zv = _inproj_values(x_ref, n1_ref, w_ref, g_ref, qw_ref, kw_ref)
    tm = zh.shape[0]
    zh_ref[...] = zh
    kb_ref[...] = kn.astype(BF16)
    for h in range(HEADS):
        hs = slice(h * HEAD_W, (h + 1) * HEAD_W)
        kf_ref[pl.ds(h, tm, stride=HEADS), :] = kn[:, hs]
        vf_ref[pl.ds(h, tm, stride=HEADS), :] = zv[:, hs]
    qt = qn.T
    vt = zv.T
    for c in range(tm // ATT_BLOCK):
        cols = slice(c * ATT_BLOCK, (c + 1) * ATT_BLOCK)
        qt_ref[c] = qt[:, cols].astype(BF16)
        for h in range(HEADS):
            r0 = h * (HEAD_W + VT_PAD)
            vt_ref[c, r0:r0 + HEAD_W, :] = vt[h * HEAD_W:(h + 1) * HEAD_W, cols].astype(BF16)
            vt_ref[c, r0 + HEAD_W:r0 + HEAD_W + VT_PAD, :] = jnp.ones((VT_PAD, ATT_BLOCK), BF16)


def _inproj_small(x2d, n1, w_a, gmat, qw, kw):
    n, d = x2d.shape
    full = lambda width: _const_spec((n, width))
    return pl.pallas_call(
        _inproj_small_kernel,
        grid=(1,),
        in_specs=[full(d), _const_spec((1, d)), _const_spec(w_a.shape), _const_spec(gmat.shape),
                  _const_spec((1, 512)), _const_spec((1, 512))],
        out_specs=[full(2048), full(512), full(512), full(512), full(512), full(512)],
        out_shape=[jax.ShapeDtypeStruct((n, 2048), F32), jax.ShapeDtypeStruct((n, 512), BF16),
                   jax.ShapeDtypeStruct((n, 512), F32), jax.ShapeDtypeStruct((n, 512), BF16),
                   jax.ShapeDtypeStruct((n, 512), F32), jax.ShapeDtypeStruct((n, 512), BF16)],
        compiler_params=pltpu.CompilerParams(dimension_semantics=("arbitrary",),
                                             vmem_limit_bytes=VMEM_LIMIT),
        name="inproj_small",
    )(x2d, n1, w_a, gmat, qw, kw)


def _inproj_prompt(x2d, n1, w_a, gmat, qw, kw, kbase, vbase, seq, tm):
    n, d = x2d.shape
    assert seq % tm == 0 and tm % ATT_BLOCK == 0
    per = tm // ATT_BLOCK
    tiles = seq // tm
    row = lambda width: pl.BlockSpec((tm, width), lambda i: (i, 0))
    t_spec = lambda rows: pl.BlockSpec((per, rows, ATT_BLOCK), lambda i: (i, 0, 0))
    t_shape = lambda rows: jax.ShapeDtypeStruct((n // ATT_BLOCK, rows, ATT_BLOCK), BF16)

    def cache_rows(i):
        tok = (i // tiles) * (N_META + seq) + N_META + (i % tiles) * tm
        return pl.multiple_of(tok * HEADS, 8 * HEADS), 0

    cache_spec = pl.BlockSpec((pl.Element(tm * HEADS), pl.Element(HEAD_W)), cache_rows)
    any_spec = pl.BlockSpec(memory_space=pl.ANY)
    return pl.pallas_call(
        _inproj_prompt_kernel,
        grid=(n // tm,),
        in_specs=[row(d), _const_spec((1, d)), _const_spec(w_a.shape), _const_spec(gmat.shape),
                  _const_spec((1, 512)), _const_spec((1, 512)), any_spec, any_spec],
        out_specs=[row(2048), t_spec(512), cache_spec, row(512), cache_spec, t_spec(VT_ROWS)],
        out_shape=[jax.ShapeDtypeStruct((n, 2048), F32), t_shape(512),
                   jax.ShapeDtypeStruct(kbase.shape, F32), jax.ShapeDtypeStruct((n, 512), BF16),
                   jax.ShapeDtypeStruct(vbase.shape, F32), t_shape(VT_ROWS)],
        input_output_aliases={6: 2, 7: 4},
        compiler_params=pltpu.CompilerParams(dimension_semantics=("parallel",),
                                             vmem_limit_bytes=VMEM_LIMIT),
        name="inproj",
    )(x2d, n1, w_a, gmat, qw, kw, kbase, vbase)


def _hgrn_kernel(z_ref, lbl_ref, onw_ref, s0_ref, o_ref, sout_ref, st_scr, *, rows):
    C = HG_CHUNK
    W = HEADS * HEAD_W
    c = pl.program_id(1)
    heads = [slice(h * HEAD_W, (h + 1) * HEAD_W) for h in range(HEADS)]

    @pl.when(c == 0)
    def _():
        for h in range(HEADS):
            st_scr[h] = s0_ref[0, h].T

    lg = lbl_ref[...]
    e = jnp.exp(lg - jnp.max(lg, axis=0, keepdims=True))
    lb = e[0:1] / jnp.sum(e, axis=0, keepdims=True)
    half = 0.5 * (1.0 - lb)
    onw = onw_ref[...]
    row = lax.broadcasted_iota(jnp.int32, (C, C), 0)
    col = lax.broadcasted_iota(jnp.int32, (C, C), 1)
    tri = jnp.where(col <= row, 1.0, 0.0).astype(BF16)

    def shr(x, width):
        return lax.shift_right_logical(x, int(math.log2(width)))

    def front(z, n_valid):
        hq, hf, hi, hg = z[:, 0:W], z[:, W:2 * W], z[:, 2 * W:3 * W], z[:, 3 * W:4 * W]
        q = _silu(hq)
        f = (lb + half) + half * jnp.tanh(0.5 * hf)
        k = 1.0 - f
        logf = jnp.log2(f)
        if n_valid < C:
            valid = lax.broadcasted_iota(jnp.int32, (C, W), 0) < n_valid
            k = jnp.where(valid, k, 0.0)
            logf = jnp.where(valid, logf, 0.0)
        x1 = logf.astype(BF16)
        r1 = logf - x1.astype(F32)
        x2 = r1.astype(BF16)
        x3 = (r1 - x2.astype(F32)).astype(BF16)
        b = _dot(tri, x1) + _dot(tri, x2) + _dot(tri, x3)
        return q, k, hi.astype(BF16), b, _silu(hg)

    def intra(q, k, b):
        b_last = b[C - 1:C]

        def block_row(width, r):
            b3 = b.reshape(C // width, width, W)
            return jnp.broadcast_to(b3[:, r:r + 1, :], b3.shape).reshape(C, W)

        def next_block_first(width):
            b3 = b.reshape(C // width, width, W)
            nxt = jnp.concatenate([b3[1:, 0:1, :], b_last[None]], axis=0)
            return jnp.broadcast_to(nxt, b3.shape).reshape(C, W)

        mid = block_row(HG_BASE, HG_BASE // 2)
        qe = (q * jnp.exp2(b - mid)).astype(BF16)
        ke = (k * jnp.exp2(mid - b)).astype(BF16)
        mask = (shr(row, HG_BASE) == shr(col, HG_BASE)) & (col <= row)
        scores = [jnp.where(mask, _dot_nt(qe[:, hs], ke[:, hs]), 0.0) for hs in heads]
        width = HG_BASE
        while width < C:
            qe = (q * jnp.exp2(b - block_row(width, 0))).astype(BF16)
            ke = (k * jnp.exp2(next_block_first(width) - b)).astype(BF16)
            mask = ((shr(row, width) == shr(col, width) + 1)
                    & (shr(row, 2 * width) == shr(col, 2 * width)))
            scores = [jnp.where(mask, _dot_nt(qe[:, hs], ke[:, hs]), sc)
                      for hs, sc in zip(heads, scores)]
            width *= 2
        qb = (q * jnp.exp2(b)).astype(BF16)
        kd = (k * jnp.exp2(b_last - b)).astype(BF16)
        return [sc.astype(BF16) for sc in scores], qb, kd, jnp.exp2(b_last)

    n_sub = max(rows // C, 1)
    n_valid = min(rows, C)
    fronts = []
    for u in range(n_sub):
        z = z_ref[0, u * n_valid:(u + 1) * n_valid, :]
        if n_valid < C:
            z = jnp.concatenate([z, jnp.zeros((C - n_valid, z.shape[1]), F32)], axis=0)
        fronts.append(front(z, n_valid))
    intras = [intra(q, k, b) for q, k, _, b, _ in fronts]
    for u in range(n_sub):
        _, _, v_bf, _, gate = fronts[u]
        scores, qb, kd, decay = intras[u]
        outs = []
        for h, hs in enumerate(heads):
            st = st_scr[h]
            outs.append(_dot_nt(qb[:, hs], st.astype(BF16)) + _dot(scores[h], v_bf[:, hs]))
            st_scr[h] = st * decay[:, hs] + _dot_tn(v_bf[:, hs], kd[:, hs])
        for h, hs in enumerate(heads):
            on = _rms_rows(outs[h], onw[:, hs]) * gate[:, hs]
            o_ref[0, u * n_valid:(u + 1) * n_valid, hs] = on[:n_valid].astype(o_ref.dtype)

    @pl.when(c == pl.num_programs(1) - 1)
    def _():
        for h in range(HEADS):
            sout_ref[0, h] = st_scr[h].T


def _hgrn(zh, lbl, onw, s0, *, shared_state, out_dtype):
    nseq, t, _ = zh.shape
    rows = min(t, HG_STEP_CHUNKS * HG_CHUNK)
    assert t % rows == 0 and rows % 8 == 0 and (rows <= HG_CHUNK or rows % HG_CHUNK == 0)
    s_map = (lambda b, c: (0, 0, 0, 0)) if shared_state else (lambda b, c: (b, 0, 0, 0))
    return pl.pallas_call(
        functools.partial(_hgrn_kernel, rows=rows),
        grid=(nseq, t // rows),
        in_specs=[pl.BlockSpec((1, rows, 2048), lambda b, c: (b, c, 0)),
                  _const_spec(lbl.shape), _const_spec((1, 512)),
                  pl.BlockSpec((1, HEADS, HEAD_W, HEAD_W), s_map)],
        out_specs=[pl.BlockSpec((1, rows, 512), lambda b, c: (b, c, 0)),
                   pl.BlockSpec((1, HEADS, HEAD_W, HEAD_W), lambda b, c: (b, 0, 0, 0))],
        out_shape=[jax.ShapeDtypeStruct((nseq, t, 512), out_dtype),
                   jax.ShapeDtypeStruct((nseq, HEADS, HEAD_W, HEAD_W), F32)],
        scratch_shapes=[pltpu.VMEM((HEADS, HEAD_W, HEAD_W), F32)],
        compiler_params=pltpu.CompilerParams(dimension_semantics=("parallel", "arbitrary"),
                                             vmem_limit_bytes=VMEM_LIMIT),
        name="hgrn",
    )(zh, lbl, onw, s0)


def _lambda_value(lp, lam_init):
    s1 = jnp.sum(lp[0:1] * lp[1:2], axis=1, keepdims=True)
    s2 = jnp.sum(lp[2:3] * lp[3:4], axis=1, keepdims=True)
    return jnp.exp(s1) - jnp.exp(s2) + lam_init


def _softmax_update(s, v_bf, m_ref, l_ref, acc_ref, first):
    m_cur = jnp.max(s, axis=1, keepdims=True)
    if first:
        m_new = m_cur
        p = jnp.exp2(s - m_new)
        l_ref[...] = jnp.sum(p, axis=1, keepdims=True)
        acc_ref[...] = _dot(p.astype(BF16), v_bf)
    else:
        m_prev = m_ref[...]
        m_new = jnp.maximum(m_prev, m_cur)
        alpha = jnp.exp2(m_prev - m_new)
        p = jnp.exp2(s - m_new)
        l_ref[...] = alpha * l_ref[...] + jnp.sum(p, axis=1, keepdims=True)
        acc_ref[...] = alpha * acc_ref[...] + _dot(p.astype(BF16), v_bf)
    m_ref[...] = m_new


def _pattn_kernel(qt_ref, k_ref, vt_ref, km_ref, vmt_ref, tb_ref, tm_ref, lam_ref, onw_ref, o_ref,
                  qz_scr, m_scr, acc_scr, *, lam_init):
    blk = ATT_BLOCK
    i = pl.program_id(1)
    sub = lax.broadcasted_iota(jnp.int32, (HEAD_W, blk), 0)
    for h in range(HEADS):
        qh = qt_ref[0, h * HEAD_W:(h + 1) * HEAD_W, :]
        zero = jnp.zeros_like(qh)
        qz_scr[h] = jnp.concatenate([jnp.where(sub < DA_DH, qh, zero),
                                     jnp.where(sub >= DA_DH, qh, zero)], axis=1)

    hv = lambda h: slice(h * (HEAD_W + VT_PAD), (h + 1) * (HEAD_W + VT_PAD))

    def key_blocks(blocks):
        def qk(j, table, h):
            hs = slice(h * HEAD_W, (h + 1) * HEAD_W)
            off = pl.multiple_of(j * blk, blk)
            s = _dot(k_ref[0, pl.ds(off, blk), hs], qz_scr[h])
            return s if table is None else s + tb_ref[table, h]

        def soft(h, s):
            m_prev = m_scr[h]
            m_new = jnp.maximum(m_prev, jnp.max(s, axis=0, keepdims=True))
            m_scr[h] = m_new
            return jnp.exp2(s - m_new).astype(BF16), jnp.exp2(m_prev - m_new)

        def pv(j, h, p, alpha):
            acc_scr[h] = alpha * acc_scr[h] + _dot(vt_ref[j, hv(h), :], p)

        stages = [(j, table, h) for j, table in blocks for h in range(HEADS)]
        s_cur = qk(*stages[0])
        pend = None
        for n, (j, table, h) in enumerate(stages):
            s_next = qk(*stages[n + 1]) if n + 1 < len(stages) else None
            if pend is not None:
                pv(stages[n - 1][0], stages[n - 1][2], *pend)
            pend = soft(h, s_cur)
            s_cur = s_next
        pv(stages[-1][0], stages[-1][2], *pend)

    meta_sel = jnp.minimum(i, 1)
    for h in range(HEADS):
        hs = slice(h * HEAD_W, (h + 1) * HEAD_W)
        s = _dot(km_ref[:, hs], qz_scr[h]) + tm_ref[meta_sel, h]
        m_new = jnp.max(s, axis=0, keepdims=True)
        m_scr[h] = m_new
        acc_scr[h] = _dot(vmt_ref[hv(h), :], jnp.exp2(s - m_new).astype(BF16))

    n_far = jnp.maximum(i - 1, 0)
    n_single = n_far % FAR_UNROLL

    def single_body(j, carry):
        key_blocks([(j, None)])
        return carry

    lax.fori_loop(0, n_single, single_body, 0)

    def far_body(t, carry):
        j = n_single + FAR_UNROLL * t
        key_blocks([(j + u, None) for u in range(FAR_UNROLL)])
        return carry

    lax.fori_loop(0, n_far // FAR_UNROLL, far_body, 0)

    @pl.when(i >= 1)
    def _():
        key_blocks([(i - 1, 1), (i, 0)])

    @pl.when(i == 0)
    def _():
        key_blocks([(0, 0)])

    lam = _lambda_value(lam_ref[...], lam_init)
    onw = onw_ref[...]
    for h in range(HEADS):
        acc = acc_scr[h]
        a = acc[:HEAD_W] / acc[HEAD_W:HEAD_W + 1]
        ot = a[:, :blk] - lam * a[:, blk:]
        ot = ot * lax.rsqrt(jnp.mean(ot * ot, axis=0, keepdims=True) + EPS) * onw * (1.0 - lam_init)
        o_ref[0, :, h * HEAD_W:(h + 1) * HEAD_W] = ot.T.astype(o_ref.dtype)


def _pattn(qt, k, vt, km, vmt, tb, tmeta, lam_p, onw_col, lam_init):
    bsz, t, w = k.shape
    blk = ATT_BLOCK
    nblk = t // blk
    assert t % blk == 0
    return pl.pallas_call(
        functools.partial(_pattn_kernel, lam_init=lam_init),
        grid=(bsz, nblk),
        in_specs=[pl.BlockSpec((1, w, blk), lambda b, i: (b * nblk + i, 0, 0)),
                  pl.BlockSpec((1, t, w), lambda b, i: (b, 0, 0)),
                  pl.BlockSpec((nblk, VT_ROWS, blk), lambda b, i: (b, 0, 0)),
                  _const_spec(km.shape), _const_spec(vmt.shape), _const_spec(tb.shape),
                  _const_spec(tmeta.shape), _const_spec(lam_p.shape), _const_spec(onw_col.shape)],
        out_specs=pl.BlockSpec((1, blk, w), lambda b, i: (b, i, 0)),
        out_shape=jax.ShapeDtypeStruct((bsz, t, w), BF16),
        scratch_shapes=[pltpu.VMEM((HEADS, HEAD_W, 2 * blk), BF16),
                        pltpu.VMEM((HEADS, 1, 2 * blk), F32),
                        pltpu.VMEM((HEADS, HEAD_W + VT_PAD, 2 * blk), F32)],
        compiler_params=pltpu.CompilerParams(dimension_semantics=("parallel", "arbitrary"),
                                             vmem_limit_bytes=VMEM_LIMIT),
        name="prompt_attn",
    )(qt, k, vt, km, vmt, tb, tmeta, lam_p, onw_col)


def _sattn_kernel(pt_ref, q_ref, kn_ref, vn_ref, tl_ref, tn_ref, tmask_ref, lam_ref, onw_ref, *rest,
                  lam_init, n_tok):
    del pt_ref
    pg = PAGES_PER_STEP
    k_refs, v_refs = rest[:pg], rest[pg:2 * pg]
    o_ref, m_scr, l_scr, acc_scr = rest[2 * pg:]
    g = pl.program_id(1)
    q = q_ref[0]

    @pl.when(g == 0)
    def _():
        s = _dot_nt(q, kn_ref[0]) + tn_ref[...]
        _softmax_update(s, vn_ref[0], m_scr, l_scr, acc_scr, True)

    is_last = g == pl.num_programs(1) - 1
    tmask = tmask_ref[...]
    ss = []
    for u in range(pg):
        s = _dot_nt(q, k_refs[u][...].astype(BF16))
        ss.append(s + (jnp.where(is_last, tl_ref[...], tmask) if u == pg - 1 else tmask))
    m_prev = m_scr[...]
    m_new = m_prev
    for s in ss:
        m_new = jnp.maximum(m_new, jnp.max(s, axis=1, keepdims=True))
    alpha = jnp.exp2(m_prev - m_new)
    l_new = alpha * l_scr[...]
    acc = alpha * acc_scr[...]
    for u in range(pg):
        p = jnp.exp2(ss[u] - m_new)
        l_new = l_new + jnp.sum(p, axis=1, keepdims=True)
        acc = acc + _dot(p.astype(BF16), v_refs[u][...].astype(BF16))
    m_scr[...] = m_new
    l_scr[...] = l_new
    acc_scr[...] = acc

    @pl.when(is_last)
    def _():
        lam = _lambda_value(lam_ref[...], lam_init)
        onw = onw_ref[...]
        a = acc_scr[...] / l_scr[...]
        for h in range(HEADS):
            r1 = (2 * h) * n_tok
            r2 = (2 * h + 1) * n_tok
            o = a[r1:r1 + n_tok] - lam * a[r2:r2 + n_tok]
            o_ref[0, :, h * HEAD_W:(h + 1) * HEAD_W] = (
                _rms_rows(o, onw) * (1.0 - lam_init)).astype(o_ref.dtype)


def _sattn(page_table, qall, knew, vnew, tlast, tnew, tmask, lam_p, onw, ck2, cv2, lam_init, n_tok):
    nb, n_pages = page_table.shape
    pg = PAGES_PER_STEP
    assert n_pages % pg == 0
    nrow = qall.shape[1]
    prow = knew.shape[1]
    pt_flat = page_table.reshape(-1)

    def page_spec(u):
        return pl.BlockSpec((prow, HEAD_W), lambda b, g, pt: (pt[b * n_pages + g * pg + u], 0))

    per_b = lambda shape: pl.BlockSpec((1,) + shape, lambda b, g, pt: (b, 0, 0))
    const = lambda shape: pl.BlockSpec(shape, lambda b, g, pt: (0,) * len(shape))
    grid_spec = pltpu.PrefetchScalarGridSpec(
        num_scalar_prefetch=1,
        grid=(nb, n_pages // pg),
        in_specs=[per_b((nrow, HEAD_W)), per_b((prow, HEAD_W)), per_b((prow, HEAD_W)),
                  const(tlast.shape), const(tnew.shape), const(tmask.shape),
                  const(lam_p.shape), const(onw.shape)]
                 + [page_spec(u) for u in range(pg)] + [page_spec(u) for u in range(pg)],
        out_specs=per_b((n_tok, HEADS * HEAD_W)),
        scratch_shapes=[pltpu.VMEM((nrow, 1), F32), pltpu.VMEM((nrow, 1), F32),
                        pltpu.VMEM((nrow, HEAD_W), F32)],
    )
    return pl.pallas_call(
        functools.partial(_sattn_kernel, lam_init=lam_init, n_tok=n_tok),
        grid_spec=grid_spec,
        out_shape=jax.ShapeDtypeStruct((nb, n_tok, HEADS * HEAD_W), F32),
        compiler_params=pltpu.CompilerParams(dimension_semantics=("parallel", "arbitrary"),
                                             vmem_limit_bytes=VMEM_LIMIT),
        name="sample_attn",
    )(pt_flat, qall, knew, vnew, tlast, tnew, tmask, lam_p, onw, *([ck2] * pg), *([cv2] * pg))


def _merge_kernel(x_ref, oa_ref, ob_ref, n1_ref, wg_ref, wha_ref, wda_ref, wo_ref, h_ref):
    x = x_ref[...]
    d = x.shape[1]
    xn = _rms_rows(x, n1_ref[...]).astype(BF16)
    gates = _dot(xn, wg_ref[...])
    ya = _dot(oa_ref[...], wha_ref[...])
    yb = _dot(ob_ref[...], wda_ref[...])
    merged = _sigmoid(gates[:, :d]) * ya + _sigmoid(gates[:, d:]) * yb
    h_ref[...] = x + _dot(merged.astype(BF16), wo_ref[...])


def _merge(x2d, oa, ob, n1, w_g, w_ha, w_da, w_o, tm):
    n, d = x2d.shape
    row = lambda width: pl.BlockSpec((tm, width), lambda i: (i, 0))
    return pl.pallas_call(
        _merge_kernel,
        grid=(n // tm,),
        in_specs=[row(d), row(512), row(512), _const_spec((1, d)), _const_spec(w_g.shape),
                  _const_spec(w_ha.shape), _const_spec(w_da.shape), _const_spec(w_o.shape)],
        out_specs=row(d),
        out_shape=jax.ShapeDtypeStruct((n, d), F32),
        compiler_params=pltpu.CompilerParams(dimension_semantics=("parallel",),
                                             vmem_limit_bytes=VMEM_LIMIT),
        name="merge",
    )(x2d, oa, ob, n1, w_g, w_ha, w_da, w_o)


def _moe_kernel(h_ref, n2_ref, wr_hi_ref, wr_lo_ref, br_ref, wgu_ref, wd_ref, y_ref):
    h = h_ref[...]
    tm = h.shape[0]
    hn = _rms_rows(h, n2_ref[...])
    hn_hi = hn.astype(BF16)
    hn_lo = (hn - hn_hi.astype(F32)).astype(BF16)
    logits = (_dot(hn_hi, wr_hi_ref[...]) + _dot(hn_lo, wr_hi_ref[...])
              + _dot(hn_hi, wr_lo_ref[...]) + br_ref[...])
    lane = lax.broadcasted_iota(jnp.int32, logits.shape, 1)
    big = jnp.int32(1 << 20)
    ninf = -jnp.inf

    def first_lane(mask):
        return jnp.min(jnp.where(mask, lane, big), axis=1, keepdims=True)

    is_g = lane < N_GROUPS
    g_max = jnp.max(jnp.where(is_g, logits, ninf), axis=1, keepdims=True)
    g_idx = first_lane(is_g & (logits == g_max))
    g_sum = jnp.sum(jnp.where(is_g, jnp.exp(logits - g_max), 0.0), axis=1, keepdims=True)
    p_g = 1.0 / g_sum
    e_lane = lane - N_GROUPS
    in_group = ((e_lane >= 0) & (e_lane < N_EXPERTS)
                & (lax.shift_right_arithmetic(e_lane, 2) == g_idx))
    e_max = jnp.max(jnp.where(in_group, logits, ninf), axis=1, keepdims=True)
    e_exp = jnp.where(in_group, jnp.exp(logits - e_max), 0.0)
    e_prob = e_exp / jnp.sum(e_exp, axis=1, keepdims=True)
    p1 = jnp.max(jnp.where(in_group, e_prob, -1.0), axis=1, keepdims=True)
    i1 = first_lane(in_group & (e_prob == p1))
    rest = in_group & (lane != i1)
    p2 = jnp.max(jnp.where(rest, e_prob, -1.0), axis=1, keepdims=True)
    i2 = first_lane(rest & (e_prob == p2))
    denom = p1 + p2
    combine = (jnp.where(lane == i1, p_g * p1 / denom, 0.0)
               + jnp.where(lane == i2, p_g * p2 / denom, 0.0))

    ff = wd_ref.shape[1]
    acc = jnp.zeros_like(h)
    for ex in range(N_EXPERTS):
        gu = _dot(hn_hi, wgu_ref[ex])
        gt, up = gu[:, :ff], gu[:, ff:]
        hh = _silu(gt) * up * combine[:, N_GROUPS + ex:N_GROUPS + ex + 1]
        acc = acc + _dot(hh.astype(BF16), wd_ref[ex])
    y_ref[...] = h + acc


def _moe(h2d, n2, wr_hi, wr_lo, br, wgu, wd, tm):
    n, d = h2d.shape
    row = pl.BlockSpec((tm, d), lambda i: (i, 0))
    single = lambda shape: pl.BlockSpec(shape, lambda i: (0,) * len(shape),
                                        pipeline_mode=pl.Buffered(1))
    return pl.pallas_call(
        _moe_kernel,
        grid=(n // tm,),
        in_specs=[row, _const_spec((1, d)), _const_spec(wr_hi.shape), _const_spec(wr_lo.shape),
                  _const_spec(br.shape), single(wgu.shape), single(wd.shape)],
        out_specs=row,
        out_shape=jax.ShapeDtypeStruct((n, d), F32),
        compiler_params=pltpu.CompilerParams(dimension_semantics=("parallel",),
                                             vmem_limit_bytes=VMEM_LIMIT),
        name="moe",
    )(h2d, n2, wr_hi, wr_lo, br, wgu, wd)


def _t5_bias(dist, rel_bias):
    n = jnp.maximum(dist, 0)
    max_exact = N_BUCKETS // 2
    nf = jnp.maximum(n, 1).astype(F32)
    large = max_exact + (jnp.log(nf / max_exact) / math.log(MAX_DISTANCE / max_exact)
                         * (N_BUCKETS - max_exact)).astype(jnp.int32)
    large = jnp.minimum(large, N_BUCKETS - 1)
    bucket = jnp.where(n < max_exact, n, large)
    rb = rel_bias.astype(F32)
    onehot = (bucket[..., None] == jnp.arange(N_BUCKETS)).astype(F32)
    bias = jnp.moveaxis(jnp.dot(onehot, rb, precision=lax.Precision.HIGHEST), -1, 0)
    far = rb[N_BUCKETS - 1].reshape((HEADS,) + (1,) * dist.ndim)
    return jnp.where(dist >= 0, (bias - far) * LOG2E, MASK_VALUE)


def kernel(x_prompt, x_sample, cache_k, cache_v, state_hgrn, page_table, meta_tokens, rel_bias,
           hg_lb_logits, norm1_w, w_in, hg_onorm_w, w_hg_out, q_norm_w, k_norm_w, da_lambda,
           da_onorm_w, w_da_out, w_o, norm2_w, w_router_group, b_router_group,
           w_router_expert, b_router_expert, w_gate, w_up, w_down):
    bsz, seq, d = x_prompt.shape
    nb, n_tok, _ = x_sample.shape
    depth = w_in.shape[0]
    assert depth == 1 and hg_lb_logits.shape[0] == 2
    n_pages = page_table.shape[1]
    psize = cache_k.shape[2]
    past_len = n_pages * psize
    blk = ATT_BLOCK
    assert blk >= MAX_DISTANCE and psize >= MAX_DISTANCE and n_tok <= 8
    lam_init = 0.8 - 0.6 * math.exp(-0.3 * 0)
    w_attn = HEADS * HEAD_W

    w_in0 = w_in[0]
    w_a = w_in0[:, :3584].astype(BF16)
    w_g = w_in0[:, 3584:].astype(BF16)
    n1 = norm1_w[0].reshape(1, d)
    n2 = norm2_w[0].reshape(1, d)
    qw = (jnp.tile(q_norm_w[0], 2 * HEADS) * (DA_DH ** -0.5 * LOG2E)).reshape(1, w_attn)
    kw = jnp.tile(k_norm_w[0], 2 * HEADS).reshape(1, w_attn)
    gi = jnp.arange(w_attn) // DA_DH
    gmat = jnp.where(gi[:, None] == gi[None, :], 1.0 / DA_DH, 0.0).astype(BF16)
    hg_onw = jnp.tile(hg_onorm_w[0], HEADS).reshape(1, w_attn)
    da_onw = da_onorm_w[0].reshape(1, HEAD_W)
    lam_p = da_lambda[0].astype(F32)
    w_ha = w_hg_out[0].astype(BF16)
    w_da = w_da_out[0].astype(BF16)
    w_o0 = w_o[0].astype(BF16)
    wr = jnp.concatenate([w_router_group[0], w_router_expert[0],
                          jnp.zeros((d, 128 - N_GROUPS - N_EXPERTS), F32)], axis=1)
    wr_hi = wr.astype(BF16)
    wr_lo = (wr - wr_hi.astype(F32)).astype(BF16)
    br = jnp.concatenate([b_router_group[0], b_router_expert[0],
                          jnp.zeros((128 - N_GROUPS - N_EXPERTS,), F32)]).reshape(1, 128)
    wgu = jnp.concatenate([w_gate[0], w_up[0]], axis=2).astype(BF16)
    wd = w_down[0].astype(BF16)

    n_small = N_META + nb * n_tok
    x_small = jnp.concatenate([meta_tokens.astype(F32), x_sample.reshape(nb * n_tok, d)], axis=0)
    zh_s, q_s, kf_s, kb_s, vf_s, vb_s = _inproj_small(x_small, n1, w_a, gmat, qw, kw)

    _, s_meta = _hgrn(zh_s[:N_META][None], hg_lb_logits, hg_onw,
                      jnp.zeros((1, HEADS, HEAD_W, HEAD_W), F32), shared_state=False, out_dtype=F32)
    oa_s, s_sample = _hgrn(zh_s[N_META:].reshape(nb, n_tok, 2048), hg_lb_logits, hg_onw,
                           state_hgrn[0], shared_state=False, out_dtype=F32)

    x_p = x_prompt.reshape(bsz * seq, d)
    def cache_base(meta_rows):
        meta_b = jnp.broadcast_to(meta_rows.reshape(1, N_META * HEADS, HEAD_W),
                                  (bsz, N_META * HEADS, HEAD_W))
        return jnp.concatenate([meta_b, jnp.zeros((bsz, seq * HEADS, HEAD_W), F32)],
                               axis=1).reshape(-1, HEAD_W)

    zh_p, qt_p, kf_p, kb_p, vf_p, vt_p = _inproj_prompt(
        x_p, n1, w_a, gmat, qw, kw, cache_base(kf_s[:N_META]), cache_base(vf_s[:N_META]), seq, 512)
    oa_p, s_prompt = _hgrn(zh_p.reshape(bsz, seq, 2048), hg_lb_logits, hg_onw, s_meta,
                           shared_state=True, out_dtype=BF16)

    r = jnp.arange(blk)
    tb = jnp.stack([_t5_bias(r[None, :] - r[:, None], rel_bias),
                    _t5_bias(blk + r[None, :] - r[:, None], rel_bias)])
    tb = jnp.concatenate([tb, tb], axis=3)
    mrow = jnp.arange(128)
    t_meta0 = _t5_bias(N_META + r[None, :] - mrow[:, None], rel_bias)
    t_meta = jnp.stack([t_meta0, jnp.zeros_like(t_meta0)])
    t_meta = jnp.where(mrow[:, None] < N_META, t_meta, MASK_VALUE)
    t_meta = jnp.concatenate([t_meta, t_meta], axis=3)
    pad_meta = lambda a: jnp.concatenate([a, jnp.zeros((128 - N_META, w_attn), a.dtype)], axis=0)
    vmt = pad_meta(vb_s[:N_META]).T.reshape(HEADS, HEAD_W, 128)
    vmt = jnp.concatenate([vmt, jnp.ones((HEADS, VT_PAD, 128), BF16)], axis=1).reshape(VT_ROWS, 128)
    ob_p = _pattn(qt_p, kb_p.reshape(bsz, seq, w_attn), vt_p, pad_meta(kb_s[:N_META]),
                  vmt, tb, t_meta, lam_p, da_onw.reshape(HEAD_W, 1), lam_init)

    q_tok = q_s[N_META:].reshape(nb, n_tok, HEADS, 1, 2, DA_DH)
    q_maps = q_tok * jnp.eye(2, dtype=BF16).reshape(1, 1, 1, 2, 2, 1)
    n_rows = 2 * HEADS * n_tok
    qall = jnp.transpose(q_maps, (0, 2, 3, 1, 4, 5)).reshape(nb, n_rows, HEAD_W)
    prow = psize * HEADS
    pad_new = lambda a: jnp.concatenate(
        [a.reshape(nb, n_tok * HEADS, HEAD_W),
         jnp.zeros((nb, prow - n_tok * HEADS, HEAD_W), a.dtype)], axis=1)
    s_idx = jnp.tile(jnp.arange(n_tok), 2 * HEADS)
    head_of_row = jnp.repeat(jnp.arange(HEADS), 2 * n_tok)
    key_r = jnp.arange(prow) // HEADS
    own_head = head_of_row[:, None] == (jnp.arange(prow) % HEADS)[None, :]
    dist_last = (past_len + s_idx[:, None]) - (past_len - psize + key_r[None, :])
    dist_new = jnp.where(key_r[None, :] < n_tok, s_idx[:, None] - key_r[None, :], -1)
    pick = lambda t: jnp.where(own_head, t[head_of_row, jnp.arange(n_rows)], MASK_VALUE)
    t_last = pick(_t5_bias(dist_last, rel_bias))
    t_new = pick(_t5_bias(dist_new, rel_bias))
    t_mask = jnp.where(own_head, 0.0, MASK_VALUE).astype(F32)
    ob_s = _sattn(page_table, qall, pad_new(kb_s[N_META:]), pad_new(vb_s[N_META:]), t_last, t_new,
                  t_mask, lam_p, da_onw, cache_k.reshape(-1, HEAD_W), cache_v.reshape(-1, HEAD_W),
                  lam_init, n_tok)

    h_p = _merge(x_p, oa_p.reshape(bsz * seq, w_attn), ob_p.reshape(bsz * seq, w_attn),
                 n1, w_g, w_ha, w_da, w_o0, 512)
    y_p = _moe(h_p, n2, wr_hi, wr_lo, br, wgu, wd, 512)
    x_s = x_small[N_META:]
    h_s = _merge(x_s, oa_s.reshape(nb * n_tok, w_attn).astype(BF16),
                 ob_s.reshape(nb * n_tok, w_attn).astype(BF16), n1, w_g, w_ha, w_da, w_o0,
                 nb * n_tok)
    y_s = _moe(h_s, n2, wr_hi, wr_lo, br, wgu, wd, nb * n_tok)

    return (y_p.reshape(bsz, seq, d),
            y_s.reshape(nb, n_tok, d),
            kf_p.reshape(1, bsz, seq + N_META, HEADS, HEAD_W),
            vf_p.reshape(1, bsz, seq + N_META, HEADS, HEAD_W),
            s_prompt[None],
            kf_s[N_META:].reshape(1, nb, n_tok, HEADS, HEAD_W),
            vf_s[N_META:].reshape(1, nb, n_tok, HEADS, HEAD_W),
            s_sample[None])
```

```python
import functools
import math

import jax
import jax.numpy as jnp
from jax import lax
from jax.experimental import pallas as pl
from jax.experimental.pallas import tpu as pltpu

F32 = jnp.float32
BF16 = jnp.bfloat16

EPS = 1e-6
N_META = 16
HEADS = 4
HEAD_W = 128
DA_DH = 64
N_BUCKETS = 32
MAX_DISTANCE = 128
N_GROUPS = 4
EXPERTS_PER_GROUP = 4
N_EXPERTS = N_GROUPS * EXPERTS_PER_GROUP
MASK_VALUE = -1e30
HG_CHUNK = 128
HG_STEP_CHUNKS = 4
HG_BASE = 16
ATT_BLOCK = 256
VT_PAD = 16
VT_ROWS = HEADS * (HEAD_W + VT_PAD)
FAR_UNROLL = 8
PAGES_PER_STEP = 16
LOG2E = math.log2(math.e)
VMEM_LIMIT = 56 * 1024 * 1024


def _dot(a, b):
    return jnp.dot(a, b, preferred_element_type=F32)


def _dot_nt(a, b):
    return lax.dot_general(a, b, (((1,), (1,)), ((), ())), preferred_element_type=F32)


def _dot_tn(a, b):
    return lax.dot_general(a, b, (((0,), (0,)), ((), ())), preferred_element_type=F32)


def _sigmoid(x):
    return 0.5 + 0.5 * jnp.tanh(0.5 * x)


def _silu(x):
    y = 0.5 * x
    return y + y * jnp.tanh(y)


def _rms_rows(x, w):
    return x * lax.rsqrt(jnp.mean(x * x, axis=-1, keepdims=True) + EPS) * w


def _const_spec(shape):
    zeros = (0,) * len(shape)
    return pl.BlockSpec(shape, lambda *_: zeros)


def _inproj_values(x_ref, n1_ref, w_ref, g_ref, qw_ref, kw_ref):
    xn = _rms_rows(x_ref[...], n1_ref[...]).astype(BF16)
    g = g_ref[...]

    def group_norm(z, w):
        sq = z * z
        hi = sq.astype(BF16)
        lo = (sq - hi.astype(F32)).astype(BF16)
        ms = _dot(hi, g) + _dot(lo, g)
        return z * lax.rsqrt(ms + EPS) * w

    zh = _dot(xn, w_ref[:, 0:2048])
    qn = group_norm(_dot(xn, w_ref[:, 2048:2560]), qw_ref[...])
    kn = group_norm(_dot(xn, w_ref[:, 2560:3072]), kw_ref[...])
    zv = _dot(xn, w_ref[:, 3072:3584])
    return zh, qn, kn, zv


def _inproj_small_kernel(x_ref, n1_ref, w_ref, g_ref, qw_ref, kw_ref,
                         zh_ref, q_ref, kf_ref, kb_ref, vf_ref, vb_ref):
    zh, qn, kn, zv = _inproj_values(x_ref, n1_ref, w_ref, g_ref, qw_ref, kw_ref)
    zh_ref[...] = zh
    q_ref[...] = qn.astype(BF16)
    kf_ref[...] = kn
    kb_ref[...] = kn.astype(BF16)
    vf_ref[...] = zv
    vb_ref[...] = zv.astype(BF16)


def _inproj_prompt_kernel(x_ref, n1_ref, w_ref, g_ref, qw_ref, kw_ref, kbase_ref, vbase_ref,
                          zh_ref, qt_ref, kf_ref, kb_ref, vf_ref, vt_ref):
    del kbase_ref, vbase_ref
    zh, qn, kn, zv = _inproj_values(x_ref, n1_ref, w_ref, g_ref, qw_ref, kw_ref)
    tm = zh.shape[0]
    zh_ref[...] = zh
    kb_ref[...] = kn.astype(BF16)
    for h in range(HEADS):
        hs = slice(h * HEAD_W, (h + 1) * HEAD_W)
        kf_ref[pl.ds(h, tm, stride=HEADS), :] = kn[:, hs]
        vf_ref[pl.ds(h, tm, stride=HEADS), :] = zv[:, hs]
    qt = qn.T
    vt = zv.T
    for c in range(tm // ATT_BLOCK):
        cols = slice(c * ATT_BLOCK, (c + 1) * ATT_BLOCK)
        qt_ref[c] = qt[:, cols].astype(BF16)
        for h in range(HEADS):
            r0 = h * (HEAD_W + VT_PAD)
            vt_ref[c, r0:r0 + HEAD_W, :] = vt[h * HEAD_W:(h + 1) * HEAD_W, cols].astype(BF16)
            vt_ref[c, r0 + HEAD_W:r0 + HEAD_W + VT_PAD, :] = jnp.ones((VT_PAD, ATT_BLOCK), BF16)


def _inproj_small(x2d, n1, w_a, gmat, qw, kw):
    n, d = x2d.shape
    full = lambda width: _const_spec((n, width))
    return pl.pallas_call(
        _inproj_small_kernel,
        grid=(1,),
        in_specs=[full(d), _const_spec((1, d)), _const_spec(w_a.shape), _const_spec(gmat.shape),
                  _const_spec((1, 512)), _const_spec((1, 512))],
        out_specs=[full(2048), full(512), full(512), full(512), full(512), full(512)],
        out_shape=[jax.ShapeDtypeStruct((n, 2048), F32), jax.ShapeDtypeStruct((n, 512), BF16),
                   jax.ShapeDtypeStruct((n, 512), F32), jax.ShapeDtypeStruct((n, 512), BF16),
                   jax.ShapeDtypeStruct((n, 512), F32), jax.ShapeDtypeStruct((n, 512), BF16)],
        compiler_params=pltpu.CompilerParams(dimension_semantics=("arbitrary",),
                                             vmem_limit_bytes=VMEM_LIMIT),
        name="inproj_small",
    )(x2d, n1, w_a, gmat, qw, kw)


def _inproj_prompt(x2d, n1, w_a, gmat, qw, kw, kbase, vbase, seq, tm):
    n, d = x2d.shape
    assert seq % tm == 0 and tm % ATT_BLOCK == 0
    per = tm // ATT_BLOCK
    tiles = seq // tm
    row = lambda width: pl.BlockSpec((tm, width), lambda i: (i, 0))
    t_spec = lambda rows: pl.BlockSpec((per, rows, ATT_BLOCK), lambda i: (i, 0, 0))
    t_shape = lambda rows: jax.ShapeDtypeStruct((n // ATT_BLOCK, rows, ATT_BLOCK), BF16)

    def cache_rows(i):
        tok = (i // tiles) * (N_META + seq) + N_META + (i % tiles) * tm
        return pl.multiple_of(tok * HEADS, 8 * HEADS), 0

    cache_spec = pl.BlockSpec((pl.Element(tm * HEADS), pl.Element(HEAD_W)), cache_rows)
    any_spec = pl.BlockSpec(memory_space=pl.ANY)
    return pl.pallas_call(
        _inproj_prompt_kernel,
        grid=(n // tm,),
        in_specs=[row(d), _const_spec((1, d)), _const_spec(w_a.shape), _const_spec(gmat.shape),
                  _const_spec((1, 512)), _const_spec((1, 512)), any_spec, any_spec],
        out_specs=[row(2048), t_spec(512), cache_spec, row(512), cache_spec, t_spec(VT_ROWS)],
        out_shape=[jax.ShapeDtypeStruct((n, 2048), F32), t_shape(512),
                   jax.ShapeDtypeStruct(kbase.shape, F32), jax.ShapeDtypeStruct((n, 512), BF16),
                   jax.ShapeDtypeStruct(vbase.shape, F32), t_shape(VT_ROWS)],
        input_output_aliases={6: 2, 7: 4},
        compiler_params=pltpu.CompilerParams(dimension_semantics=("parallel",),
                                             vmem_limit_bytes=VMEM_LIMIT),
        name="inproj",
    )(x2d, n1, w_a, gmat, qw, kw, kbase, vbase)


def _hgrn_kernel(z_ref, lbl_ref, onw_ref, s0_ref, o_ref, sout_ref, st_scr, *, rows):
    C = HG_CHUNK
    W = HEADS * HEAD_W
    n_sub = max(rows // C, 1)
    n_valid = min(rows, C)
    c = pl.program_id(1)
    heads = [slice(h * HEAD_W, (h + 1) * HEAD_W) for h in range(HEADS)]

    @pl.when(c == 0)
    def _():
        for h in range(HEADS):
            st_scr[h] = s0_ref[0, h].T

    lg = lbl_ref[...]
    e = jnp.exp(lg - jnp.max(lg, axis=0, keepdims=True))
    lb = e[0:1] / jnp.sum(e, axis=0, keepdims=True)
    half = 0.5 * (1.0 - lb)
    onw = onw_ref[...]
    row = lax.broadcasted_iota(jnp.int32, (C, C), 0)
    col = lax.broadcasted_iota(jnp.int32, (C, C), 1)
    tri = jnp.where(col <= row, 1.0, 0.0).astype(BF16)

    def shr(x, width):
        return lax.shift_right_logical(x, int(math.log2(width)))

    def front(z, n_valid):
        hq, hf, hi, hg = z[:, 0:W], z[:, W:2 * W], z[:, 2 * W:3 * W], z[:, 3 * W:4 * W]
        q = _silu(hq)
        f = (lb + half) + half * jnp.tanh(0.5 * hf)
        k = 1.0 - f
        logf = jnp.log2(f)
        if n_valid < C:
            valid = lax.broadcasted_iota(jnp.int32, (C, W), 0) < n_valid
            k = jnp.where(valid, k, 0.0)
            logf = jnp.where(valid, logf, 0.0)
        x1 = logf.astype(BF16)
        r1 = logf - x1.astype(F32)
        x2 = r1.astype(BF16)
        x3 = (r1 - x2.astype(F32)).astype(BF16)
        b = _dot(tri, x1) + _dot(tri, x2) + _dot(tri, x3)
        return q, k, hi.astype(BF16), b, _silu(hg)

    def intra(q, k, b):
        b_last = b[C - 1:C]

        def block_row(width, r):
            b3 = b.reshape(C // width, width, W)
            return jnp.broadcast_to(b3[:, r:r + 1, :], b3.shape).reshape(C, W)

        def next_block_first(width):
            b3 = b.reshape(C // width, width, W)
            nxt = jnp.concatenate([b3[1:, 0:1, :], b_last[None]], axis=0)
            return jnp.broadcast_to(nxt, b3.shape).reshape(C, W)

        mid = block_row(HG_BASE, HG_BASE // 2)
        qe = (q * jnp.exp2(b - mid)).astype(BF16)
        ke = (k * jnp.exp2(mid - b)).astype(BF16)
        mask = (shr(row, HG_BASE) == shr(col, HG_BASE)) & (col <= row)
        scores = [jnp.where(mask, _dot_nt(qe[:, hs], ke[:, hs]), 0.0) for hs in heads]
        width = HG_BASE
        while width < n_valid:
            qe = (q * jnp.exp2(b - block_row(width, 0))).astype(BF16)
            ke = (k * jnp.exp2(next_block_first(width) - b)).astype(BF16)
            mask = ((shr(row, width) == shr(col, width) + 1)
                    & (shr(row, 2 * width) == shr(col, 2 * width)))
            scores = [jnp.where(mask, _dot_nt(qe[:, hs], ke[:, hs]), sc)
                      for hs, sc in zip(heads, scores)]
            width *= 2
        qb = (q * jnp.exp2(b)).astype(BF16)
        kd = (k * jnp.exp2(b_last - b)).astype(BF16)
        return [sc.astype(BF16) for sc in scores], qb, kd, jnp.exp2(b_last)

    fronts = []
    for u in range(n_sub):
        z = z_ref[0, u * n_valid:(u + 1) * n_valid, :]
        if n_valid < C:
            z = jnp.concatenate([z, jnp.zeros((C - n_valid, z.shape[1]), F32)], axis=0)
        fronts.append(front(z, n_valid))
    intras = [intra(q, k, b) for q, k, _, b, _ in fronts]
    for u in range(n_sub):
        _, _, v_bf, _, gate = fronts[u]
        scores, qb, kd, decay = intras[u]
        outs = []
        for h, hs in enumerate(heads):
            st = st_scr[h]
            outs.append(_dot_nt(qb[:, hs], st.astype(BF16)) + _dot(scores[h], v_bf[:, hs]))
            st_scr[h] = st * decay[:, hs] + _dot_tn(v_bf[:, hs], kd[:, hs])
        for h, hs in enumerate(heads):
            on = _rms_rows(outs[h], onw[:, hs]) * gate[:, hs]
            o_ref[0, u * n_valid:(u + 1) * n_valid, hs] = on[:n_valid].astype(o_ref.dtype)

    @pl.when(c == pl.num_programs(1) - 1)
    def _():
        for h in range(HEADS):
            sout_ref[0, h] = st_scr[h].T


def _hgrn(zh, lbl, onw, s0, *, shared_state, out_dtype):
    nseq, t, _ = zh.shape
    rows = min(t, HG_STEP_CHUNKS * HG_CHUNK)
    assert t % rows == 0 and rows % 8 == 0 and (rows <= HG_CHUNK or rows % HG_CHUNK == 0)
    s_map = (lambda b, c: (0, 0, 0, 0)) if shared_state else (lambda b, c: (b, 0, 0, 0))
    return pl.pallas_call(
        functools.partial(_hgrn_kernel, rows=rows),
        grid=(nseq, t // rows),
        in_specs=[pl.BlockSpec((1, rows, 2048), lambda b, c: (b, c, 0)),
                  _const_spec(lbl.shape), _const_spec((1, 512)),
                  pl.BlockSpec((1, HEADS, HEAD_W, HEAD_W), s_map)],
        out_specs=[pl.BlockSpec((1, rows, 512), lambda b, c: (b, c, 0)),
                   pl.BlockSpec((1, HEADS, HEAD_W, HEAD_W), lambda b, c: (b, 0, 0, 0))],
        out_shape=[jax.ShapeDtypeStruct((nseq, t, 512), out_dtype),
                   jax.ShapeDtypeStruct((nseq, HEADS, HEAD_W, HEAD_W), F32)],
        scratch_shapes=[pltpu.VMEM((HEADS, HEAD_W, HEAD_W), F32)],
        compiler_params=pltpu.CompilerParams(dimension_semantics=("parallel", "arbitrary"),
                                             vmem_limit_bytes=VMEM_LIMIT),
        name="hgrn",
    )(zh, lbl, onw, s0)


def _lambda_value(lp, lam_init):
    s1 = jnp.sum(lp[0:1] * lp[1:2], axis=1, keepdims=True)
    s2 = jnp.sum(lp[2:3] * lp[3:4], axis=1, keepdims=True)
    return jnp.exp(s1) - jnp.exp(s2) + lam_init


def _softmax_update(s, v_bf, m_ref, l_ref, acc_ref, first):
    m_cur = jnp.max(s, axis=1, keepdims=True)
    if first:
        m_new = m_cur
        p = jnp.exp2(s - m_new)
        l_ref[...] = jnp.sum(p, axis=1, keepdims=True)
        acc_ref[...] = _dot(p.astype(BF16), v_bf)
    else:
        m_prev = m_ref[...]
        m_new = jnp.maximum(m_prev, m_cur)
        alpha = jnp.exp2(m_prev - m_new)
        p = jnp.exp2(s - m_new)
        l_ref[...] = alpha * l_ref[...] + jnp.sum(p, axis=1, keepdims=True)
        acc_ref[...] = alpha * acc_ref[...] + _dot(p.astype(BF16), v_bf)
    m_ref[...] = m_new


def _pattn_kernel(qt_ref, k_ref, vt_ref, km_ref, vmt_ref, tb_ref, tm_ref, lam_ref, onw_ref, o_ref,
                  qz_scr, m_scr, acc_scr, *, lam_init):
    blk = ATT_BLOCK
    i = pl.program_id(1)
    sub = lax.broadcasted_iota(jnp.int32, (HEAD_W, blk), 0)
    for h in range(HEADS):
        qh = qt_ref[0, h * HEAD_W:(h + 1) * HEAD_W, :]
        zero = jnp.zeros_like(qh)
        qz_scr[h] = jnp.concatenate([jnp.where(sub < DA_DH, qh, zero),
                                     jnp.where(sub >= DA_DH, qh, zero)], axis=1)

    hv = lambda h: slice(h * (HEAD_W + VT_PAD), (h + 1) * (HEAD_W + VT_PAD))

    meta_sel = jnp.minimum(i, 1)
    META = "meta"

    def key_blocks(blocks):
        def qk(j, table, h):
            hs = slice(h * HEAD_W, (h + 1) * HEAD_W)
            if j is META:
                return _dot(km_ref[:, hs], qz_scr[h]) + tm_ref[meta_sel, h]
            off = pl.multiple_of(j * blk, blk)
            s = _dot(k_ref[0, pl.ds(off, blk), hs], qz_scr[h])
            return s if table is None else s + tb_ref[table, h]

        def soft(j, h, s):
            m_cur = jnp.max(s, axis=0, keepdims=True)
            if j is META:
                m_scr[h] = m_cur
                return jnp.exp2(s - m_cur).astype(BF16), None
            m_prev = m_scr[h]
            m_new = jnp.maximum(m_prev, m_cur)
            m_scr[h] = m_new
            return jnp.exp2(s - m_new).astype(BF16), jnp.exp2(m_prev - m_new)

        def pv(j, h, p, alpha):
            if j is META:
                acc_scr[h] = _dot(vmt_ref[hv(h), :], p)
            else:
                acc_scr[h] = alpha * acc_scr[h] + _dot(vt_ref[j, hv(h), :], p)

        stages = [(j, table, h) for j, table in blocks for h in range(HEADS)]
        s_cur = qk(*stages[0])
        pend = None
        for n, (j, table, h) in enumerate(stages):
            s_next = qk(*stages[n + 1]) if n + 1 < len(stages) else None
            if pend is not None:
                pv(stages[n - 1][0], stages[n - 1][2], *pend)
            pend = soft(j, h, s_cur)
            s_cur = s_next
        pv(stages[-1][0], stages[-1][2], *pend)

    @pl.when(i >= 1)
    def _():
        key_blocks([(META, None), (i - 1, 1), (i, 0)])

    @pl.when(i == 0)
    def _():
        key_blocks([(META, None), (0, 0)])

    n_far = jnp.maximum(i - 1, 0)
    done = 0
    width = 1
    while width < FAR_UNROLL:
        start = done

        @pl.when((n_far & width) != 0)
        def _(start=start, width=width):
            key_blocks([(start + u, None) for u in range(width)])

        done = done + (n_far & width)
        width *= 2
    n_rem = done

    def far_body(t, carry):
        j = n_rem + FAR_UNROLL * t
        key_blocks([(j + u, None) for u in range(FAR_UNROLL)])
        return carry

    lax.fori_loop(0, n_far // FAR_UNROLL, far_body, 0)

    lam = _lambda_value(lam_ref[...], lam_init)
    onw = onw_ref[...]
    for h in range(HEADS):
        acc = acc_scr[h]
        a = acc[:HEAD_W] / acc[HEAD_W:HEAD_W + 1]
        ot = a[:, :blk] - lam * a[:, blk:]
        ot = ot * lax.rsqrt(jnp.mean(ot * ot, axis=0, keepdims=True) + EPS) * onw * (1.0 - lam_init)
        o_ref[0, :, h * HEAD_W:(h + 1) * HEAD_W] = ot.T.astype(o_ref.dtype)


def _pattn(qt, k, vt, km, vmt, tb, tmeta, lam_p, onw_col, lam_init):
    bsz, t, w = k.shape
    blk = ATT_BLOCK
    nblk = t // blk
    assert t % blk == 0
    return pl.pallas_call(
        functools.partial(_pattn_kernel, lam_init=lam_init),
        grid=(bsz, nblk),
        in_specs=[pl.BlockSpec((1, w, blk), lambda b, i: (b * nblk + i, 0, 0)),
                  pl.BlockSpec((1, t, w), lambda b, i: (b, 0, 0)),
                  pl.BlockSpec((nblk, VT_ROWS, blk), lambda b, i: (b, 0, 0)),
                  _const_spec(km.shape), _const_spec(vmt.shape), _const_spec(tb.shape),
                  _const_spec(tmeta.shape), _const_spec(lam_p.shape), _const_spec(onw_col.shape)],
        out_specs=pl.BlockSpec((1, blk, w), lambda b, i: (b, i, 0)),
        out_shape=jax.ShapeDtypeStruct((bsz, t, w), BF16),
        scratch_shapes=[pltpu.VMEM((HEADS, HEAD_W, 2 * blk), BF16),
                        pltpu.VMEM((HEADS, 1, 2 * blk), F32),
                        pltpu.VMEM((HEADS, HEAD_W + VT_PAD, 2 * blk), F32)],
        compiler_params=pltpu.CompilerParams(dimension_semantics=("parallel", "arbitrary"),
                                             vmem_limit_bytes=VMEM_LIMIT),
        name="prompt_attn",
    )(qt, k, vt, km, vmt, tb, tmeta, lam_p, onw_col)


def _sattn_kernel(pt_ref, q_ref, kn_ref, vn_ref, tl_ref, tn_ref, tmask_ref, lam_ref, onw_ref, *rest,
                  lam_init, n_tok):
    del pt_ref
    pg = PAGES_PER_STEP
    k_refs, v_refs = rest[:pg], rest[pg:2 * pg]
    o_ref, m_scr, l_scr, acc_scr = rest[2 * pg:]
    g = pl.program_id(1)
    q = q_ref[0]

    @pl.when(g == 0)
    def _():
        s = _dot_nt(q, kn_ref[0]) + tn_ref[...]
        _softmax_update(s, vn_ref[0], m_scr, l_scr, acc_scr, True)

    is_last = g == pl.num_programs(1) - 1
    tmask = tmask_ref[...]
    ss = []
    for u in range(pg):
        s = _dot_nt(q, k_refs[u][...].astype(BF16))
        ss.append(s + (jnp.where(is_last, tl_ref[...], tmask) if u == pg - 1 else tmask))
    m_prev = m_scr[...]
    m_new = m_prev
    for s in ss:
        m_new = jnp.maximum(m_new, jnp.max(s, axis=1, keepdims=True))
    alpha = jnp.exp2(m_prev - m_new)
    l_new = alpha * l_scr[...]
    acc = alpha * acc_scr[...]
    for u in range(pg):
        p = jnp.exp2(ss[u] - m_new)
        l_new = l_new + jnp.sum(p, axis=1, keepdims=True)
        acc = acc + _dot(p.astype(BF16), v_refs[u][...].astype(BF16))
    m_scr[...] = m_new
    l_scr[...] = l_new
    acc_scr[...] = acc

    @pl.when(is_last)
    def _():
        lam = _lambda_value(lam_ref[...], lam_init)
        onw = onw_ref[...]
        a = acc_scr[...] / l_scr[...]
        for h in range(HEADS):
            r1 = (2 * h) * n_tok
            r2 = (2 * h + 1) * n_tok
            o = a[r1:r1 + n_tok] - lam * a[r2:r2 + n_tok]
            o_ref[0, :, h * HEAD_W:(h + 1) * HEAD_W] = (
                _rms_rows(o, onw) * (1.0 - lam_init)).astype(o_ref.dtype)


def _sattn(page_table, qall, knew, vnew, tlast, tnew, tmask, lam_p, onw, ck2, cv2, lam_init, n_tok):
    nb, n_pages = page_table.shape
    pg = PAGES_PER_STEP
    assert n_pages % pg == 0
    nrow = qall.shape[1]
    prow = knew.shape[1]
    pt_flat = page_table.reshape(-1)

    def page_spec(u):
        return pl.BlockSpec((prow, HEAD_W), lambda b, g, pt: (pt[b * n_pages + g * pg + u], 0))

    per_b = lambda shape: pl.BlockSpec((1,) + shape, lambda b, g, pt: (b, 0, 0))
    const = lambda shape: pl.BlockSpec(shape, lambda b, g, pt: (0,) * len(shape))
    grid_spec = pltpu.PrefetchScalarGridSpec(
        num_scalar_prefetch=1,
        grid=(nb, n_pages // pg),
        in_specs=[per_b((nrow, HEAD_W)), per_b((prow, HEAD_W)), per_b((prow, HEAD_W)),
                  const(tlast.shape), const(tnew.shape), const(tmask.shape),
                  const(lam_p.shape), const(onw.shape)]
                 + [page_spec(u) for u in range(pg)] + [page_spec(u) for u in range(pg)],
        out_specs=per_b((n_tok, HEADS * HEAD_W)),
        scratch_shapes=[pltpu.VMEM((nrow, 1), F32), pltpu.VMEM((nrow, 1), F32),
                        pltpu.VMEM((nrow, HEAD_W), F32)],
    )
    return pl.pallas_call(
        functools.partial(_sattn_kernel, lam_init=lam_init, n_tok=n_tok),
        grid_spec=grid_spec,
        out_shape=jax.ShapeDtypeStruct((nb, n_tok, HEADS * HEAD_W), F32),
        compiler_params=pltpu.CompilerParams(dimension_semantics=("parallel", "arbitrary"),
                                             vmem_limit_bytes=VMEM_LIMIT),
        name="sample_attn",
    )(pt_flat, qall, knew, vnew, tlast, tnew, tmask, lam_p, onw, *([ck2] * pg), *([cv2] * pg))


def _merge_kernel(x_ref, oa_ref, ob_ref, n1_ref, wg_ref, wha_ref, wda_ref, wo_ref, h_ref):
    x = x_ref[...]
    d = x.shape[1]
    xn = _rms_rows(x, n1_ref[...]).astype(BF16)
    gates = _dot(xn, wg_ref[...])
    ya = _dot(oa_ref[...], wha_ref[...])
    yb = _dot(ob_ref[...], wda_ref[...])
    merged = _sigmoid(gates[:, :d]) * ya + _sigmoid(gates[:, d:]) * yb
    h_ref[...] = x + _dot(merged.astype(BF16), wo_ref[...])


def _merge(x2d, oa, ob, n1, w_g, w_ha, w_da, w_o, tm):
    n, d = x2d.shape
    row = lambda width: pl.BlockSpec((tm, width), lambda i: (i, 0))
    return pl.pallas_call(
        _merge_kernel,
        grid=(n // tm,),
        in_specs=[row(d), row(512), row(512), _const_spec((1, d)), _const_spec(w_g.shape),
                  _const_spec(w_ha.shape), _const_spec(w_da.shape), _const_spec(w_o.shape)],
        out_specs=row(d),
        out_shape=jax.ShapeDtypeStruct((n, d), F32),
        compiler_params=pltpu.CompilerParams(dimension_semantics=("parallel",),
                                             vmem_limit_bytes=VMEM_LIMIT),
        name="merge",
    )(x2d, oa, ob, n1, w_g, w_ha, w_da, w_o)


def _moe_kernel(h_ref, n2_ref, wr_hi_ref, wr_lo_ref, br_ref, wgu_ref, wd_ref, y_ref):
    h = h_ref[...]
    tm = h.shape[0]
    hn = _rms_rows(h, n2_ref[...])
    hn_hi = hn.astype(BF16)
    hn_lo = (hn - hn_hi.astype(F32)).astype(BF16)
    logits = (_dot(hn_hi, wr_hi_ref[...]) + _dot(hn_lo, wr_hi_ref[...])
              + _dot(hn_hi, wr_lo_ref[...]) + br_ref[...])
    lane = lax.broadcasted_iota(jnp.int32, logits.shape, 1)
    big = jnp.int32(1 << 20)
    ninf = -jnp.inf

    def first_lane(mask):
        return jnp.min(jnp.where(mask, lane, big), axis=1, keepdims=True)

    is_g = lane < N_GROUPS
    g_max = jnp.max(jnp.where(is_g, logits, ninf), axis=1, keepdims=True)
    g_idx = first_lane(is_g & (logits == g_max))
    g_sum = jnp.sum(jnp.where(is_g, jnp.exp(logits - g_max), 0.0), axis=1, keepdims=True)
    p_g = 1.0 / g_sum
    e_lane = lane - N_GROUPS
    in_group = ((e_lane >= 0) & (e_lane < N_EXPERTS)
                & (lax.shift_right_arithmetic(e_lane, 2) == g_idx))
    e_max = jnp.max(jnp.where(in_group, logits, ninf), axis=1, keepdims=True)
    e_exp = jnp.where(in_group, jnp.exp(logits - e_max), 0.0)
    e_prob = e_exp / jnp.sum(e_exp, axis=1, keepdims=True)
    p1 = jnp.max(jnp.where(in_group, e_prob, -1.0), axis=1, keepdims=True)
    i1 = first_lane(in_group & (e_prob == p1))
    rest = in_group & (lane != i1)
    p2 = jnp.max(jnp.where(rest, e_prob, -1.0), axis=1, keepdims=True)
    i2 = first_lane(rest & (e_prob == p2))
    denom = p1 + p2
    combine = (jnp.where(lane == i1, p_g * p1 / denom, 0.0)
               + jnp.where(lane == i2, p_g * p2 / denom, 0.0))

    ff = wd_ref.shape[1]
    acc = jnp.zeros_like(h)
    for ex in range(N_EXPERTS):
        gu = _dot(hn_hi, wgu_ref[ex])
        gt, up = gu[:, :ff], gu[:, ff:]
        hh = _silu(gt) * up * combine[:, N_GROUPS + ex:N_GROUPS + ex + 1]
        acc = acc + _dot(hh.astype(BF16), wd_ref[ex])
    y_ref[...] = h + acc


def _moe(h2d, n2, wr_hi, wr_lo, br, wgu, wd, tm):
    n, d = h2d.shape
    row = pl.BlockSpec((tm, d), lambda i: (i, 0))
    single = lambda shape: pl.BlockSpec(shape, lambda i: (0,) * len(shape),
                                        pipeline_mode=pl.Buffered(1))
    return pl.pallas_call(
        _moe_kernel,
        grid=(n // tm,),
        in_specs=[row, _const_spec((1, d)), _const_spec(wr_hi.shape), _const_spec(wr_lo.shape),
                  _const_spec(br.shape), single(wgu.shape), single(wd.shape)],
        out_specs=row,
        out_shape=jax.ShapeDtypeStruct((n, d), F32),
        compiler_params=pltpu.CompilerParams(dimension_semantics=("parallel",),
                                             vmem_limit_bytes=VMEM_LIMIT),
        name="moe",
    )(h2d, n2, wr_hi, wr_lo, br, wgu, wd)


def _t5_bias(dist, rel_bias):
    n = jnp.maximum(dist, 0)
    max_exact = N_BUCKETS // 2
    nf = jnp.maximum(n, 1).astype(F32)
    scaled = (jnp.log(nf / max_exact) / math.log(MAX_DISTANCE / max_exact)
              * (N_BUCKETS - max_exact))
    large = max_exact + jnp.floor(jnp.maximum(scaled, 0.0)).astype(jnp.int32)
    large = jnp.minimum(large, N_BUCKETS - 1)
    bucket = jnp.where(n < max_exact, n, large)
    rb = rel_bias.astype(F32)
    onehot = (bucket[..., None] == jnp.arange(N_BUCKETS)).astype(F32)
    bias = jnp.moveaxis(jnp.dot(onehot, rb, precision=lax.Precision.HIGHEST), -1, 0)
    far = rb[N_BUCKETS - 1].reshape((HEADS,) + (1,) * dist.ndim)
    return jnp.where(dist >= 0, (bias - far) * LOG2E, MASK_VALUE)


def kernel(x_prompt, x_sample, cache_k, cache_v, state_hgrn, page_table, meta_tokens, rel_bias,
           hg_lb_logits, norm1_w, w_in, hg_onorm_w, w_hg_out, q_norm_w, k_norm_w, da_lambda,
           da_onorm_w, w_da_out, w_o, norm2_w, w_router_group, b_router_group,
           w_router_expert, b_router_expert, w_gate, w_up, w_down):
    bsz, seq, d = x_prompt.shape
    nb, n_tok, _ = x_sample.shape
    depth = w_in.shape[0]
    assert depth == 1 and hg_lb_logits.shape[0] == 2
    n_pages = page_table.shape[1]
    psize = cache_k.shape[2]
    past_len = n_pages * psize
    blk = ATT_BLOCK
    assert blk >= MAX_DISTANCE and psize >= MAX_DISTANCE and n_tok <= 8
    lam_init = 0.8 - 0.6 * math.exp(-0.3 * 0)
    w_attn = HEADS * HEAD_W

    w_in0 = w_in[0]
    w_a = w_in0[:, :3584].astype(BF16)
    w_g = w_in0[:, 3584:].astype(BF16)
    n1 = norm1_w[0].reshape(1, d)
    n2 = norm2_w[0].reshape(1, d)
    qw = (jnp.tile(q_norm_w[0], 2 * HEADS) * (DA_DH ** -0.5 * LOG2E)).reshape(1, w_attn)
    kw = jnp.tile(k_norm_w[0], 2 * HEADS).reshape(1, w_attn)
    gi = jnp.arange(w_attn) // DA_DH
    gmat = jnp.where(gi[:, None] == gi[None, :], 1.0 / DA_DH, 0.0).astype(BF16)
    hg_onw = jnp.tile(hg_onorm_w[0], HEADS).reshape(1, w_attn)
    da_onw = da_onorm_w[0].reshape(1, HEAD_W)
    lam_p = da_lambda[0].astype(F32)
    w_ha = w_hg_out[0].astype(BF16)
    w_da = w_da_out[0].astype(BF16)
    w_o0 = w_o[0].astype(BF16)
    wr = jnp.concatenate([w_router_group[0], w_router_expert[0],
                          jnp.zeros((d, 128 - N_GROUPS - N_EXPERTS), F32)], axis=1)
    wr_hi = wr.astype(BF16)
    wr_lo = (wr - wr_hi.astype(F32)).astype(BF16)
    br = jnp.concatenate([b_router_group[0], b_router_expert[0],
                          jnp.zeros((128 - N_GROUPS - N_EXPERTS,), F32)]).reshape(1, 128)
    wgu = jnp.concatenate([w_gate[0], w_up[0]], axis=2).astype(BF16)
    wd = w_down[0].astype(BF16)

    n_small = N_META + nb * n_tok
    x_small = jnp.concatenate([meta_tokens.astype(F32), x_sample.reshape(nb * n_tok, d)], axis=0)
    zh_s, q_s, kf_s, kb_s, vf_s, vb_s = _inproj_small(x_small, n1, w_a, gmat, qw, kw)

    _, s_meta = _hgrn(zh_s[:N_META][None], hg_lb_logits, hg_onw,
                      jnp.zeros((1, HEADS, HEAD_W, HEAD_W), F32), shared_state=False, out_dtype=F32)
    oa_s, s_sample = _hgrn(zh_s[N_META:].reshape(nb, n_tok, 2048), hg_lb_logits, hg_onw,
                           state_hgrn[0], shared_state=False, out_dtype=F32)

    x_p = x_prompt.reshape(bsz * seq, d)
    def cache_base(meta_rows):
        meta_b = jnp.broadcast_to(meta_rows.reshape(1, N_META * HEADS, HEAD_W),
                                  (bsz, N_META * HEADS, HEAD_W))
        return jnp.concatenate([meta_b, jnp.zeros((bsz, seq * HEADS, HEAD_W), F32)],
                               axis=1).reshape(-1, HEAD_W)

    zh_p, qt_p, kf_p, kb_p, vf_p, vt_p = _inproj_prompt(
        x_p, n1, w_a, gmat, qw, kw, cache_base(kf_s[:N_META]), cache_base(vf_s[:N_META]), seq, 512)
    oa_p, s_prompt = _hgrn(zh_p.reshape(bsz, seq, 2048), hg_lb_logits, hg_onw, s_meta,
                           shared_state=True, out_dtype=BF16)

    r = jnp.arange(blk)
    tb = jnp.stack([_t5_bias(r[None, :] - r[:, None], rel_bias),
                    _t5_bias(blk + r[None, :] - r[:, None], rel_bias)])
    tb = jnp.concatenate([tb, tb], axis=3)
    mrow = jnp.arange(128)
    t_meta0 = _t5_bias(N_META + r[None, :] - mrow[:, None], rel_bias)
    t_meta = jnp.stack([t_meta0, jnp.zeros_like(t_meta0)])
    t_meta = jnp.where(mrow[:, None] < N_META, t_meta, MASK_VALUE)
    t_meta = jnp.concatenate([t_meta, t_meta], axis=3)
    pad_meta = lambda a: jnp.concatenate([a, jnp.zeros((128 - N_META, w_attn), a.dtype)], axis=0)
    vmt = pad_meta(vb_s[:N_META]).T.reshape(HEADS, HEAD_W, 128)
    vmt = jnp.concatenate([vmt, jnp.ones((HEADS, VT_PAD, 128), BF16)], axis=1).reshape(VT_ROWS, 128)
    ob_p = _pattn(qt_p, kb_p.reshape(bsz, seq, w_attn), vt_p, pad_meta(kb_s[:N_META]),
                  vmt, tb, t_meta, lam_p, da_onw.reshape(HEAD_W, 1), lam_init)

    q_tok = q_s[N_META:].reshape(nb, n_tok, HEADS, 1, 2, DA_DH)
    q_maps = q_tok * jnp.eye(2, dtype=BF16).reshape(1, 1, 1, 2, 2, 1)
    n_rows = 2 * HEADS * n_tok
    qall = jnp.transpose(q_maps, (0, 2, 3, 1, 4, 5)).reshape(nb, n_rows, HEAD_W)
    prow = psize * HEADS
    pad_new = lambda a: jnp.concatenate(
        [a.reshape(nb, n_tok * HEADS, HEAD_W),
         jnp.zeros((nb, prow - n_tok * HEADS, HEAD_W), a.dtype)], axis=1)
    s_idx = jnp.tile(jnp.arange(n_tok), 2 * HEADS)
    head_of_row = jnp.repeat(jnp.arange(HEADS), 2 * n_tok)
    key_r = jnp.arange(prow) // HEADS
    own_head = head_of_row[:, None] == (jnp.arange(prow) % HEADS)[None, :]
    dist_last = (past_len + s_idx[:, None]) - (past_len - psize + key_r[None, :])
    dist_new = jnp.where(key_r[None, :] < n_tok, s_idx[:, None] - key_r[None, :], -1)
    pick = lambda t: jnp.where(own_head, t[head_of_row, jnp.arange(n_rows)], MASK_VALUE)
    t_last = pick(_t5_bias(dist_last, rel_bias))
    t_new = pick(_t5_bias(dist_new, rel_bias))
    t_mask = jnp.where(own_head, 0.0, MASK_VALUE).astype(F32)
    ob_s = _sattn(page_table, qall, pad_new(kb_s[N_META:]), pad_new(vb_s[N_META:]), t_last, t_new,
                  t_mask, lam_p, da_onw, cache_k.reshape(-1, HEAD_W), cache_v.reshape(-1, HEAD_W),
                  lam_init, n_tok)

    h_p = _merge(x_p, oa_p.reshape(bsz * seq, w_attn), ob_p.reshape(bsz * seq, w_attn),
                 n1, w_g, w_ha, w_da, w_o0, 512)
    y_p = _moe(h_p, n2, wr_hi, wr_lo, br, wgu, wd, 512)
    x_s = x_small[N_META:]
    h_s = _merge(x_s, oa_s.reshape(nb * n_tok, w_attn).astype(BF16),
                 ob_s.reshape(nb * n_tok, w_attn).astype(BF16), n1, w_g, w_ha, w_da, w_o0,
                 nb * n_tok)
    y_s = _moe(h_s, n2, wr_hi, wr_lo, br, wgu, wd, nb * n_tok)

    return (y_p.reshape(bsz, seq, d),
            y_s.reshape(nb, n_tok, d),
            kf_p.reshape(1, bsz, seq + N_META, HEADS, HEAD_W),
            vf_p.reshape(1, bsz, seq + N_META, HEADS, HEAD_W),
            s_prompt[None],
            kf_s[N_META:].reshape(1, nb, n_tok, HEADS, HEAD_W),
            vf_s[N_META:].reshape(1, nb, n_tok, HEADS, HEAD_W),
            s_sample[None])
```

```python
import functools
import math

import jax
import jax.numpy as jnp
from jax import lax
from jax.experimental import pallas as pl
from jax.experimental.pallas import tpu as pltpu

F32 = jnp.float32
BF16 = jnp.bfloat16

EPS = 1e-6
N_META = 16
HEADS = 4
HEAD_W = 128
DA_DH = 64
N_BUCKETS = 32
MAX_DISTANCE = 128
N_GROUPS = 4
EXPERTS_PER_GROUP = 4
N_EXPERTS = N_GROUPS * EXPERTS_PER_GROUP
MASK_VALUE = -1e30
HG_CHUNK = 128
HG_STEP_CHUNKS = 4
HG_BASE = 8
ATT_BLOCK = 256
VT_PAD = 16
VT_ROWS = HEADS * (HEAD_W + VT_PAD)
FAR_UNROLL = 8
PAGES_PER_STEP = 16
LOG2E = math.log2(math.e)
VMEM_LIMIT = 56 * 1024 * 1024


def _dot(a, b):
    return jnp.dot(a, b, preferred_element_type=F32)


def _dot_nt(a, b):
    return lax.dot_general(a, b, (((1,), (1,)), ((), ())), preferred_element_type=F32)


def _dot_tn(a, b):
    return lax.dot_general(a, b, (((0,), (0,)), ((), ())), preferred_element_type=F32)


def _sigmoid(x):
    return 0.5 + 0.5 * jnp.tanh(0.5 * x)


def _silu(x):
    y = 0.5 * x
    return y + y * jnp.tanh(y)


def _rms_rows(x, w):
    return x * lax.rsqrt(jnp.mean(x * x, axis=-1, keepdims=True) + EPS) * w


def _const_spec(shape):
    zeros = (0,) * len(shape)
    return pl.BlockSpec(shape, lambda *_: zeros)


def _inproj_values(x_ref, n1_ref, w_ref, g_ref, qw_ref, kw_ref):
    xn = _rms_rows(x_ref[...], n1_ref[...]).astype(BF16)
    g = g_ref[...]

    def group_norm(z, w):
        sq = z * z
        hi = sq.astype(BF16)
        lo = (sq - hi.astype(F32)).astype(BF16)
        ms = _dot(hi, g) + _dot(lo, g)
        return z * lax.rsqrt(ms + EPS) * w

    zh = _dot(xn, w_ref[:, 0:2048])
    qn = group_norm(_dot(xn, w_ref[:, 2048:2560]), qw_ref[...])
    kn = group_norm(_dot(xn, w_ref[:, 2560:3072]), kw_ref[...])
    zv = _dot(xn, w_ref[:, 3072:3584])
    return zh, qn, kn, zv


def _inproj_small_kernel(x_ref, n1_ref, w_ref, g_ref, qw_ref, kw_ref,
                         zh_ref, q_ref, kf_ref, kb_ref, vf_ref, vb_ref):
    zh, qn, kn, zv = _inproj_values(x_ref, n1_ref, w_ref, g_ref, qw_ref, kw_ref)
    zh_ref[...] = zh
    q_ref[...] = qn.astype(BF16)
    kf_ref[...] = kn
    kb_ref[...] = kn.astype(BF16)
    vf_ref[...] = zv
    vb_ref[...] = zv.astype(BF16)


def _inproj_prompt_kernel(x_ref, n1_ref, w_ref, g_ref, qw_ref, kw_ref, kbase_ref, vbase_ref,
                          zh_ref, qt_ref, kf_ref, kb_ref, vf_ref, vt_ref):
    del kbase_ref, vbase_ref
    zh, qn, kn, zv = _inproj_values(x_ref, n1_ref, w_ref, g_ref, qw_ref, kw_ref)
    tm = zh.shape[0]
    zh_ref[...] = zh
    kb_ref[...] = kn.astype(BF16)
    for h in range(HEADS):
        hs = slice(h * HEAD_W, (h + 1) * HEAD_W)
        kf_ref[pl.ds(h, tm, stride=HEADS), :] = kn[:, hs]
        vf_ref[pl.ds(h, tm, stride=HEADS), :] = zv[:, hs]
    qt = qn.T
    vt = zv.T
    for c in range(tm // ATT_BLOCK):
        cols = slice(c * ATT_BLOCK, (c + 1) * ATT_BLOCK)
        qt_ref[c] = qt[:, cols].astype(BF16)
        for h in range(HEADS):
            r0 = h * (HEAD_W + VT_PAD)
            vt_ref[c, r0:r0 + HEAD_W, :] = vt[h * HEAD_W:(h + 1) * HEAD_W, cols].astype(BF16)
            vt_ref[c, r0 + HEAD_W:r0 + HEAD_W + VT_PAD, :] = jnp.ones((VT_PAD, ATT_BLOCK), BF16)


def _inproj_small(x2d, n1, w_a, gmat, qw, kw):
    n, d = x2d.shape
    full = lambda width: _const_spec((n, width))
    return pl.pallas_call(
        _inproj_small_kernel,
        grid=(1,),
        in_specs=[full(d), _const_spec((1, d)), _const_spec(w_a.shape), _const_spec(gmat.shape),
                  _const_spec((1, 512)), _const_spec((1, 512))],
        out_specs=[full(2048), full(512), full(512), full(512), full(512), full(512)],
        out_shape=[jax.ShapeDtypeStruct((n, 2048), F32), jax.ShapeDtypeStruct((n, 512), BF16),
                   jax.ShapeDtypeStruct((n, 512), F32), jax.ShapeDtypeStruct((n, 512), BF16),
                   jax.ShapeDtypeStruct((n, 512), F32), jax.ShapeDtypeStruct((n, 512), BF16)],
        compiler_params=pltpu.CompilerParams(dimension_semantics=("arbitrary",),
                                             vmem_limit_bytes=VMEM_LIMIT),
        name="inproj_small",
    )(x2d, n1, w_a, gmat, qw, kw)


def _inproj_prompt(x2d, n1, w_a, gmat, qw, kw, kbase, vbase, seq, tm):
    n, d = x2d.shape
    assert seq % tm == 0 and tm % ATT_BLOCK == 0
    per = tm // ATT_BLOCK
    tiles = seq // tm
    row = lambda width: pl.BlockSpec((tm, width), lambda i: (i, 0))
    t_spec = lambda rows: pl.BlockSpec((per, rows, ATT_BLOCK), lambda i: (i, 0, 0))
    t_shape = lambda rows: jax.ShapeDtypeStruct((n // ATT_BLOCK, rows, ATT_BLOCK), BF16)

    def cache_rows(i):
        tok = (i // tiles) * (N_META + seq) + N_META + (i % tiles) * tm
        return pl.multiple_of(tok * HEADS, 8 * HEADS), 0

    cache_spec = pl.BlockSpec((pl.Element(tm * HEADS), pl.Element(HEAD_W)), cache_rows)
    any_spec = pl.BlockSpec(memory_space=pl.ANY)
    return pl.pallas_call(
        _inproj_prompt_kernel,
        grid=(n // tm,),
        in_specs=[row(d), _const_spec((1, d)), _const_spec(w_a.shape), _const_spec(gmat.shape),
                  _const_spec((1, 512)), _const_spec((1, 512)), any_spec, any_spec],
        out_specs=[row(2048), t_spec(512), cache_spec, row(512), cache_spec, t_spec(VT_ROWS)],
        out_shape=[jax.ShapeDtypeStruct((n, 2048), F32), t_shape(512),
                   jax.ShapeDtypeStruct(kbase.shape, F32), jax.ShapeDtypeStruct((n, 512), BF16),
                   jax.ShapeDtypeStruct(vbase.shape, F32), t_shape(VT_ROWS)],
        input_output_aliases={6: 2, 7: 4},
        compiler_params=pltpu.CompilerParams(dimension_semantics=("parallel",),
                                             vmem_limit_bytes=VMEM_LIMIT),
        name="inproj",
    )(x2d, n1, w_a, gmat, qw, kw, kbase, vbase)


def _hgrn_kernel(z_ref, lbl_ref, onw_ref, s0_ref, o_ref, sout_ref, st_scr, *, rows):
    C = HG_CHUNK
    W = HEADS * HEAD_W
    n_sub = max(rows // C, 1)
    n_valid = min(rows, C)
    c = pl.program_id(1)
    heads = [slice(h * HEAD_W, (h + 1) * HEAD_W) for h in range(HEADS)]

    @pl.when(c == 0)
    def _():
        for h in range(HEADS):
            st_scr[h] = s0_ref[0, h].T

    lg = lbl_ref[...]
    e = jnp.exp(lg - jnp.max(lg, axis=0, keepdims=True))
    lb = e[0:1] / jnp.sum(e, axis=0, keepdims=True)
    half = 0.5 * (1.0 - lb)
    onw = onw_ref[...]
    row = lax.broadcasted_iota(jnp.int32, (C, C), 0)
    col = lax.broadcasted_iota(jnp.int32, (C, C), 1)
    tri = jnp.where(col <= row, 1.0, 0.0).astype(BF16)

    def shr(x, width):
        return lax.shift_right_logical(x, int(math.log2(width)))

    def front(z, n_valid):
        hq, hf, hi, hg = z[:, 0:W], z[:, W:2 * W], z[:, 2 * W:3 * W], z[:, 3 * W:4 * W]
        q = _silu(hq)
        f = (lb + half) + half * jnp.tanh(0.5 * hf)
        k = 1.0 - f
        logf = jnp.log2(f)
        if n_valid < C:
            valid = lax.broadcasted_iota(jnp.int32, (C, W), 0) < n_valid
            k = jnp.where(valid, k, 0.0)
            logf = jnp.where(valid, logf, 0.0)
        x1 = logf.astype(BF16)
        r1 = logf - x1.astype(F32)
        x2 = r1.astype(BF16)
        x3 = (r1 - x2.astype(F32)).astype(BF16)
        b = _dot(tri, x1) + _dot(tri, x2) + _dot(tri, x3)
        return q, k, hi.astype(BF16), b, _silu(hg)

    def intra(q, k, b):
        b_last = b[C - 1:C]

        def block_row(width, r):
            b3 = b.reshape(C // width, width, W)
            return jnp.broadcast_to(b3[:, r:r + 1, :], b3.shape).reshape(C, W)

        def next_block_first(width):
            b3 = b.reshape(C // width, width, W)
            nxt = jnp.concatenate([b3[1:, 0:1, :], b_last[None]], axis=0)
            return jnp.broadcast_to(nxt, b3.shape).reshape(C, W)

        mid = block_row(HG_BASE, HG_BASE // 2)
        qe = (q * jnp.exp2(b - mid)).astype(BF16)
        ke = (k * jnp.exp2(mid - b)).astype(BF16)
        mask = (shr(row, HG_BASE) == shr(col, HG_BASE)) & (col <= row)
        scores = [jnp.where(mask, _dot_nt(qe[:, hs], ke[:, hs]), 0.0) for hs in heads]
        width = HG_BASE
        while width < n_valid:
            qe = (q * jnp.exp2(b - block_row(width, 0))).astype(BF16)
            ke = (k * jnp.exp2(next_block_first(width) - b)).astype(BF16)
            mask = ((shr(row, width) == shr(col, width) + 1)
                    & (shr(row, 2 * width) == shr(col, 2 * width)))
            scores = [jnp.where(mask, _dot_nt(qe[:, hs], ke[:, hs]), sc)
                      for hs, sc in zip(heads, scores)]
            width *= 2
        qb = (q * jnp.exp2(b)).astype(BF16)
        kd = (k * jnp.exp2(b_last - b)).astype(BF16)
        return [sc.astype(BF16) for sc in scores], qb, kd, jnp.exp2(b_last)

    fronts = []
    for u in range(n_sub):
        z = z_ref[0, u * n_valid:(u + 1) * n_valid, :]
        if n_valid < C:
            z = jnp.concatenate([z, jnp.zeros((C - n_valid, z.shape[1]), F32)], axis=0)
        fronts.append(front(z, n_valid))
    intras = [intra(q, k, b) for q, k, _, b, _ in fronts]
    for u in range(n_sub):
        _, _, v_bf, _, gate = fronts[u]
        scores, qb, kd, decay = intras[u]
        outs = []
        for h, hs in enumerate(heads):
            st = st_scr[h]
            outs.append(_dot_nt(qb[:, hs], st.astype(BF16)) + _dot(scores[h], v_bf[:, hs]))
            st_scr[h] = st * decay[:, hs] + _dot_tn(v_bf[:, hs], kd[:, hs])
        for h, hs in enumerate(heads):
            on = _rms_rows(outs[h], onw[:, hs]) * gate[:, hs]
            o_ref[0, u * n_valid:(u + 1) * n_valid, hs] = on[:n_valid].astype(o_ref.dtype)

    @pl.when(c == pl.num_programs(1) - 1)
    def _():
        for h in range(HEADS):
            sout_ref[0, h] = st_scr[h].T


def _hgrn(zh, lbl, onw, s0, *, shared_state, out_dtype):
    nseq, t, _ = zh.shape
    rows = min(t, HG_STEP_CHUNKS * HG_CHUNK)
    assert t % rows == 0 and rows % 8 == 0 and (rows <= HG_CHUNK or rows % HG_CHUNK == 0)
    s_map = (lambda b, c: (0, 0, 0, 0)) if shared_state else (lambda b, c: (b, 0, 0, 0))
    return pl.pallas_call(
        functools.partial(_hgrn_kernel, rows=rows),
        grid=(nseq, t // rows),
        in_specs=[pl.BlockSpec((1, rows, 2048), lambda b, c: (b, c, 0)),
                  _const_spec(lbl.shape), _const_spec((1, 512)),
                  pl.BlockSpec((1, HEADS, HEAD_W, HEAD_W), s_map)],
        out_specs=[pl.BlockSpec((1, rows, 512), lambda b, c: (b, c, 0)),
                   pl.BlockSpec((1, HEADS, HEAD_W, HEAD_W), lambda b, c: (b, 0, 0, 0))],
        out_shape=[jax.ShapeDtypeStruct((nseq, t, 512), out_dtype),
                   jax.ShapeDtypeStruct((nseq, HEADS, HEAD_W, HEAD_W), F32)],
        scratch_shapes=[pltpu.VMEM((HEADS, HEAD_W, HEAD_W), F32)],
        compiler_params=pltpu.CompilerParams(dimension_semantics=("parallel", "arbitrary"),
                                             vmem_limit_bytes=VMEM_LIMIT),
        name="hgrn",
    )(zh, lbl, onw, s0)


def _lambda_value(lp, lam_init):
    s1 = jnp.sum(lp[0:1] * lp[1:2], axis=1, keepdims=True)
    s2 = jnp.sum(lp[2:3] * lp[3:4], axis=1, keepdims=True)
    return jnp.exp(s1) - jnp.exp(s2) + lam_init


def _softmax_update(s, v_bf, m_ref, l_ref, acc_ref, first):
    m_cur = jnp.max(s, axis=1, keepdims=True)
    if first:
        m_new = m_cur
        p = jnp.exp2(s - m_new)
        l_ref[...] = jnp.sum(p, axis=1, keepdims=True)
        acc_ref[...] = _dot(p.astype(BF16), v_bf)
    else:
        m_prev = m_ref[...]
        m_new = jnp.maximum(m_prev, m_cur)
        alpha = jnp.exp2(m_prev - m_new)
        p = jnp.exp2(s - m_new)
        l_ref[...] = alpha * l_ref[...] + jnp.sum(p, axis=1, keepdims=True)
        acc_ref[...] = alpha * acc_ref[...] + _dot(p.astype(BF16), v_bf)
    m_ref[...] = m_new


def _pattn_kernel(qt_ref, k_ref, vt_ref, km_ref, vmt_ref, tb_ref, tm_ref, lam_ref, onw_ref, o_ref,
                  qz_scr, m_scr, acc_scr, *, lam_init):
    blk = ATT_BLOCK
    i = pl.program_id(1)
    sub = lax.broadcasted_iota(jnp.int32, (HEAD_W, blk), 0)
    for h in range(HEADS):
        qh = qt_ref[0, h * HEAD_W:(h + 1) * HEAD_W, :]
        zero = jnp.zeros_like(qh)
        qz_scr[h] = jnp.concatenate([jnp.where(sub < DA_DH, qh, zero),
                                     jnp.where(sub >= DA_DH, qh, zero)], axis=1)

    hv = lambda h: slice(h * (HEAD_W + VT_PAD), (h + 1) * (HEAD_W + VT_PAD))

    meta_sel = jnp.minimum(i, 1)
    META = "meta"

    def key_blocks(blocks):
        def qk(j, table, h):
            hs = slice(h * HEAD_W, (h + 1) * HEAD_W)
            if j is META:
                return _dot(km_ref[:, hs], qz_scr[h]) + tm_ref[meta_sel, h]
            off = pl.multiple_of(j * blk, blk)
            s = _dot(k_ref[0, pl.ds(off, blk), hs], qz_scr[h])
            return s if table is None else s + tb_ref[table, h]

        def soft(j, h, s):
            m_cur = jnp.max(s, axis=0, keepdims=True)
            if j is META:
                m_scr[h] = m_cur
                return jnp.exp2(s - m_cur).astype(BF16), None
            m_prev = m_scr[h]
            m_new = jnp.maximum(m_prev, m_cur)
            m_scr[h] = m_new
            return jnp.exp2(s - m_new).astype(BF16), jnp.exp2(m_prev - m_new)

        def pv(j, h, p, alpha):
            if j is META:
                acc_scr[h] = _dot(vmt_ref[hv(h), :], p)
            else:
                acc_scr[h] = alpha * acc_scr[h] + _dot(vt_ref[j, hv(h), :], p)

        stages = [(j, table, h) for j, table in blocks for h in range(HEADS)]
        s_cur = qk(*stages[0])
        pend = None
        for n, (j, table, h) in enumerate(stages):
            s_next = qk(*stages[n + 1]) if n + 1 < len(stages) else None
            if pend is not None:
                pv(stages[n - 1][0], stages[n - 1][2], *pend)
            pend = soft(j, h, s_cur)
            s_cur = s_next
        pv(stages[-1][0], stages[-1][2], *pend)

    @pl.when(i >= 1)
    def _():
        key_blocks([(META, None), (i - 1, 1), (i, 0)])

    @pl.when(i == 0)
    def _():
        key_blocks([(META, None), (0, 0)])

    n_far = jnp.maximum(i - 1, 0)
    done = 0
    width = 1
    while width < FAR_UNROLL:
        start = done

        @pl.when((n_far & width) != 0)
        def _(start=start, width=width):
            key_blocks([(start + u, None) for u in range(width)])

        done = done + (n_far & width)
        width *= 2
    n_rem = done

    def far_body(t, carry):
        j = n_rem + FAR_UNROLL * t
        key_blocks([(j + u, None) for u in range(FAR_UNROLL)])
        return carry

    lax.fori_loop(0, n_far // FAR_UNROLL, far_body, 0)

    lam = _lambda_value(lam_ref[...], lam_init)
    onw = onw_ref[...]
    for h in range(HEADS):
        acc = acc_scr[h]
        a = acc[:HEAD_W] / acc[HEAD_W:HEAD_W + 1]
        ot = a[:, :blk] - lam * a[:, blk:]
        ot = ot * lax.rsqrt(jnp.mean(ot * ot, axis=0, keepdims=True) + EPS) * onw * (1.0 - lam_init)
        o_ref[0, :, h * HEAD_W:(h + 1) * HEAD_W] = ot.T.astype(o_ref.dtype)


def _pattn(qt, k, vt, km, vmt, tb, tmeta, lam_p, onw_col, lam_init):
    bsz, t, w = k.shape
    blk = ATT_BLOCK
    nblk = t // blk
    assert t % blk == 0
    return pl.pallas_call(
        functools.partial(_pattn_kernel, lam_init=lam_init),
        grid=(bsz, nblk),
        in_specs=[pl.BlockSpec((1, w, blk), lambda b, i: (b * nblk + i, 0, 0)),
                  pl.BlockSpec((1, t, w), lambda b, i: (b, 0, 0)),
                  pl.BlockSpec((nblk, VT_ROWS, blk), lambda b, i: (b, 0, 0)),
                  _const_spec(km.shape), _const_spec(vmt.shape), _const_spec(tb.shape),
                  _const_spec(tmeta.shape), _const_spec(lam_p.shape), _const_spec(onw_col.shape)],
        out_specs=pl.BlockSpec((1, blk, w), lambda b, i: (b, i, 0)),
        out_shape=jax.ShapeDtypeStruct((bsz, t, w), BF16),
        scratch_shapes=[pltpu.VMEM((HEADS, HEAD_W, 2 * blk), BF16),
                        pltpu.VMEM((HEADS, 1, 2 * blk), F32),
                        pltpu.VMEM((HEADS, HEAD_W + VT_PAD, 2 * blk), F32)],
        compiler_params=pltpu.CompilerParams(dimension_semantics=("parallel", "arbitrary"),
                                             vmem_limit_bytes=VMEM_LIMIT),
        name="prompt_attn",
    )(qt, k, vt, km, vmt, tb, tmeta, lam_p, onw_col)


def _sattn_kernel(pt_ref, q_ref, kn_ref, vn_ref, tl_ref, tn_ref, tmask_ref, lam_ref, onw_ref,
                  ck_hbm, cv_hbm, o_ref, kbuf, vbuf, sem, m_scr, l_scr, acc_scr,
                  *, lam_init, n_tok, n_pages):
    pg = PAGES_PER_STEP
    prow = kbuf.shape[2]
    b, g = pl.program_id(0), pl.program_id(1)
    n_steps = pl.num_programs(1)
    t = b * n_steps + g
    slot = t % 2

    def page_copies(step, to_slot):
        base = (step // n_steps) * n_pages + (step % n_steps) * pg
        copies = []
        for u in range(pg):
            rows = pl.ds(pl.multiple_of(pt_ref[base + u] * prow, prow), prow)
            copies.append(pltpu.make_async_copy(ck_hbm.at[rows, :], kbuf.at[to_slot, u],
                                                sem.at[to_slot, 0]))
            copies.append(pltpu.make_async_copy(cv_hbm.at[rows, :], vbuf.at[to_slot, u],
                                                sem.at[to_slot, 1]))
        return copies

    @pl.when(t == 0)
    def _():
        for cp in page_copies(t, slot):
            cp.start()

    @pl.when(t + 1 < pl.num_programs(0) * n_steps)
    def _():
        for cp in page_copies(t + 1, 1 - slot):
            cp.start()

    q = q_ref[0]

    @pl.when(g == 0)
    def _():
        s = _dot_nt(q, kn_ref[0]) + tn_ref[...]
        _softmax_update(s, vn_ref[0], m_scr, l_scr, acc_scr, True)

    for cp in page_copies(t, slot):
        cp.wait()

    is_last = g == n_steps - 1
    tmask = tmask_ref[...]
    ss = []
    for u in range(pg):
        s = _dot_nt(q, kbuf[slot, u].astype(BF16))
        ss.append(s + (jnp.where(is_last, tl_ref[...], tmask) if u == pg - 1 else tmask))
    m_prev = m_scr[...]
    m_new = m_prev
    for s in ss:
        m_new = jnp.maximum(m_new, jnp.max(s, axis=1, keepdims=True))
    alpha = jnp.exp2(m_prev - m_new)
    l_new = alpha * l_scr[...]
    acc = alpha * acc_scr[...]
    for u in range(pg):
        p = jnp.exp2(ss[u] - m_new)
        l_new = l_new + jnp.sum(p, axis=1, keepdims=True)
        acc = acc + _dot(p.astype(BF16), vbuf[slot, u].astype(BF16))
    m_scr[...] = m_new
    l_scr[...] = l_new
    acc_scr[...] = acc

    @pl.when(is_last)
    def _():
        lam = _lambda_value(lam_ref[...], lam_init)
        onw = onw_ref[...]
        a = acc_scr[...] / l_scr[...]
        for h in range(HEADS):
            r1 = (2 * h) * n_tok
            r2 = (2 * h + 1) * n_tok
            o = a[r1:r1 + n_tok] - lam * a[r2:r2 + n_tok]
            o_ref[0, :, h * HEAD_W:(h + 1) * HEAD_W] = (
                _rms_rows(o, onw) * (1.0 - lam_init)).astype(o_ref.dtype)


def _sattn(page_table, qall, knew, vnew, tlast, tnew, tmask, lam_p, onw, ck2, cv2, lam_init, n_tok):
    nb, n_pages = page_table.shape
    pg = PAGES_PER_STEP
    assert n_pages % pg == 0
    nrow = qall.shape[1]
    prow = knew.shape[1]
    pt_flat = page_table.reshape(-1)
    per_b = lambda shape: pl.BlockSpec((1,) + shape, lambda b, g, pt: (b, 0, 0))
    const = lambda shape: pl.BlockSpec(shape, lambda b, g, pt: (0,) * len(shape))
    hbm = pl.BlockSpec(memory_space=pl.ANY)
    grid_spec = pltpu.PrefetchScalarGridSpec(
        num_scalar_prefetch=1,
        grid=(nb, n_pages // pg),
        in_specs=[per_b((nrow, HEAD_W)), per_b((prow, HEAD_W)), per_b((prow, HEAD_W)),
                  const(tlast.shape), const(tnew.shape), const(tmask.shape),
                  const(lam_p.shape), const(onw.shape), hbm, hbm],
        out_specs=per_b((n_tok, HEADS * HEAD_W)),
        scratch_shapes=[pltpu.VMEM((2, pg, prow, HEAD_W), F32), pltpu.VMEM((2, pg, prow, HEAD_W), F32),
                        pltpu.SemaphoreType.DMA((2, 2)),
                        pltpu.VMEM((nrow, 1), F32), pltpu.VMEM((nrow, 1), F32),
                        pltpu.VMEM((nrow, HEAD_W), F32)],
    )
    return pl.pallas_call(
        functools.partial(_sattn_kernel, lam_init=lam_init, n_tok=n_tok, n_pages=n_pages),
        grid_spec=grid_spec,
        out_shape=jax.ShapeDtypeStruct((nb, n_tok, HEADS * HEAD_W), F32),
        compiler_params=pltpu.CompilerParams(dimension_semantics=("arbitrary", "arbitrary"),
                                             vmem_limit_bytes=VMEM_LIMIT),
        name="sample_attn",
    )(pt_flat, qall, knew, vnew, tlast, tnew, tmask, lam_p, onw, ck2, cv2)


def _merge_kernel(x_ref, oa_ref, ob_ref, n1_ref, wg_ref, wha_ref, wda_ref, wo_ref, h_ref):
    x = x_ref[...]
    d = x.shape[1]
    xn = _rms_rows(x, n1_ref[...]).astype(BF16)
    gates = _dot(xn, wg_ref[...])
    ya = _dot(oa_ref[...], wha_ref[...])
    yb = _dot(ob_ref[...], wda_ref[...])
    merged = _sigmoid(gates[:, :d]) * ya + _sigmoid(gates[:, d:]) * yb
    h_ref[...] = x + _dot(merged.astype(BF16), wo_ref[...])


def _merge(x2d, oa, ob, n1, w_g, w_ha, w_da, w_o, tm):
    n, d = x2d.shape
    row = lambda width: pl.BlockSpec((tm, width), lambda i: (i, 0))
    return pl.pallas_call(
        _merge_kernel,
        grid=(n // tm,),
        in_specs=[row(d), row(512), row(512), _const_spec((1, d)), _const_spec(w_g.shape),
                  _const_spec(w_ha.shape), _const_spec(w_da.shape), _const_spec(w_o.shape)],
        out_specs=row(d),
        out_shape=jax.ShapeDtypeStruct((n, d), F32),
        compiler_params=pltpu.CompilerParams(dimension_semantics=("parallel",),
                                             vmem_limit_bytes=VMEM_LIMIT),
        name="merge",
    )(x2d, oa, ob, n1, w_g, w_ha, w_da, w_o)


def _moe_kernel(h_ref, n2_ref, wr_hi_ref, wr_lo_ref, br_ref, wgu_ref, wd_ref, y_ref):
    h = h_ref[...]
    tm = h.shape[0]
    hn = _rms_rows(h, n2_ref[...])
    hn_hi = hn.astype(BF16)
    hn_lo = (hn - hn_hi.astype(F32)).astype(BF16)
    logits = (_dot(hn_hi, wr_hi_ref[...]) + _dot(hn_lo, wr_hi_ref[...])
              + _dot(hn_hi, wr_lo_ref[...]) + br_ref[...])
    lane = lax.broadcasted_iota(jnp.int32, logits.shape, 1)
    big = jnp.int32(1 << 20)
    ninf = -jnp.inf

    def first_lane(mask):
        return jnp.min(jnp.where(mask, lane, big), axis=1, keepdims=True)

    is_g = lane < N_GROUPS
    g_max = jnp.max(jnp.where(is_g, logits, ninf), axis=1, keepdims=True)
    g_idx = first_lane(is_g & (logits == g_max))
    g_sum = jnp.sum(jnp.where(is_g, jnp.exp(logits - g_max), 0.0), axis=1, keepdims=True)
    p_g = 1.0 / g_sum
    e_lane = lane - N_GROUPS
    in_group = ((e_lane >= 0) & (e_lane < N_EXPERTS)
                & (lax.shift_right_arithmetic(e_lane, 2) == g_idx))
    e_max = jnp.max(jnp.where(in_group, logits, ninf), axis=1, keepdims=True)
    e_exp = jnp.where(in_group, jnp.exp(logits - e_max), 0.0)
    e_prob = e_exp / jnp.sum(e_exp, axis=1, keepdims=True)
    p1 = jnp.max(jnp.where(in_group, e_prob, -1.0), axis=1, keepdims=True)
    i1 = first_lane(in_group & (e_prob == p1))
    rest = in_group & (lane != i1)
    p2 = jnp.max(jnp.where(rest, e_prob, -1.0), axis=1, keepdims=True)
    i2 = first_lane(rest & (e_prob == p2))
    denom = p1 + p2
    combine = (jnp.where(lane == i1, p_g * p1 / denom, 0.0)
               + jnp.where(lane == i2, p_g * p2 / denom, 0.0))

    ff = wd_ref.shape[1]
    acc = jnp.zeros_like(h)
    for ex in range(N_EXPERTS):
        gu = _dot(hn_hi, wgu_ref[ex])
        gt, up = gu[:, :ff], gu[:, ff:]
        hh = _silu(gt) * up * combine[:, N_GROUPS + ex:N_GROUPS + ex + 1]
        acc = acc + _dot(hh.astype(BF16), wd_ref[ex])
    y_ref[...] = h + acc


def _moe(h2d, n2, wr_hi, wr_lo, br, wgu, wd, tm):
    n, d = h2d.shape
    row = pl.BlockSpec((tm, d), lambda i: (i, 0))
    single = lambda shape: pl.BlockSpec(shape, lambda i: (0,) * len(shape),
                                        pipeline_mode=pl.Buffered(1))
    return pl.pallas_call(
        _moe_kernel,
        grid=(n // tm,),
        in_specs=[row, _const_spec((1, d)), _const_spec(wr_hi.shape), _const_spec(wr_lo.shape),
                  _const_spec(br.shape), single(wgu.shape), single(wd.shape)],
        out_specs=row,
        out_shape=jax.ShapeDtypeStruct((n, d), F32),
        compiler_params=pltpu.CompilerParams(dimension_semantics=("parallel",),
                                             vmem_limit_bytes=VMEM_LIMIT),
        name="moe",
    )(h2d, n2, wr_hi, wr_lo, br, wgu, wd)


def _t5_bias(dist, rel_bias):
    n = jnp.maximum(dist, 0)
    max_exact = N_BUCKETS // 2
    nf = jnp.maximum(n, 1).astype(F32)
    scaled = (jnp.log(nf / max_exact) / math.log(MAX_DISTANCE / max_exact)
              * (N_BUCKETS - max_exact))
    large = max_exact + jnp.floor(jnp.maximum(scaled, 0.0)).astype(jnp.int32)
    large = jnp.minimum(large, N_BUCKETS - 1)
    bucket = jnp.where(n < max_exact, n, large)
    rb = rel_bias.astype(F32)
    onehot = (bucket[..., None] == jnp.arange(N_BUCKETS)).astype(F32)
    bias = jnp.moveaxis(jnp.dot(onehot, rb, precision=lax.Precision.HIGHEST), -1, 0)
    far = rb[N_BUCKETS - 1].reshape((HEADS,) + (1,) * dist.ndim)
    return jnp.where(dist >= 0, (bias - far) * LOG2E, MASK_VALUE)


def kernel(x_prompt, x_sample, cache_k, cache_v, state_hgrn, page_table, meta_tokens, rel_bias,
           hg_lb_logits, norm1_w, w_in, hg_onorm_w, w_hg_out, q_norm_w, k_norm_w, da_lambda,
           da_onorm_w, w_da_out, w_o, norm2_w, w_router_group, b_router_group,
           w_router_expert, b_router_expert, w_gate, w_up, w_down):
    bsz, seq, d = x_prompt.shape
    nb, n_tok, _ = x_sample.shape
    depth = w_in.shape[0]
    assert depth == 1 and hg_lb_logits.shape[0] == 2
    n_pages = page_table.shape[1]
    psize = cache_k.shape[2]
    past_len = n_pages * psize
    blk = ATT_BLOCK
    assert blk >= MAX_DISTANCE and psize >= MAX_DISTANCE and n_tok <= 8
    lam_init = 0.8 - 0.6 * math.exp(-0.3 * 0)
    w_attn = HEADS * HEAD_W

    w_in0 = w_in[0]
    w_a = w_in0[:, :3584].astype(BF16)
    w_g = w_in0[:, 3584:].astype(BF16)
    n1 = norm1_w[0].reshape(1, d)
    n2 = norm2_w[0].reshape(1, d)
    qw = (jnp.tile(q_norm_w[0], 2 * HEADS) * (DA_DH ** -0.5 * LOG2E)).reshape(1, w_attn)
    kw = jnp.tile(k_norm_w[0], 2 * HEADS).reshape(1, w_attn)
    gi = jnp.arange(w_attn) // DA_DH
    gmat = jnp.where(gi[:, None] == gi[None, :], 1.0 / DA_DH, 0.0).astype(BF16)
    hg_onw = jnp.tile(hg_onorm_w[0], HEADS).reshape(1, w_attn)
    da_onw = da_onorm_w[0].reshape(1, HEAD_W)
    lam_p = da_lambda[0].astype(F32)
    w_ha = w_hg_out[0].astype(BF16)
    w_da = w_da_out[0].astype(BF16)
    w_o0 = w_o[0].astype(BF16)
    wr = jnp.concatenate([w_router_group[0], w_router_expert[0],
                          jnp.zeros((d, 128 - N_GROUPS - N_EXPERTS), F32)], axis=1)
    wr_hi = wr.astype(BF16)
    wr_lo = (wr - wr_hi.astype(F32)).astype(BF16)
    br = jnp.concatenate([b_router_group[0], b_router_expert[0],
                          jnp.zeros((128 - N_GROUPS - N_EXPERTS,), F32)]).reshape(1, 128)
    wgu = jnp.concatenate([w_gate[0], w_up[0]], axis=2).astype(BF16)
    wd = w_down[0].astype(BF16)

    n_small = N_META + nb * n_tok
    x_small = jnp.concatenate([meta_tokens.astype(F32), x_sample.reshape(nb * n_tok, d)], axis=0)
    zh_s, q_s, kf_s, kb_s, vf_s, vb_s = _inproj_small(x_small, n1, w_a, gmat, qw, kw)

    _, s_meta = _hgrn(zh_s[:N_META][None], hg_lb_logits, hg_onw,
                      jnp.zeros((1, HEADS, HEAD_W, HEAD_W), F32), shared_state=False, out_dtype=F32)
    oa_s, s_sample = _hgrn(zh_s[N_META:].reshape(nb, n_tok, 2048), hg_lb_logits, hg_onw,
                           state_hgrn[0], shared_state=False, out_dtype=F32)

    x_p = x_prompt.reshape(bsz * seq, d)
    def cache_base(meta_rows):
        meta_b = jnp.broadcast_to(meta_rows.reshape(1, N_META * HEADS, HEAD_W),
                                  (bsz, N_META * HEADS, HEAD_W))
        return jnp.concatenate([meta_b, jnp.zeros((bsz, seq * HEADS, HEAD_W), F32)],
                               axis=1).reshape(-1, HEAD_W)

    zh_p, qt_p, kf_p, kb_p, vf_p, vt_p = _inproj_prompt(
        x_p, n1, w_a, gmat, qw, kw, cache_base(kf_s[:N_META]), cache_base(vf_s[:N_META]), seq, 512)
    oa_p, s_prompt = _hgrn(zh_p.reshape(bsz, seq, 2048), hg_lb_logits, hg_onw, s_meta,
                           shared_state=True, out_dtype=BF16)

    r = jnp.arange(blk)
    tb = jnp.stack([_t5_bias(r[None, :] - r[:, None], rel_bias),
                    _t5_bias(blk + r[None, :] - r[:, None], rel_bias)])
    tb = jnp.concatenate([tb, tb], axis=3)
    mrow = jnp.arange(128)
    t_meta0 = _t5_bias(N_META + r[None, :] - mrow[:, None], rel_bias)
    t_meta = jnp.stack([t_meta0, jnp.zeros_like(t_meta0)])
    t_meta = jnp.where(mrow[:, None] < N_META, t_meta, MASK_VALUE)
    t_meta = jnp.concatenate([t_meta, t_meta], axis=3)
    pad_meta = lambda a: jnp.concatenate([a, jnp.zeros((128 - N_META, w_attn), a.dtype)], axis=0)
    vmt = pad_meta(vb_s[:N_META]).T.reshape(HEADS, HEAD_W, 128)
    vmt = jnp.concatenate([vmt, jnp.ones((HEADS, VT_PAD, 128), BF16)], axis=1).reshape(VT_ROWS, 128)
    ob_p = _pattn(qt_p, kb_p.reshape(bsz, seq, w_attn), vt_p, pad_meta(kb_s[:N_META]),
                  vmt, tb, t_meta, lam_p, da_onw.reshape(HEAD_W, 1), lam_init)

    q_tok = q_s[N_META:].reshape(nb, n_tok, HEADS, 1, 2, DA_DH)
    q_maps = q_tok * jnp.eye(2, dtype=BF16).reshape(1, 1, 1, 2, 2, 1)
    n_rows = 2 * HEADS * n_tok
    qall = jnp.transpose(q_maps, (0, 2, 3, 1, 4, 5)).reshape(nb, n_rows, HEAD_W)
    prow = psize * HEADS
    pad_new = lambda a: jnp.concatenate(
        [a.reshape(nb, n_tok * HEADS, HEAD_W),
         jnp.zeros((nb, prow - n_tok * HEADS, HEAD_W), a.dtype)], axis=1)
    s_idx = jnp.tile(jnp.arange(n_tok), 2 * HEADS)
    head_of_row = jnp.repeat(jnp.arange(HEADS), 2 * n_tok)
    key_r = jnp.arange(prow) // HEADS
    own_head = head_of_row[:, None] == (jnp.arange(prow) % HEADS)[None, :]
    dist_last = (past_len + s_idx[:, None]) - (past_len - psize + key_r[None, :])
    dist_new = jnp.where(key_r[None, :] < n_tok, s_idx[:, None] - key_r[None, :], -1)
    pick = lambda t: jnp.where(own_head, t[head_of_row, jnp.arange(n_rows)], MASK_VALUE)
    t_last = pick(_t5_bias(dist_last, rel_bias))
    t_new = pick(_t5_bias(dist_new, rel_bias))
    t_mask = jnp.where(own_head, 0.0, MASK_VALUE).astype(F32)
    ob_s = _sattn(page_table, qall, pad_new(kb_s[N_META:]), pad_new(vb_s[N_META:]), t_last, t_new,
                  t_mask, lam_p, da_onw, cache_k.reshape(-1, HEAD_W), cache_v.reshape(-1, HEAD_W),
                  lam_init, n_tok)

    h_p = _merge(x_p, oa_p.reshape(bsz * seq, w_attn), ob_p.reshape(bsz * seq, w_attn),
                 n1, w_g, w_ha, w_da, w_o0, 512)
    y_p = _moe(h_p, n2, wr_hi, wr_lo, br, wgu, wd, 512)
    x_s = x_small[N_META:]
    h_s = _merge(x_s, oa_s.reshape(nb * n_tok, w_attn).astype(BF16),
                 ob_s.reshape(nb * n_tok, w_attn).astype(BF16), n1, w_g, w_ha, w_da, w_o0,
                 nb * n_tok)
    y_s = _moe(h_s, n2, wr_hi, wr_lo, br, wgu, wd, nb * n_tok)

    return (y_p.reshape(bsz, seq, d),
            y_s.reshape(nb, n_tok, d),
            kf_p.reshape(1, bsz, seq + N_META, HEADS, HEAD_W),
            vf_p.reshape(1, bsz, seq + N_META, HEADS, HEAD_W),
            s_prompt[None],
            kf_s[N_META:].reshape(1, nb, n_tok, HEADS, HEAD_W),
            vf_s[N_META:].reshape(1, nb, n_tok, HEADS, HEAD_W),
            s_sample[None])
```

```python
import functools
import math

import jax
import jax.numpy as jnp
from jax import lax
from jax.experimental import pallas as pl
from jax.experimental.pallas import tpu as pltpu

F32 = jnp.float32
BF16 = jnp.bfloat16

EPS = 1e-6
N_META = 16
HEADS = 4
HEAD_W = 128
DA_DH = 64
N_BUCKETS = 32
MAX_DISTANCE = 128
N_GROUPS = 4
EXPERTS_PER_GROUP = 4
N_EXPERTS = N_GROUPS * EXPERTS_PER_GROUP
MASK_VALUE = -1e30
HG_CHUNK = 128
HG_STEP_CHUNKS = 4
HG_BASE = 8
ATT_BLOCK = 256
VT_PAD = 16
VT_ROWS = HEADS * (HEAD_W + VT_PAD)
FAR_UNROLL = 8
PAGES_PER_STEP = 16
IN_A_COLS = 3584
LOG2E = math.log2(math.e)
VMEM_LIMIT = 56 * 1024 * 1024


def _dot(a, b):
    return jnp.dot(a, b, preferred_element_type=F32)


def _dot_nt(a, b):
    return lax.dot_general(a, b, (((1,), (1,)), ((), ())), preferred_element_type=F32)


def _dot_tn(a, b):
    return lax.dot_general(a, b, (((0,), (0,)), ((), ())), preferred_element_type=F32)


def _sigmoid(x):
    return 0.5 + 0.5 * jnp.tanh(0.5 * x)


def _silu(x):
    y = 0.5 * x
    return y + y * jnp.tanh(y)


def _rms_rows(x, w):
    return x * lax.rsqrt(jnp.mean(x * x, axis=-1, keepdims=True) + EPS) * w


def _const_spec(shape):
    zeros = (0,) * len(shape)
    return pl.BlockSpec(shape, lambda *_: zeros)


def _inproj_values(x_ref, n1_ref, w_ref, g_ref, qw_ref, kw_ref):
    xn = _rms_rows(x_ref[...], n1_ref[...]).astype(BF16)
    g = g_ref[...]

    def group_norm(z, w):
        sq = z * z
        hi = sq.astype(BF16)
        lo = (sq - hi.astype(F32)).astype(BF16)
        ms = _dot(hi, g) + _dot(lo, g)
        return z * lax.rsqrt(ms + EPS) * w

    zh = _dot(xn, w_ref[:, 0:2048])
    qn = group_norm(_dot(xn, w_ref[:, 2048:2560]), qw_ref[...])
    kn = group_norm(_dot(xn, w_ref[:, 2560:3072]), kw_ref[...])
    zv = _dot(xn, w_ref[:, 3072:3584])
    return zh, qn, kn, zv


def _inproj_small_kernel(x_ref, n1_ref, w_ref, g_ref, qw_ref, kw_ref,
                         zh_ref, q_ref, kf_ref, kb_ref, vf_ref, vb_ref):
    zh, qn, kn, zv = _inproj_values(x_ref, n1_ref, w_ref, g_ref, qw_ref, kw_ref)
    zh_ref[...] = zh
    q_ref[...] = qn.astype(BF16)
    kf_ref[...] = kn
    kb_ref[...] = kn.astype(BF16)
    vf_ref[...] = zv
    vb_ref[...] = zv.astype(BF16)


def _inproj_prompt_kernel(x_ref, n1_ref, w_ref, g_ref, qw_ref, kw_ref, kbase_ref, vbase_ref,
                          zh_ref, qt_ref, kf_ref, kb_ref, vf_ref, vt_ref):
    del kbase_ref, vbase_ref
    zh, qn, kn, zv = _inproj_values(x_ref, n1_ref, w_ref, g_ref, qw_ref, kw_ref)
    tm = zh.shape[0]
    zh_ref[...] = zh
    kb_ref[...] = kn.astype(BF16)
    for h in range(HEADS):
        hs = slice(h * HEAD_W, (h + 1) * HEAD_W)
        kf_ref[pl.ds(h, tm, stride=HEADS), :] = kn[:, hs]
        vf_ref[pl.ds(h, tm, stride=HEADS), :] = zv[:, hs]
    qt = qn.T
    vt = zv.T
    for c in range(tm // ATT_BLOCK):
        cols = slice(c * ATT_BLOCK, (c + 1) * ATT_BLOCK)
        qt_ref[c] = qt[:, cols].astype(BF16)
        for h in range(HEADS):
            r0 = h * (HEAD_W + VT_PAD)
            vt_ref[c, r0:r0 + HEAD_W, :] = vt[h * HEAD_W:(h + 1) * HEAD_W, cols].astype(BF16)
            vt_ref[c, r0 + HEAD_W:r0 + HEAD_W + VT_PAD, :] = jnp.ones((VT_PAD, ATT_BLOCK), BF16)


def _w_in_spec(d):
    return pl.BlockSpec((d, IN_A_COLS), lambda i: (0, 0))


def _inproj_small(x2d, n1, w_in_bf, gmat, qw, kw):
    n, d = x2d.shape
    full = lambda width: _const_spec((n, width))
    return pl.pallas_call(
        _inproj_small_kernel,
        grid=(1,),
        in_specs=[full(d), _const_spec((1, d)), _w_in_spec(d), _const_spec(gmat.shape),
                  _const_spec((1, 512)), _const_spec((1, 512))],
        out_specs=[full(2048), full(512), full(512), full(512), full(512), full(512)],
        out_shape=[jax.ShapeDtypeStruct((n, 2048), F32), jax.ShapeDtypeStruct((n, 512), BF16),
                   jax.ShapeDtypeStruct((n, 512), F32), jax.ShapeDtypeStruct((n, 512), BF16),
                   jax.ShapeDtypeStruct((n, 512), F32), jax.ShapeDtypeStruct((n, 512), BF16)],
        compiler_params=pltpu.CompilerParams(dimension_semantics=("arbitrary",),
                                             vmem_limit_bytes=VMEM_LIMIT),
        name="inproj_small",
    )(x2d, n1, w_in_bf, gmat, qw, kw)


def _inproj_prompt(x2d, n1, w_in_bf, gmat, qw, kw, kbase, vbase, seq, tm):
    n, d = x2d.shape
    cache_shape = jax.ShapeDtypeStruct(kbase.shape, F32)
    assert seq % tm == 0 and tm % ATT_BLOCK == 0
    per = tm // ATT_BLOCK
    tiles = seq // tm
    row = lambda width: pl.BlockSpec((tm, width), lambda i: (i, 0))
    t_spec = lambda rows: pl.BlockSpec((per, rows, ATT_BLOCK), lambda i: (i, 0, 0))
    t_shape = lambda rows: jax.ShapeDtypeStruct((n // ATT_BLOCK, rows, ATT_BLOCK), BF16)

    def cache_rows(i):
        tok = (i // tiles) * (N_META + seq) + N_META + (i % tiles) * tm
        return pl.multiple_of(tok * HEADS, 8 * HEADS), 0

    cache_spec = pl.BlockSpec((pl.Element(tm * HEADS), pl.Element(HEAD_W)), cache_rows)
    any_spec = pl.BlockSpec(memory_space=pl.ANY)
    return pl.pallas_call(
        _inproj_prompt_kernel,
        grid=(n // tm,),
        in_specs=[row(d), _const_spec((1, d)), _w_in_spec(d), _const_spec(gmat.shape),
                  _const_spec((1, 512)), _const_spec((1, 512)), any_spec, any_spec],
        out_specs=[row(2048), t_spec(512), cache_spec, row(512), cache_spec, t_spec(VT_ROWS)],
        out_shape=[jax.ShapeDtypeStruct((n, 2048), F32), t_shape(512), cache_shape,
                   jax.ShapeDtypeStruct((n, 512), BF16), cache_shape, t_shape(VT_ROWS)],
        input_output_aliases={6: 2, 7: 4},
        compiler_params=pltpu.CompilerParams(dimension_semantics=("parallel",),
                                             vmem_limit_bytes=VMEM_LIMIT),
        name="inproj",
    )(x2d, n1, w_in_bf, gmat, qw, kw, kbase, vbase)


def _hgrn_kernel(z_ref, lbl_ref, onw_ref, s0_ref, o_ref, sout_ref, st_scr, *, rows):
    C = HG_CHUNK
    W = HEADS * HEAD_W
    n_sub = max(rows // C, 1)
    n_valid = min(rows, C)
    c = pl.program_id(1)
    heads = [slice(h * HEAD_W, (h + 1) * HEAD_W) for h in range(HEADS)]

    @pl.when(c == 0)
    def _():
        for h in range(HEADS):
            st_scr[h] = s0_ref[0, h].T

    lg = lbl_ref[...]
    e = jnp.exp(lg - jnp.max(lg, axis=0, keepdims=True))
    lb = e[0:1] / jnp.sum(e, axis=0, keepdims=True)
    half = 0.5 * (1.0 - lb)
    onw = onw_ref[...]
    row = lax.broadcasted_iota(jnp.int32, (C, C), 0)
    col = lax.broadcasted_iota(jnp.int32, (C, C), 1)
    tri = jnp.where(col <= row, 1.0, 0.0).astype(BF16)

    def shr(x, width):
        return lax.shift_right_logical(x, int(math.log2(width)))

    def front(z, n_valid):
        hq, hf, hi, hg = z[:, 0:W], z[:, W:2 * W], z[:, 2 * W:3 * W], z[:, 3 * W:4 * W]
        q = _silu(hq)
        f = (lb + half) + half * jnp.tanh(0.5 * hf)
        k = 1.0 - f
        logf = jnp.log2(f)
        if n_valid < C:
            valid = lax.broadcasted_iota(jnp.int32, (C, W), 0) < n_valid
            k = jnp.where(valid, k, 0.0)
            logf = jnp.where(valid, logf, 0.0)
        x1 = logf.astype(BF16)
        r1 = logf - x1.astype(F32)
        x2 = r1.astype(BF16)
        x3 = (r1 - x2.astype(F32)).astype(BF16)
        b = _dot(tri, x1) + _dot(tri, x2) + _dot(tri, x3)
        return q, k, hi.astype(BF16), b, _silu(hg)

    def intra(q, k, b):
        b_last = b[C - 1:C]

        def block_row(width, r):
            b3 = b.reshape(C // width, width, W)
            return jnp.broadcast_to(b3[:, r:r + 1, :], b3.shape).reshape(C, W)

        def next_block_first(width):
            b3 = b.reshape(C // width, width, W)
            nxt = jnp.concatenate([b3[1:, 0:1, :], b_last[None]], axis=0)
            return jnp.broadcast_to(nxt, b3.shape).reshape(C, W)

        mid = block_row(HG_BASE, HG_BASE // 2)
        qe = (q * jnp.exp2(b - mid)).astype(BF16)
        ke = (k * jnp.exp2(mid - b)).astype(BF16)
        mask = (shr(row, HG_BASE) == shr(col, HG_BASE)) & (col <= row)
        scores = [jnp.where(mask, _dot_nt(qe[:, hs], ke[:, hs]), 0.0) for hs in heads]
        width = HG_BASE
        while width < n_valid:
            qe = (q * jnp.exp2(b - block_row(width, 0))).astype(BF16)
            ke = (k * jnp.exp2(next_block_first(width) - b)).astype(BF16)
            mask = ((shr(row, width) == shr(col, width) + 1)
                    & (shr(row, 2 * width) == shr(col, 2 * width)))
            scores = [jnp.where(mask, _dot_nt(qe[:, hs], ke[:, hs]), sc)
                      for hs, sc in zip(heads, scores)]
            width *= 2
        qb = (q * jnp.exp2(b)).astype(BF16)
        kd = (k * jnp.exp2(b_last - b)).astype(BF16)
        return [sc.astype(BF16) for sc in scores], qb, kd, jnp.exp2(b_last)

    fronts = []
    for u in range(n_sub):
        z = z_ref[0, u * n_valid:(u + 1) * n_valid, :]
        if n_valid < C:
            z = jnp.concatenate([z, jnp.zeros((C - n_valid, z.shape[1]), F32)], axis=0)
        fronts.append(front(z, n_valid))
    intras = [intra(q, k, b) for q, k, _, b, _ in fronts]
    for u in range(n_sub):
        _, _, v_bf, _, gate = fronts[u]
        scores, qb, kd, decay = intras[u]
        outs = []
        for h, hs in enumerate(heads):
            st = st_scr[h]
            outs.append(_dot_nt(qb[:, hs], st.astype(BF16)) + _dot(scores[h], v_bf[:, hs]))
            st_scr[h] = st * decay[:, hs] + _dot_tn(v_bf[:, hs], kd[:, hs])
        for h, hs in enumerate(heads):
            on = _rms_rows(outs[h], onw[:, hs]) * gate[:, hs]
            o_ref[0, u * n_valid:(u + 1) * n_valid, hs] = on[:n_valid].astype(o_ref.dtype)

    @pl.when(c == pl.num_programs(1) - 1)
    def _():
        for h in range(HEADS):
            sout_ref[0, h] = st_scr[h].T


def _hgrn(zh, lbl, onw, s0, *, shared_state, out_dtype):
    nseq, t, _ = zh.shape
    rows = min(t, HG_STEP_CHUNKS * HG_CHUNK)
    assert t % rows == 0 and rows % 8 == 0 and (rows <= HG_CHUNK or rows % HG_CHUNK == 0)
    s_map = (lambda b, c: (0, 0, 0, 0)) if shared_state else (lambda b, c: (b, 0, 0, 0))
    return pl.pallas_call(
        functools.partial(_hgrn_kernel, rows=rows),
        grid=(nseq, t // rows),
        in_specs=[pl.BlockSpec((1, rows, 2048), lambda b, c: (b, c, 0)),
                  _const_spec(lbl.shape), _const_spec((1, 512)),
                  pl.BlockSpec((1, HEADS, HEAD_W, HEAD_W), s_map)],
        out_specs=[pl.BlockSpec((1, rows, 512), lambda b, c: (b, c, 0)),
                   pl.BlockSpec((1, HEADS, HEAD_W, HEAD_W), lambda b, c: (b, 0, 0, 0))],
        out_shape=[jax.ShapeDtypeStruct((nseq, t, 512), out_dtype),
                   jax.ShapeDtypeStruct((nseq, HEADS, HEAD_W, HEAD_W), F32)],
        scratch_shapes=[pltpu.VMEM((HEADS, HEAD_W, HEAD_W), F32)],
        compiler_params=pltpu.CompilerParams(dimension_semantics=("parallel", "arbitrary"),
                                             vmem_limit_bytes=VMEM_LIMIT),
        name="hgrn",
    )(zh, lbl, onw, s0)


def _lambda_value(lp, lam_init):
    s1 = jnp.sum(lp[0:1] * lp[1:2], axis=1, keepdims=True)
    s2 = jnp.sum(lp[2:3] * lp[3:4], axis=1, keepdims=True)
    return jnp.exp(s1) - jnp.exp(s2) + lam_init


def _softmax_update(s, v_bf, m_ref, l_ref, acc_ref, first):
    m_cur = jnp.max(s, axis=1, keepdims=True)
    if first:
        m_new = m_cur
        p = jnp.exp2(s - m_new)
        l_ref[...] = jnp.sum(p, axis=1, keepdims=True)
        acc_ref[...] = _dot(p.astype(BF16), v_bf)
    else:
        m_prev = m_ref[...]
        m_new = jnp.maximum(m_prev, m_cur)
        alpha = jnp.exp2(m_prev - m_new)
        p = jnp.exp2(s - m_new)
        l_ref[...] = alpha * l_ref[...] + jnp.sum(p, axis=1, keepdims=True)
        acc_ref[...] = alpha * acc_ref[...] + _dot(p.astype(BF16), v_bf)
    m_ref[...] = m_new


def _pattn_kernel(qt_ref, k_ref, vt_ref, km_ref, vmt_ref, tb_ref, tm_ref, lam_ref, onw_ref, o_ref,
                  qz_scr, m_scr, acc_scr, *, lam_init):
    blk = ATT_BLOCK
    i = pl.program_id(1)
    sub = lax.broadcasted_iota(jnp.int32, (HEAD_W, blk), 0)
    for h in range(HEADS):
        qh = qt_ref[0, h * HEAD_W:(h + 1) * HEAD_W, :]
        zero = jnp.zeros_like(qh)
        qz_scr[h] = jnp.concatenate([jnp.where(sub < DA_DH, qh, zero),
                                     jnp.where(sub >= DA_DH, qh, zero)], axis=1)

    hv = lambda h: slice(h * (HEAD_W + VT_PAD), (h + 1) * (HEAD_W + VT_PAD))

    meta_sel = jnp.minimum(i, 1)
    META = "meta"

    def key_blocks(blocks):
        def qk(j, table, h):
            hs = slice(h * HEAD_W, (h + 1) * HEAD_W)
            if j is META:
                return _dot(km_ref[:, hs], qz_scr[h]) + tm_ref[meta_sel, h]
            off = pl.multiple_of(j * blk, blk)
            s = _dot(k_ref[0, pl.ds(off, blk), hs], qz_scr[h])
            return s if table is None else s + tb_ref[table, h]

        def soft(j, h, s):
            m_cur = jnp.max(s, axis=0, keepdims=True)
            if j is META:
                m_scr[h] = m_cur
                return jnp.exp2(s - m_cur).astype(BF16), None
            m_prev = m_scr[h]
            m_new = jnp.maximum(m_prev, m_cur)
            m_scr[h] = m_new
            return jnp.exp2(s - m_new).astype(BF16), jnp.exp2(m_prev - m_new)

        def pv(j, h, p, alpha):
            if j is META:
                acc_scr[h] = _dot(vmt_ref[hv(h), :], p)
            else:
                acc_scr[h] = alpha * acc_scr[h] + _dot(vt_ref[j, hv(h), :], p)

        stages = [(j, table, h) for j, table in blocks for h in range(HEADS)]
        s_cur = qk(*stages[0])
        pend = None
        for n, (j, table, h) in enumerate(stages):
            s_next = qk(*stages[n + 1]) if n + 1 < len(stages) else None
            if pend is not None:
                pv(stages[n - 1][0], stages[n - 1][2], *pend)
            pend = soft(j, h, s_cur)
            s_cur = s_next
        pv(stages[-1][0], stages[-1][2], *pend)

    @pl.when(i >= 1)
    def _():
        key_blocks([(META, None), (i - 1, 1), (i, 0)])

    @pl.when(i == 0)
    def _():
        key_blocks([(META, None), (0, 0)])

    n_far = jnp.maximum(i - 1, 0)
    done = 0
    width = 1
    while width < FAR_UNROLL:
        start = done

        @pl.when((n_far & width) != 0)
        def _(start=start, width=width):
            key_blocks([(start + u, None) for u in range(width)])

        done = done + (n_far & width)
        width *= 2
    n_rem = done

    def far_body(t, carry):
        j = n_rem + FAR_UNROLL * t
        key_blocks([(j + u, None) for u in range(FAR_UNROLL)])
        return carry

    lax.fori_loop(0, n_far // FAR_UNROLL, far_body, 0)

    lam = _lambda_value(lam_ref[...], lam_init)
    onw = onw_ref[...]
    for h in range(HEADS):
        acc = acc_scr[h]
        a = acc[:HEAD_W] / acc[HEAD_W:HEAD_W + 1]
        ot = a[:, :blk] - lam * a[:, blk:]
        ot = ot * lax.rsqrt(jnp.mean(ot * ot, axis=0, keepdims=True) + EPS) * onw * (1.0 - lam_init)
        o_ref[0, :, h * HEAD_W:(h + 1) * HEAD_W] = ot.T.astype(o_ref.dtype)


def _pattn(qt, k, vt, km, vmt, tb, tmeta, lam_p, onw_col, lam_init):
    bsz, t, w = k.shape
    blk = ATT_BLOCK
    nblk = t // blk
    assert t % blk == 0
    return pl.pallas_call(
        functools.partial(_pattn_kernel, lam_init=lam_init),
        grid=(bsz, nblk),
        in_specs=[pl.BlockSpec((1, w, blk), lambda b, i: (b * nblk + i, 0, 0)),
                  pl.BlockSpec((1, t, w), lambda b, i: (b, 0, 0)),
                  pl.BlockSpec((nblk, VT_ROWS, blk), lambda b, i: (b, 0, 0)),
                  _const_spec(km.shape), _const_spec(vmt.shape), _const_spec(tb.shape),
                  _const_spec(tmeta.shape), _const_spec(lam_p.shape), _const_spec(onw_col.shape)],
        out_specs=pl.BlockSpec((1, blk, w), lambda b, i: (b, i, 0)),
        out_shape=jax.ShapeDtypeStruct((bsz, t, w), BF16),
        scratch_shapes=[pltpu.VMEM((HEADS, HEAD_W, 2 * blk), BF16),
                        pltpu.VMEM((HEADS, 1, 2 * blk), F32),
                        pltpu.VMEM((HEADS, HEAD_W + VT_PAD, 2 * blk), F32)],
        compiler_params=pltpu.CompilerParams(dimension_semantics=("parallel", "arbitrary"),
                                             vmem_limit_bytes=VMEM_LIMIT),
        name="prompt_attn",
    )(qt, k, vt, km, vmt, tb, tmeta, lam_p, onw_col)


def _sattn_kernel(pt_ref, q_ref, kn_ref, vn_ref, tl_ref, tn_ref, tmask_ref, lam_ref, onw_ref,
                  ck_hbm, cv_hbm, o_ref, kbuf, vbuf, sem, m_scr, l_scr, acc_scr,
                  *, lam_init, n_tok, n_pages):
    pg = PAGES_PER_STEP
    prow = kbuf.shape[2]
    b, g = pl.program_id(0), pl.program_id(1)
    n_steps = pl.num_programs(1)
    t = b * n_steps + g
    slot = t % 2

    def page_copies(step, to_slot):
        base = (step // n_steps) * n_pages + (step % n_steps) * pg
        copies = []
        for u in range(pg):
            rows = pl.ds(pl.multiple_of(pt_ref[base + u] * prow, prow), prow)
            copies.append(pltpu.make_async_copy(ck_hbm.at[rows, :], kbuf.at[to_slot, u],
                                                sem.at[to_slot, 0]))
            copies.append(pltpu.make_async_copy(cv_hbm.at[rows, :], vbuf.at[to_slot, u],
                                                sem.at[to_slot, 1]))
        return copies

    @pl.when(t == 0)
    def _():
        for cp in page_copies(t, slot):
            cp.start()

    @pl.when(t + 1 < pl.num_programs(0) * n_steps)
    def _():
        for cp in page_copies(t + 1, 1 - slot):
            cp.start()

    q = q_ref[0]

    @pl.when(g == 0)
    def _():
        s = _dot_nt(q, kn_ref[0]) + tn_ref[...]
        _softmax_update(s, vn_ref[0], m_scr, l_scr, acc_scr, True)

    for cp in page_copies(t, slot):
        cp.wait()

    is_last = g == n_steps - 1
    tmask = tmask_ref[...]
    ss = []
    for u in range(pg):
        s = _dot_nt(q, kbuf[slot, u].astype(BF16))
        ss.append(s + (jnp.where(is_last, tl_ref[...], tmask) if u == pg - 1 else tmask))
    m_prev = m_scr[...]
    m_new = m_prev
    for s in ss:
        m_new = jnp.maximum(m_new, jnp.max(s, axis=1, keepdims=True))
    alpha = jnp.exp2(m_prev - m_new)
    l_new = alpha * l_scr[...]
    acc = alpha * acc_scr[...]
    for u in range(pg):
        p = jnp.exp2(ss[u] - m_new)
        l_new = l_new + jnp.sum(p, axis=1, keepdims=True)
        acc = acc + _dot(p.astype(BF16), vbuf[slot, u].astype(BF16))
    m_scr[...] = m_new
    l_scr[...] = l_new
    acc_scr[...] = acc

    @pl.when(is_last)
    def _():
        lam = _lambda_value(lam_ref[...], lam_init)
        onw = onw_ref[...]
        a = acc_scr[...] / l_scr[...]
        for h in range(HEADS):
            r1 = (2 * h) * n_tok
            r2 = (2 * h + 1) * n_tok
            o = a[r1:r1 + n_tok] - lam * a[r2:r2 + n_tok]
            o_ref[0, :, h * HEAD_W:(h + 1) * HEAD_W] = (
                _rms_rows(o, onw) * (1.0 - lam_init)).astype(o_ref.dtype)


def _sattn(page_table, qall, knew, vnew, tlast, tnew, tmask, lam_p, onw, ck2, cv2, lam_init, n_tok):
    nb, n_pages = page_table.shape
    pg = PAGES_PER_STEP
    assert n_pages % pg == 0
    nrow = qall.shape[1]
    prow = knew.shape[1]
    pt_flat = page_table.reshape(-1)
    per_b = lambda shape: pl.BlockSpec((1,) + shape, lambda b, g, pt: (b, 0, 0))
    const = lambda shape: pl.BlockSpec(shape, lambda b, g, pt: (0,) * len(shape))
    hbm = pl.BlockSpec(memory_space=pl.ANY)
    grid_spec = pltpu.PrefetchScalarGridSpec(
        num_scalar_prefetch=1,
        grid=(nb, n_pages // pg),
        in_specs=[per_b((nrow, HEAD_W)), per_b((prow, HEAD_W)), per_b((prow, HEAD_W)),
                  const(tlast.shape), const(tnew.shape), const(tmask.shape),
                  const(lam_p.shape), const(onw.shape), hbm, hbm],
        out_specs=per_b((n_tok, HEADS * HEAD_W)),
        scratch_shapes=[pltpu.VMEM((2, pg, prow, HEAD_W), F32), pltpu.VMEM((2, pg, prow, HEAD_W), F32),
                        pltpu.SemaphoreType.DMA((2, 2)),
                        pltpu.VMEM((nrow, 1), F32), pltpu.VMEM((nrow, 1), F32),
                        pltpu.VMEM((nrow, HEAD_W), F32)],
    )
    return pl.pallas_call(
        functools.partial(_sattn_kernel, lam_init=lam_init, n_tok=n_tok, n_pages=n_pages),
        grid_spec=grid_spec,
        out_shape=jax.ShapeDtypeStruct((nb, n_tok, HEADS * HEAD_W), F32),
        compiler_params=pltpu.CompilerParams(dimension_semantics=("arbitrary", "arbitrary"),
                                             vmem_limit_bytes=VMEM_LIMIT),
        name="sample_attn",
    )(pt_flat, qall, knew, vnew, tlast, tnew, tmask, lam_p, onw, ck2, cv2)


def _merge_kernel(x_ref, oa_ref, ob_ref, n1_ref, wg_ref, wha_ref, wda_ref, wo_ref, h_ref):
    x = x_ref[...]
    d = x.shape[1]
    xn = _rms_rows(x, n1_ref[...]).astype(BF16)
    gates = _dot(xn, wg_ref[...])
    ya = _dot(oa_ref[...], wha_ref[...])
    yb = _dot(ob_ref[...], wda_ref[...])
    merged = _sigmoid(gates[:, :d]) * ya + _sigmoid(gates[:, d:]) * yb
    h_ref[...] = x + _dot(merged.astype(BF16), wo_ref[...])


def _merge(x2d, oa, ob, n1, w_in_bf, w_ha, w_da, w_o, tm):
    n, d = x2d.shape
    row = lambda width: pl.BlockSpec((tm, width), lambda i: (i, 0))
    gate_spec = pl.BlockSpec((pl.Element(d), pl.Element(2 * d)), lambda i: (0, IN_A_COLS))
    return pl.pallas_call(
        _merge_kernel,
        grid=(n // tm,),
        in_specs=[row(d), row(512), row(512), _const_spec((1, d)), gate_spec,
                  _const_spec(w_ha.shape), _const_spec(w_da.shape), _const_spec(w_o.shape)],
        out_specs=row(d),
        out_shape=jax.ShapeDtypeStruct((n, d), F32),
        compiler_params=pltpu.CompilerParams(dimension_semantics=("parallel",),
                                             vmem_limit_bytes=VMEM_LIMIT),
        name="merge",
    )(x2d, oa, ob, n1, w_in_bf, w_ha, w_da, w_o)


def _moe_kernel(h_ref, n2_ref, wr_hi_ref, wr_lo_ref, br_ref, wg_ref, wu_ref, wd_ref, y_ref):
    h = h_ref[...]
    hn = _rms_rows(h, n2_ref[...])
    hn_hi = hn.astype(BF16)
    hn_lo = (hn - hn_hi.astype(F32)).astype(BF16)
    logits = (_dot(hn_hi, wr_hi_ref[...]) + _dot(hn_lo, wr_hi_ref[...])
              + _dot(hn_hi, wr_lo_ref[...]) + br_ref[...])
    lane = lax.broadcasted_iota(jnp.int32, logits.shape, 1)
    big = jnp.int32(1 << 20)
    ninf = -jnp.inf

    def first_lane(mask):
        return jnp.min(jnp.where(mask, lane, big), axis=1, keepdims=True)

    is_g = lane < N_GROUPS
    g_max = jnp.max(jnp.where(is_g, logits, ninf), axis=1, keepdims=True)
    g_idx = first_lane(is_g & (logits == g_max))
    g_sum = jnp.sum(jnp.where(is_g, jnp.exp(logits - g_max), 0.0), axis=1, keepdims=True)
    p_g = 1.0 / g_sum
    e_lane = lane - N_GROUPS
    in_group = ((e_lane >= 0) & (e_lane < N_EXPERTS)
                & (lax.shift_right_arithmetic(e_lane, 2) == g_idx))
    e_max = jnp.max(jnp.where(in_group, logits, ninf), axis=1, keepdims=True)
    e_exp = jnp.where(in_group, jnp.exp(logits - e_max), 0.0)
    e_prob = e_exp / jnp.sum(e_exp, axis=1, keepdims=True)
    p1 = jnp.max(jnp.where(in_group, e_prob, -1.0), axis=1, keepdims=True)
    i1 = first_lane(in_group & (e_prob == p1))
    rest = in_group & (lane != i1)
    p2 = jnp.max(jnp.where(rest, e_prob, -1.0), axis=1, keepdims=True)
    i2 = first_lane(rest & (e_prob == p2))
    denom = p1 + p2
    combine = (jnp.where(lane == i1, p_g * p1 / denom, 0.0)
               + jnp.where(lane == i2, p_g * p2 / denom, 0.0))

    acc = jnp.zeros_like(h)
    for ex in range(N_EXPERTS):
        gt = _dot(hn_hi, wg_ref[ex])
        up = _dot(hn_hi, wu_ref[ex])
        hh = _silu(gt) * up * combine[:, N_GROUPS + ex:N_GROUPS + ex + 1]
        acc = acc + _dot(hh.astype(BF16), wd_ref[ex])
    y_ref[...] = h + acc


def _moe(h2d, n2, wr_hi, wr_lo, br, wg, wu, wd, tm):
    n, d = h2d.shape
    row = pl.BlockSpec((tm, d), lambda i: (i, 0))
    single = lambda shape: pl.BlockSpec(shape, lambda i: (0,) * len(shape),
                                        pipeline_mode=pl.Buffered(1))
    return pl.pallas_call(
        _moe_kernel,
        grid=(n // tm,),
        in_specs=[row, _const_spec((1, d)), _const_spec(wr_hi.shape), _const_spec(wr_lo.shape),
                  _const_spec(br.shape), single(wg.shape), single(wu.shape), single(wd.shape)],
        out_specs=row,
        out_shape=jax.ShapeDtypeStruct((n, d), F32),
        compiler_params=pltpu.CompilerParams(dimension_semantics=("parallel",),
                                             vmem_limit_bytes=VMEM_LIMIT),
        name="moe",
    )(h2d, n2, wr_hi, wr_lo, br, wg, wu, wd)


def _t5_bias(dist, rel_bias):
    n = jnp.maximum(dist, 0)
    max_exact = N_BUCKETS // 2
    nf = jnp.maximum(n, 1).astype(F32)
    scaled = (jnp.log(nf / max_exact) / math.log(MAX_DISTANCE / max_exact)
              * (N_BUCKETS - max_exact))
    large = max_exact + jnp.floor(jnp.maximum(scaled, 0.0)).astype(jnp.int32)
    large = jnp.minimum(large, N_BUCKETS - 1)
    bucket = jnp.where(n < max_exact, n, large)
    rb = rel_bias.astype(F32)
    onehot = (bucket[..., None] == jnp.arange(N_BUCKETS)).astype(F32)
    bias = jnp.moveaxis(jnp.dot(onehot, rb, precision=lax.Precision.HIGHEST), -1, 0)
    far = rb[N_BUCKETS - 1].reshape((HEADS,) + (1,) * dist.ndim)
    return jnp.where(dist >= 0, (bias - far) * LOG2E, MASK_VALUE)


def kernel(x_prompt, x_sample, cache_k, cache_v, state_hgrn, page_table, meta_tokens, rel_bias,
           hg_lb_logits, norm1_w, w_in, hg_onorm_w, w_hg_out, q_norm_w, k_norm_w, da_lambda,
           da_onorm_w, w_da_out, w_o, norm2_w, w_router_group, b_router_group,
           w_router_expert, b_router_expert, w_gate, w_up, w_down):
    bsz, seq, d = x_prompt.shape
    nb, n_tok, _ = x_sample.shape
    depth = w_in.shape[0]
    assert depth == 1 and hg_lb_logits.shape[0] == 2
    n_pages = page_table.shape[1]
    psize = cache_k.shape[2]
    past_len = n_pages * psize
    blk = ATT_BLOCK
    assert blk >= MAX_DISTANCE and psize >= MAX_DISTANCE and n_tok <= 8
    lam_init = 0.8 - 0.6 * math.exp(-0.3 * 0)
    w_attn = HEADS * HEAD_W

    assert w_in.shape[2] == IN_A_COLS + 2 * d
    w_in_bf = w_in[0].astype(BF16)
    n1 = norm1_w[0].reshape(1, d)
    n2 = norm2_w[0].reshape(1, d)
    qw = (jnp.tile(q_norm_w[0], 2 * HEADS) * (DA_DH ** -0.5 * LOG2E)).reshape(1, w_attn)
    kw = jnp.tile(k_norm_w[0], 2 * HEADS).reshape(1, w_attn)
    gi = jnp.arange(w_attn) // DA_DH
    gmat = jnp.where(gi[:, None] == gi[None, :], 1.0 / DA_DH, 0.0).astype(BF16)
    hg_onw = jnp.tile(hg_onorm_w[0], HEADS).reshape(1, w_attn)
    da_onw = da_onorm_w[0].reshape(1, HEAD_W)
    lam_p = da_lambda[0].astype(F32)
    w_ha = w_hg_out[0].astype(BF16)
    w_da = w_da_out[0].astype(BF16)
    w_o0 = w_o[0].astype(BF16)
    wr = jnp.concatenate([w_router_group[0], w_router_expert[0],
                          jnp.zeros((d, 128 - N_GROUPS - N_EXPERTS), F32)], axis=1)
    wr_hi = wr.astype(BF16)
    wr_lo = (wr - wr_hi.astype(F32)).astype(BF16)
    br = jnp.concatenate([b_router_group[0], b_router_expert[0],
                          jnp.zeros((128 - N_GROUPS - N_EXPERTS,), F32)]).reshape(1, 128)
    wg = w_gate[0].astype(BF16)
    wu = w_up[0].astype(BF16)
    wd = w_down[0].astype(BF16)

    n_small = N_META + nb * n_tok
    x_small = jnp.concatenate([meta_tokens.astype(F32), x_sample.reshape(nb * n_tok, d)], axis=0)
    zh_s, q_s, kf_s, kb_s, vf_s, vb_s = _inproj_small(x_small, n1, w_in_bf, gmat, qw, kw)

    _, s_meta = _hgrn(zh_s[:N_META][None], hg_lb_logits, hg_onw,
                      jnp.zeros((1, HEADS, HEAD_W, HEAD_W), F32), shared_state=False, out_dtype=F32)
    oa_s, s_sample = _hgrn(zh_s[N_META:].reshape(nb, n_tok, 2048), hg_lb_logits, hg_onw,
                           state_hgrn[0], shared_state=False, out_dtype=F32)

    x_p = x_prompt.reshape(bsz * seq, d)
    def cache_base(meta_rows):
        meta_b = jnp.broadcast_to(meta_rows.reshape(1, N_META * HEADS, HEAD_W),
                                  (bsz, N_META * HEADS, HEAD_W))
        return jnp.concatenate([meta_b, jnp.zeros((bsz, seq * HEADS, HEAD_W), F32)],
                               axis=1).reshape(-1, HEAD_W)

    zh_p, qt_p, kf_p, kb_p, vf_p, vt_p = _inproj_prompt(
        x_p, n1, w_in_bf, gmat, qw, kw, cache_base(kf_s[:N_META]), cache_base(vf_s[:N_META]),
        seq, 512)
    oa_p, s_prompt = _hgrn(zh_p.reshape(bsz, seq, 2048), hg_lb_logits, hg_onw, s_meta,
                           shared_state=True, out_dtype=BF16)

    r = jnp.arange(blk)
    tb = jnp.stack([_t5_bias(r[None, :] - r[:, None], rel_bias),
                    _t5_bias(blk + r[None, :] - r[:, None], rel_bias)])
    tb = jnp.concatenate([tb, tb], axis=3)
    mrow = jnp.arange(128)
    t_meta0 = _t5_bias(N_META + r[None, :] - mrow[:, None], rel_bias)
    t_meta = jnp.stack([t_meta0, jnp.zeros_like(t_meta0)])
    t_meta = jnp.where(mrow[:, None] < N_META, t_meta, MASK_VALUE)
    t_meta = jnp.concatenate([t_meta, t_meta], axis=3)
    pad_meta = lambda a: jnp.concatenate([a, jnp.zeros((128 - N_META, w_attn), a.dtype)], axis=0)
    vmt = pad_meta(vb_s[:N_META]).T.reshape(HEADS, HEAD_W, 128)
    vmt = jnp.concatenate([vmt, jnp.ones((HEADS, VT_PAD, 128), BF16)], axis=1).reshape(VT_ROWS, 128)
    ob_p = _pattn(qt_p, kb_p.reshape(bsz, seq, w_attn), vt_p, pad_meta(kb_s[:N_META]),
                  vmt, tb, t_meta, lam_p, da_onw.reshape(HEAD_W, 1), lam_init)

    q_tok = q_s[N_META:].reshape(nb, n_tok, HEADS, 1, 2, DA_DH)
    q_maps = q_tok * jnp.eye(2, dtype=BF16).reshape(1, 1, 1, 2, 2, 1)
    n_rows = 2 * HEADS * n_tok
    qall = jnp.transpose(q_maps, (0, 2, 3, 1, 4, 5)).reshape(nb, n_rows, HEAD_W)
    prow = psize * HEADS
    pad_new = lambda a: jnp.concatenate(
        [a.reshape(nb, n_tok * HEADS, HEAD_W),
         jnp.zeros((nb, prow - n_tok * HEADS, HEAD_W), a.dtype)], axis=1)
    s_idx = jnp.tile(jnp.arange(n_tok), 2 * HEADS)
    head_of_row = jnp.repeat(jnp.arange(HEADS), 2 * n_tok)
    key_r = jnp.arange(prow) // HEADS
    own_head = head_of_row[:, None] == (jnp.arange(prow) % HEADS)[None, :]
    dist_last = (past_len + s_idx[:, None]) - (past_len - psize + key_r[None, :])
    dist_new = jnp.where(key_r[None, :] < n_tok, s_idx[:, None] - key_r[None, :], -1)
    pick = lambda t: jnp.where(own_head, t[head_of_row, jnp.arange(n_rows)], MASK_VALUE)
    t_last = pick(_t5_bias(dist_last, rel_bias))
    t_new = pick(_t5_bias(dist_new, rel_bias))
    t_mask = jnp.where(own_head, 0.0, MASK_VALUE).astype(F32)
    ob_s = _sattn(page_table, qall, pad_new(kb_s[N_META:]), pad_new(vb_s[N_META:]), t_last, t_new,
                  t_mask, lam_p, da_onw, cache_k.reshape(-1, HEAD_W), cache_v.reshape(-1, HEAD_W),
                  lam_init, n_tok)

    h_p = _merge(x_p, oa_p.reshape(bsz * seq, w_attn), ob_p.reshape(bsz * seq, w_attn),
                 n1, w_in_bf, w_ha, w_da, w_o0, 512)
    y_p = _moe(h_p, n2, wr_hi, wr_lo, br, wg, wu, wd, 512)
    x_s = x_small[N_META:]
    h_s = _merge(x_s, oa_s.reshape(nb * n_tok, w_attn).astype(BF16),
                 ob_s.reshape(nb * n_tok, w_attn).astype(BF16), n1, w_in_bf, w_ha, w_da, w_o0,
                 nb * n_tok)
    y_s = _moe(h_s, n2, wr_hi, wr_lo, br, wg, wu, wd, nb * n_tok)

    return (y_p.reshape(bsz, seq, d),
            y_s.reshape(nb, n_tok, d),
            kf_p.reshape(1, bsz, seq + N_META, HEADS, HEAD_W),
            vf_p.reshape(1, bsz, seq + N_META, HEADS, HEAD_W),
            s_prompt[None],
            kf_s[N_META:].reshape(1, nb, n_tok, HEADS, HEAD_W),
            vf_s[N_META:].reshape(1, nb, n_tok, HEADS, HEAD_W),
            s_sample[None])
```

```python
import functools
import math

import jax
import jax.numpy as jnp
from jax import lax
from jax.experimental import pallas as pl
from jax.experimental.pallas import tpu as pltpu

F32 = jnp.float32
BF16 = jnp.bfloat16

EPS = 1e-6
N_META = 16
HEADS = 4
HEAD_W = 128
DA_DH = 64
N_BUCKETS = 32
MAX_DISTANCE = 128
N_GROUPS = 4
EXPERTS_PER_GROUP = 4
N_EXPERTS = N_GROUPS * EXPERTS_PER_GROUP
MASK_VALUE = -1e30
HG_CHUNK = 128
HG_STEP_CHUNKS = 4
HG_BASE = 8
ATT_BLOCK = 256
VT_PAD = 16
VT_ROWS = HEADS * (HEAD_W + VT_PAD)
FAR_UNROLL = 8
PAGES_PER_STEP = 16
IN_A_COLS = 3584
LOG2E = math.log2(math.e)
VMEM_LIMIT = 56 * 1024 * 1024


def _dot(a, b):
    return jnp.dot(a, b, preferred_element_type=F32)


def _dot_nt(a, b):
    return lax.dot_general(a, b, (((1,), (1,)), ((), ())), preferred_element_type=F32)


def _dot_tn(a, b):
    return lax.dot_general(a, b, (((0,), (0,)), ((), ())), preferred_element_type=F32)


def _sigmoid(x):
    return 0.5 + 0.5 * jnp.tanh(0.5 * x)


def _silu(x):
    y = 0.5 * x
    return y + y * jnp.tanh(y)


def _rms_rows(x, w):
    return x * lax.rsqrt(jnp.mean(x * x, axis=-1, keepdims=True) + EPS) * w


def _const_spec(shape):
    zeros = (0,) * len(shape)
    return pl.BlockSpec(shape, lambda *_: zeros)


def _inproj_values(x_ref, n1_ref, w_ref, g_ref, qw_ref, kw_ref):
    xn = _rms_rows(x_ref[...], n1_ref[...]).astype(BF16)
    g = g_ref[...]

    def group_norm(z, w):
        sq = z * z
        hi = sq.astype(BF16)
        lo = (sq - hi.astype(F32)).astype(BF16)
        ms = _dot(hi, g) + _dot(lo, g)
        return z * lax.rsqrt(ms + EPS) * w

    zh = _dot(xn, w_ref[:, 0:2048])
    qn = group_norm(_dot(xn, w_ref[:, 2048:2560]), qw_ref[...])
    kn = group_norm(_dot(xn, w_ref[:, 2560:3072]), kw_ref[...])
    zv = _dot(xn, w_ref[:, 3072:3584])
    return zh, qn, kn, zv


def _inproj_small_kernel(x_ref, n1_ref, w_ref, g_ref, qw_ref, kw_ref,
                         zh_ref, q_ref, kf_ref, kb_ref, vf_ref, vb_ref):
    zh, qn, kn, zv = _inproj_values(x_ref, n1_ref, w_ref, g_ref, qw_ref, kw_ref)
    zh_ref[...] = zh
    q_ref[...] = qn.astype(BF16)
    kf_ref[...] = kn
    kb_ref[...] = kn.astype(BF16)
    vf_ref[...] = zv
    vb_ref[...] = zv.astype(BF16)


def _inproj_prompt_kernel(x_ref, n1_ref, w_ref, g_ref, qw_ref, kw_ref, km_ref, vm_ref,
                          zh_ref, qt_ref, kf_hbm, kb_ref, vf_hbm, vt_ref,
                          kst, vst, sem, msem, *, seq):
    zh, qn, kn, zv = _inproj_values(x_ref, n1_ref, w_ref, g_ref, qw_ref, kw_ref)
    tm = zh.shape[0]
    tiles = seq // tm
    i = pl.program_id(0)
    n_steps = pl.num_programs(0)
    slot = i % 2

    def batch_row0(step):
        return (step // tiles) * (N_META + seq) * HEADS

    def tile_copies(step, from_slot):
        row0 = batch_row0(step) + (N_META + (step % tiles) * tm) * HEADS
        rows = pl.ds(pl.multiple_of(row0, 8 * HEADS), tm * HEADS)
        return [pltpu.make_async_copy(kst.at[from_slot], kf_hbm.at[rows, :], sem.at[from_slot, 0]),
                pltpu.make_async_copy(vst.at[from_slot], vf_hbm.at[rows, :], sem.at[from_slot, 1])]

    zh_ref[...] = zh
    kb_ref[...] = kn.astype(BF16)
    qt = qn.T
    vt = zv.T
    for c in range(tm // ATT_BLOCK):
        cols = slice(c * ATT_BLOCK, (c + 1) * ATT_BLOCK)
        qt_ref[c] = qt[:, cols].astype(BF16)
        for h in range(HEADS):
            r0 = h * (HEAD_W + VT_PAD)
            vt_ref[c, r0:r0 + HEAD_W, :] = vt[h * HEAD_W:(h + 1) * HEAD_W, cols].astype(BF16)
            vt_ref[c, r0 + HEAD_W:r0 + HEAD_W + VT_PAD, :] = jnp.ones((VT_PAD, ATT_BLOCK), BF16)

    @pl.when(i >= 2)
    def _():
        for cp in tile_copies(i - 2, slot):
            cp.wait()

    for h in range(HEADS):
        hs = slice(h * HEAD_W, (h + 1) * HEAD_W)
        kst[slot, pl.ds(h, tm, stride=HEADS), :] = kn[:, hs]
        vst[slot, pl.ds(h, tm, stride=HEADS), :] = zv[:, hs]
    for cp in tile_copies(i, slot):
        cp.start()

    @pl.when(i % tiles == 0)
    def _():
        rows = pl.ds(pl.multiple_of(batch_row0(i), 8 * HEADS), N_META * HEADS)
        meta = [pltpu.make_async_copy(km_ref, kf_hbm.at[rows, :], msem.at[0]),
                pltpu.make_async_copy(vm_ref, vf_hbm.at[rows, :], msem.at[1])]
        for cp in meta:
            cp.start()
        for cp in meta:
            cp.wait()

    @pl.when(i == n_steps - 1)
    def _():
        @pl.when(i >= 1)
        def _():
            for cp in tile_copies(i - 1, 1 - slot):
                cp.wait()
        for cp in tile_copies(i, slot):
            cp.wait()


def _w_in_spec(d):
    return pl.BlockSpec((d, IN_A_COLS), lambda i: (0, 0))


def _inproj_small(x2d, n1, w_in_bf, gmat, qw, kw):
    n, d = x2d.shape
    full = lambda width: _const_spec((n, width))
    return pl.pallas_call(
        _inproj_small_kernel,
        grid=(1,),
        in_specs=[full(d), _const_spec((1, d)), _w_in_spec(d), _const_spec(gmat.shape),
                  _const_spec((1, 512)), _const_spec((1, 512))],
        out_specs=[full(2048), full(512), full(512), full(512), full(512), full(512)],
        out_shape=[jax.ShapeDtypeStruct((n, 2048), F32), jax.ShapeDtypeStruct((n, 512), BF16),
                   jax.ShapeDtypeStruct((n, 512), F32), jax.ShapeDtypeStruct((n, 512), BF16),
                   jax.ShapeDtypeStruct((n, 512), F32), jax.ShapeDtypeStruct((n, 512), BF16)],
        compiler_params=pltpu.CompilerParams(dimension_semantics=("arbitrary",),
                                             vmem_limit_bytes=VMEM_LIMIT),
        name="inproj_small",
    )(x2d, n1, w_in_bf, gmat, qw, kw)


def _inproj_prompt(x2d, n1, w_in_bf, gmat, qw, kw, k_meta, v_meta, seq, tm):
    n, d = x2d.shape
    cache_shape = jax.ShapeDtypeStruct(((n // seq) * (N_META + seq) * HEADS, HEAD_W), F32)
    assert seq % tm == 0 and tm % ATT_BLOCK == 0
    per = tm // ATT_BLOCK
    row = lambda width: pl.BlockSpec((tm, width), lambda i: (i, 0))
    t_spec = lambda rows: pl.BlockSpec((per, rows, ATT_BLOCK), lambda i: (i, 0, 0))
    t_shape = lambda rows: jax.ShapeDtypeStruct((n // ATT_BLOCK, rows, ATT_BLOCK), BF16)
    hbm = pl.BlockSpec(memory_space=pl.ANY)
    return pl.pallas_call(
        functools.partial(_inproj_prompt_kernel, seq=seq),
        grid=(n // tm,),
        in_specs=[row(d), _const_spec((1, d)), _w_in_spec(d), _const_spec(gmat.shape),
                  _const_spec((1, 512)), _const_spec((1, 512)),
                  _const_spec(k_meta.shape), _const_spec(v_meta.shape)],
        out_specs=[row(2048), t_spec(512), hbm, row(512), hbm, t_spec(VT_ROWS)],
        out_shape=[jax.ShapeDtypeStruct((n, 2048), F32), t_shape(512), cache_shape,
                   jax.ShapeDtypeStruct((n, 512), BF16), cache_shape, t_shape(VT_ROWS)],
        scratch_shapes=[pltpu.VMEM((2, tm * HEADS, HEAD_W), F32),
                        pltpu.VMEM((2, tm * HEADS, HEAD_W), F32),
                        pltpu.SemaphoreType.DMA((2, 2)), pltpu.SemaphoreType.DMA((2,))],
        compiler_params=pltpu.CompilerParams(dimension_semantics=("arbitrary",),
                                             vmem_limit_bytes=VMEM_LIMIT),
        name="inproj",
    )(x2d, n1, w_in_bf, gmat, qw, kw, k_meta, v_meta)


def _hgrn_kernel(z_ref, lbl_ref, onw_ref, s0_ref, o_ref, sout_ref, st_scr, *, rows):
    C = HG_CHUNK
    W = HEADS * HEAD_W
    n_sub = max(rows // C, 1)
    n_valid = min(rows, C)
    c = pl.program_id(1)
    heads = [slice(h * HEAD_W, (h + 1) * HEAD_W) for h in range(HEADS)]

    @pl.when(c == 0)
    def _():
        for h in range(HEADS):
            st_scr[h] = s0_ref[0, h].T

    lg = lbl_ref[...]
    e = jnp.exp(lg - jnp.max(lg, axis=0, keepdims=True))
    lb = e[0:1] / jnp.sum(e, axis=0, keepdims=True)
    half = 0.5 * (1.0 - lb)
    onw = onw_ref[...]
    row = lax.broadcasted_iota(jnp.int32, (C, C), 0)
    col = lax.broadcasted_iota(jnp.int32, (C, C), 1)
    tri = jnp.where(col <= row, 1.0, 0.0).astype(BF16)

    def shr(x, width):
        return lax.shift_right_logical(x, int(math.log2(width)))

    def front(z, n_valid):
        hq, hf, hi, hg = z[:, 0:W], z[:, W:2 * W], z[:, 2 * W:3 * W], z[:, 3 * W:4 * W]
        q = _silu(hq)
        f = (lb + half) + half * jnp.tanh(0.5 * hf)
        k = 1.0 - f
        logf = jnp.log2(f)
        if n_valid < C:
            valid = lax.broadcasted_iota(jnp.int32, (C, W), 0) < n_valid
            k = jnp.where(valid, k, 0.0)
            logf = jnp.where(valid, logf, 0.0)
        x1 = logf.astype(BF16)
        r1 = logf - x1.astype(F32)
        x2 = r1.astype(BF16)
        x3 = (r1 - x2.astype(F32)).astype(BF16)
        b = _dot(tri, x1) + _dot(tri, x2) + _dot(tri, x3)
        return q, k, hi.astype(BF16), b, _silu(hg)

    def intra(q, k, b):
        b_last = b[C - 1:C]

        def block_row(width, r):
            b3 = b.reshape(C // width, width, W)
            return jnp.broadcast_to(b3[:, r:r + 1, :], b3.shape).reshape(C, W)

        def next_block_first(width):
            b3 = b.reshape(C // width, width, W)
            nxt = jnp.concatenate([b3[1:, 0:1, :], b_last[None]], axis=0)
            return jnp.broadcast_to(nxt, b3.shape).reshape(C, W)

        mid = block_row(HG_BASE, HG_BASE // 2)
        qe = (q * jnp.exp2(b - mid)).astype(BF16)
        ke = (k * jnp.exp2(mid - b)).astype(BF16)
        mask = (shr(row, HG_BASE) == shr(col, HG_BASE)) & (col <= row)
        scores = [jnp.where(mask, _dot_nt(qe[:, hs], ke[:, hs]), 0.0) for hs in heads]
        width = HG_BASE
        while width < n_valid:
            qe = (q * jnp.exp2(b - block_row(width, 0))).astype(BF16)
            ke = (k * jnp.exp2(next_block_first(width) - b)).astype(BF16)
            mask = ((shr(row, width) == shr(col, width) + 1)
                    & (shr(row, 2 * width) == shr(col, 2 * width)))
            scores = [jnp.where(mask, _dot_nt(qe[:, hs], ke[:, hs]), sc)
                      for hs, sc in zip(heads, scores)]
            width *= 2
        qb = (q * jnp.exp2(b)).astype(BF16)
        kd = (k * jnp.exp2(b_last - b)).astype(BF16)
        return [sc.astype(BF16) for sc in scores], qb, kd, jnp.exp2(b_last)

    fronts = []
    for u in range(n_sub):
        z = z_ref[0, u * n_valid:(u + 1) * n_valid, :]
        if n_valid < C:
            z = jnp.concatenate([z, jnp.zeros((C - n_valid, z.shape[1]), F32)], axis=0)
        fronts.append(front(z, n_valid))
    intras = [intra(q, k, b) for q, k, _, b, _ in fronts]
    for u in range(n_sub):
        _, _, v_bf, _, gate = fronts[u]
        scores, qb, kd, decay = intras[u]
        outs = []
        for h, hs in enumerate(heads):
            st = st_scr[h]
            outs.append(_dot_nt(qb[:, hs], st.astype(BF16)) + _dot(scores[h], v_bf[:, hs]))
            st_scr[h] = st * decay[:, hs] + _dot_tn(v_bf[:, hs], kd[:, hs])
        for h, hs in enumerate(heads):
            on = _rms_rows(outs[h], onw[:, hs]) * gate[:, hs]
            o_ref[0, u * n_valid:(u + 1) * n_valid, hs] = on[:n_valid].astype(o_ref.dtype)

    @pl.when(c == pl.num_programs(1) - 1)
    def _():
        for h in range(HEADS):
            sout_ref[0, h] = st_scr[h].T


def _hgrn(zh, lbl, onw, s0, *, shared_state, out_dtype):
    nseq, t, _ = zh.shape
    rows = min(t, HG_STEP_CHUNKS * HG_CHUNK)
    assert t % rows == 0 and rows % 8 == 0 and (rows <= HG_CHUNK or rows % HG_CHUNK == 0)
    s_map = (lambda b, c: (0, 0, 0, 0)) if shared_state else (lambda b, c: (b, 0, 0, 0))
    return pl.pallas_call(
        functools.partial(_hgrn_kernel, rows=rows),
        grid=(nseq, t // rows),
        in_specs=[pl.BlockSpec((1, rows, 2048), lambda b, c: (b, c, 0)),
                  _const_spec(lbl.shape), _const_spec((1, 512)),
                  pl.BlockSpec((1, HEADS, HEAD_W, HEAD_W), s_map)],
        out_specs=[pl.BlockSpec((1, rows, 512), lambda b, c: (b, c, 0)),
                   pl.BlockSpec((1, HEADS, HEAD_W, HEAD_W), lambda b, c: (b, 0, 0, 0))],
        out_shape=[jax.ShapeDtypeStruct((nseq, t, 512), out_dtype),
                   jax.ShapeDtypeStruct((nseq, HEADS, HEAD_W, HEAD_W), F32)],
        scratch_shapes=[pltpu.VMEM((HEADS, HEAD_W, HEAD_W), F32)],
        compiler_params=pltpu.CompilerParams(dimension_semantics=("parallel", "arbitrary"),
                                             vmem_limit_bytes=VMEM_LIMIT),
        name="hgrn",
    )(zh, lbl, onw, s0)


def _lambda_value(lp, lam_init):
    s1 = jnp.sum(lp[0:1] * lp[1:2], axis=1, keepdims=True)
    s2 = jnp.sum(lp[2:3] * lp[3:4], axis=1, keepdims=True)
    return jnp.exp(s1) - jnp.exp(s2) + lam_init


def _softmax_update(s, v_bf, m_ref, l_ref, acc_ref, first):
    m_cur = jnp.max(s, axis=1, keepdims=True)
    if first:
        m_new = m_cur
        p = jnp.exp2(s - m_new)
        l_ref[...] = jnp.sum(p, axis=1, keepdims=True)
        acc_ref[...] = _dot(p.astype(BF16), v_bf)
    else:
        m_prev = m_ref[...]
        m_new = jnp.maximum(m_prev, m_cur)
        alpha = jnp.exp2(m_prev - m_new)
        p = jnp.exp2(s - m_new)
        l_ref[...] = alpha * l_ref[...] + jnp.sum(p, axis=1, keepdims=True)
        acc_ref[...] = alpha * acc_ref[...] + _dot(p.astype(BF16), v_bf)
    m_ref[...] = m_new


def _pattn_kernel(qt_ref, k_ref, vt_ref, km_ref, vmt_ref, tb_ref, tm_ref, lam_ref, onw_ref, o_ref,
                  qz_scr, m_scr, acc_scr, *, lam_init):
    blk = ATT_BLOCK
    i = pl.program_id(1)
    sub = lax.broadcasted_iota(jnp.int32, (HEAD_W, blk), 0)
    for h in range(HEADS):
        qh = qt_ref[0, h * HEAD_W:(h + 1) * HEAD_W, :]
        zero = jnp.zeros_like(qh)
        qz_scr[h] = jnp.concatenate([jnp.where(sub < DA_DH, qh, zero),
                                     jnp.where(sub >= DA_DH, qh, zero)], axis=1)

    hv = lambda h: slice(h * (HEAD_W + VT_PAD), (h + 1) * (HEAD_W + VT_PAD))

    meta_sel = jnp.minimum(i, 1)
    META = "meta"

    def key_blocks(blocks):
        def qk(j, table, h):
            hs = slice(h * HEAD_W, (h + 1) * HEAD_W)
            if j is META:
                return _dot(km_ref[:, hs], qz_scr[h]) + tm_ref[meta_sel, h]
            off = pl.multiple_of(j * blk, blk)
            s = _dot(k_ref[0, pl.ds(off, blk), hs], qz_scr[h])
            return s if table is None else s + tb_ref[table, h]

        def soft(j, h, s):
            m_cur = jnp.max(s, axis=0, keepdims=True)
            if j is META:
                m_scr[h] = m_cur
                return jnp.exp2(s - m_cur).astype(BF16), None
            m_prev = m_scr[h]
            m_new = jnp.maximum(m_prev, m_cur)
            m_scr[h] = m_new
            return jnp.exp2(s - m_new).astype(BF16), jnp.exp2(m_prev - m_new)

        def pv(j, h, p, alpha):
            if j is META:
                acc_scr[h] = _dot(vmt_ref[hv(h), :], p)
            else:
                acc_scr[h] = alpha * acc_scr[h] + _dot(vt_ref[j, hv(h), :], p)

        stages = [(j, table, h) for j, table in blocks for h in range(HEADS)]
        s_cur = qk(*stages[0])
        pend = None
        for n, (j, table, h) in enumerate(stages):
            s_next = qk(*stages[n + 1]) if n + 1 < len(stages) else None
            if pend is not None:
                pv(stages[n - 1][0], stages[n - 1][2], *pend)
            pend = soft(j, h, s_cur)
            s_cur = s_next
        pv(stages[-1][0], stages[-1][2], *pend)

    @pl.when(i >= 1)
    def _():
        key_blocks([(META, None), (i - 1, 1), (i, 0)])

    @pl.when(i == 0)
    def _():
        key_blocks([(META, None), (0, 0)])

    n_far = jnp.maximum(i - 1, 0)
    done = 0
    width = 1
    while width < FAR_UNROLL:
        start = done

        @pl.when((n_far & width) != 0)
        def _(start=start, width=width):
            key_blocks([(start + u, None) for u in range(width)])

        done = done + (n_far & width)
        width *= 2
    n_rem = done

    def far_body(t, carry):
        j = n_rem + FAR_UNROLL * t
        key_blocks([(j + u, None) for u in range(FAR_UNROLL)])
        return carry

    lax.fori_loop(0, n_far // FAR_UNROLL, far_body, 0)

    lam = _lambda_value(lam_ref[...], lam_init)
    onw = onw_ref[...]
    for h in range(HEADS):
        acc = acc_scr[h]
        a = acc[:HEAD_W] / acc[HEAD_W:HEAD_W + 1]
        ot = a[:, :blk] - lam * a[:, blk:]
        ot = ot * lax.rsqrt(jnp.mean(ot * ot, axis=0, keepdims=True) + EPS) * onw * (1.0 - lam_init)
        o_ref[0, :, h * HEAD_W:(h + 1) * HEAD_W] = ot.T.astype(o_ref.dtype)


def _pattn(qt, k, vt, km, vmt, tb, tmeta, lam_p, onw_col, lam_init):
    bsz, t, w = k.shape
    blk = ATT_BLOCK
    nblk = t // blk
    assert t % blk == 0
    return pl.pallas_call(
        functools.partial(_pattn_kernel, lam_init=lam_init),
        grid=(bsz, nblk),
        in_specs=[pl.BlockSpec((1, w, blk), lambda b, i: (b * nblk + i, 0, 0)),
                  pl.BlockSpec((1, t, w), lambda b, i: (b, 0, 0)),
                  pl.BlockSpec((nblk, VT_ROWS, blk), lambda b, i: (b, 0, 0)),
                  _const_spec(km.shape), _const_spec(vmt.shape), _const_spec(tb.shape),
                  _const_spec(tmeta.shape), _const_spec(lam_p.shape), _const_spec(onw_col.shape)],
        out_specs=pl.BlockSpec((1, blk, w), lambda b, i: (b, i, 0)),
        out_shape=jax.ShapeDtypeStruct((bsz, t, w), BF16),
        scratch_shapes=[pltpu.VMEM((HEADS, HEAD_W, 2 * blk), BF16),
                        pltpu.VMEM((HEADS, 1, 2 * blk), F32),
                        pltpu.VMEM((HEADS, HEAD_W + VT_PAD, 2 * blk), F32)],
        compiler_params=pltpu.CompilerParams(dimension_semantics=("parallel", "arbitrary"),
                                             vmem_limit_bytes=VMEM_LIMIT),
        name="prompt_attn",
    )(qt, k, vt, km, vmt, tb, tmeta, lam_p, onw_col)


def _sattn_kernel(pt_ref, q_ref, kn_ref, vn_ref, tl_ref, tn_ref, tmask_ref, lam_ref, onw_ref,
                  ck_hbm, cv_hbm, o_ref, kbuf, vbuf, sem, m_scr, l_scr, acc_scr,
                  *, lam_init, n_tok, n_pages):
    pg = PAGES_PER_STEP
    prow = kbuf.shape[2]
    b, g = pl.program_id(0), pl.program_id(1)
    n_steps = pl.num_programs(1)
    t = b * n_steps + g
    slot = t % 2

    def page_copies(step, to_slot):
        base = (step // n_steps) * n_pages + (step % n_steps) * pg
        copies = []
        for u in range(pg):
            rows = pl.ds(pl.multiple_of(pt_ref[base + u] * prow, prow), prow)
            copies.append(pltpu.make_async_copy(ck_hbm.at[rows, :], kbuf.at[to_slot, u],
                                                sem.at[to_slot, 0]))
            copies.append(pltpu.make_async_copy(cv_hbm.at[rows, :], vbuf.at[to_slot, u],
                                                sem.at[to_slot, 1]))
        return copies

    @pl.when(t == 0)
    def _():
        for cp in page_copies(t, slot):
            cp.start()

    @pl.when(t + 1 < pl.num_programs(0) * n_steps)
    def _():
        for cp in page_copies(t + 1, 1 - slot):
            cp.start()

    q = q_ref[0]

    @pl.when(g == 0)
    def _():
        s = _dot_nt(q, kn_ref[0]) + tn_ref[...]
        _softmax_update(s, vn_ref[0], m_scr, l_scr, acc_scr, True)

    for cp in page_copies(t, slot):
        cp.wait()

    is_last = g == n_steps - 1
    tmask = tmask_ref[...]
    ss = []
    for u in range(pg):
        s = _dot_nt(q, kbuf[slot, u].astype(BF16))
        ss.append(s + (jnp.where(is_last, tl_ref[...], tmask) if u == pg - 1 else tmask))
    m_prev = m_scr[...]
    m_new = m_prev
    for s in ss:
        m_new = jnp.maximum(m_new, jnp.max(s, axis=1, keepdims=True))
    alpha = jnp.exp2(m_prev - m_new)
    l_new = alpha * l_scr[...]
    acc = alpha * acc_scr[...]
    for u in range(pg):
        p = jnp.exp2(ss[u] - m_new)
        l_new = l_new + jnp.sum(p, axis=1, keepdims=True)
        acc = acc + _dot(p.astype(BF16), vbuf[slot, u].astype(BF16))
    m_scr[...] = m_new
    l_scr[...] = l_new
    acc_scr[...] = acc

    @pl.when(is_last)
    def _():
        lam = _lambda_value(lam_ref[...], lam_init)
        onw = onw_ref[...]
        a = acc_scr[...] / l_scr[...]
        for h in range(HEADS):
            r1 = (2 * h) * n_tok
            r2 = (2 * h + 1) * n_tok
            o = a[r1:r1 + n_tok] - lam * a[r2:r2 + n_tok]
            o_ref[0, :, h * HEAD_W:(h + 1) * HEAD_W] = (
                _rms_rows(o, onw) * (1.0 - lam_init)).astype(o_ref.dtype)


def _sattn(page_table, qall, knew, vnew, tlast, tnew, tmask, lam_p, onw, ck2, cv2, lam_init, n_tok):
    nb, n_pages = page_table.shape
    pg = PAGES_PER_STEP
    assert n_pages % pg == 0
    nrow = qall.shape[1]
    prow = knew.shape[1]
    pt_flat = page_table.reshape(-1)
    per_b = lambda shape: pl.BlockSpec((1,) + shape, lambda b, g, pt: (b, 0, 0))
    const = lambda shape: pl.BlockSpec(shape, lambda b, g, pt: (0,) * len(shape))
    hbm = pl.BlockSpec(memory_space=pl.ANY)
    grid_spec = pltpu.PrefetchScalarGridSpec(
        num_scalar_prefetch=1,
        grid=(nb, n_pages // pg),
        in_specs=[per_b((nrow, HEAD_W)), per_b((prow, HEAD_W)), per_b((prow, HEAD_W)),
                  const(tlast.shape), const(tnew.shape), const(tmask.shape),
                  const(lam_p.shape), const(onw.shape), hbm, hbm],
        out_specs=per_b((n_tok, HEADS * HEAD_W)),
        scratch_shapes=[pltpu.VMEM((2, pg, prow, HEAD_W), F32), pltpu.VMEM((2, pg, prow, HEAD_W), F32),
                        pltpu.SemaphoreType.DMA((2, 2)),
                        pltpu.VMEM((nrow, 1), F32), pltpu.VMEM((nrow, 1), F32),
                        pltpu.VMEM((nrow, HEAD_W), F32)],
    )
    return pl.pallas_call(
        functools.partial(_sattn_kernel, lam_init=lam_init, n_tok=n_tok, n_pages=n_pages),
        grid_spec=grid_spec,
        out_shape=jax.ShapeDtypeStruct((nb, n_tok, HEADS * HEAD_W), F32),
        compiler_params=pltpu.CompilerParams(dimension_semantics=("arbitrary", "arbitrary"),
                                             vmem_limit_bytes=VMEM_LIMIT),
        name="sample_attn",
    )(pt_flat, qall, knew, vnew, tlast, tnew, tmask, lam_p, onw, ck2, cv2)


def _merge_kernel(x_ref, oa_ref, ob_ref, n1_ref, wg_ref, wha_ref, wda_ref, wo_ref, h_ref):
    x = x_ref[...]
    d = x.shape[1]
    xn = _rms_rows(x, n1_ref[...]).astype(BF16)
    gates = _dot(xn, wg_ref[...])
    ya = _dot(oa_ref[...], wha_ref[...])
    yb = _dot(ob_ref[...], wda_ref[...])
    merged = _sigmoid(gates[:, :d]) * ya + _sigmoid(gates[:, d:]) * yb
    h_ref[...] = x + _dot(merged.astype(BF16), wo_ref[...])


def _merge(x2d, oa, ob, n1, w_in_bf, w_ha, w_da, w_o, tm):
    n, d = x2d.shape
    row = lambda width: pl.BlockSpec((tm, width), lambda i: (i, 0))
    gate_spec = pl.BlockSpec((pl.Element(d), pl.Element(2 * d)), lambda i: (0, IN_A_COLS))
    return pl.pallas_call(
        _merge_kernel,
        grid=(n // tm,),
        in_specs=[row(d), row(512), row(512), _const_spec((1, d)), gate_spec,
                  _const_spec(w_ha.shape), _const_spec(w_da.shape), _const_spec(w_o.shape)],
        out_specs=row(d),
        out_shape=jax.ShapeDtypeStruct((n, d), F32),
        compiler_params=pltpu.CompilerParams(dimension_semantics=("parallel",),
                                             vmem_limit_bytes=VMEM_LIMIT),
        name="merge",
    )(x2d, oa, ob, n1, w_in_bf, w_ha, w_da, w_o)


def _moe_kernel(h_ref, n2_ref, wr_hi_ref, wr_lo_ref, br_ref, wg_ref, wu_ref, wd_ref, y_ref):
    h = h_ref[...]
    hn = _rms_rows(h, n2_ref[...])
    hn_hi = hn.astype(BF16)
    hn_lo = (hn - hn_hi.astype(F32)).astype(BF16)
    logits = (_dot(hn_hi, wr_hi_ref[...]) + _dot(hn_lo, wr_hi_ref[...])
              + _dot(hn_hi, wr_lo_ref[...]) + br_ref[...])
    lane = lax.broadcasted_iota(jnp.int32, logits.shape, 1)
    big = jnp.int32(1 << 20)
    ninf = -jnp.inf

    def first_lane(mask):
        return jnp.min(jnp.where(mask, lane, big), axis=1, keepdims=True)

    is_g = lane < N_GROUPS
    g_max = jnp.max(jnp.where(is_g, logits, ninf), axis=1, keepdims=True)
    g_idx = first_lane(is_g & (logits == g_max))
    g_sum = jnp.sum(jnp.where(is_g, jnp.exp(logits - g_max), 0.0), axis=1, keepdims=True)
    p_g = 1.0 / g_sum
    e_lane = lane - N_GROUPS
    in_group = ((e_lane >= 0) & (e_lane < N_EXPERTS)
                & (lax.shift_right_arithmetic(e_lane, 2) == g_idx))
    e_max = jnp.max(jnp.where(in_group, logits, ninf), axis=1, keepdims=True)
    e_exp = jnp.where(in_group, jnp.exp(logits - e_max), 0.0)
    e_prob = e_exp / jnp.sum(e_exp, axis=1, keepdims=True)
    p1 = jnp.max(jnp.where(in_group, e_prob, -1.0), axis=1, keepdims=True)
    i1 = first_lane(in_group & (e_prob == p1))
    rest = in_group & (lane != i1)
    p2 = jnp.max(jnp.where(rest, e_prob, -1.0), axis=1, keepdims=True)
    i2 = first_lane(rest & (e_prob == p2))
    denom = p1 + p2
    combine = (jnp.where(lane == i1, p_g * p1 / denom, 0.0)
               + jnp.where(lane == i2, p_g * p2 / denom, 0.0))

    acc = jnp.zeros_like(h)
    for ex in range(N_EXPERTS):
        gt = _dot(hn_hi, wg_ref[ex])
        up = _dot(hn_hi, wu_ref[ex])
        hh = _silu(gt) * up * combine[:, N_GROUPS + ex:N_GROUPS + ex + 1]
        acc = acc + _dot(hh.astype(BF16), wd_ref[ex])
    y_ref[...] = h + acc


def _moe(h2d, n2, wr_hi, wr_lo, br, wg, wu, wd, tm):
    n, d = h2d.shape
    row = pl.BlockSpec((tm, d), lambda i: (i, 0))
    single = lambda shape: pl.BlockSpec(shape, lambda i: (0,) * len(shape),
                                        pipeline_mode=pl.Buffered(1))
    return pl.pallas_call(
        _moe_kernel,
        grid=(n // tm,),
        in_specs=[row, _const_spec((1, d)), _const_spec(wr_hi.shape), _const_spec(wr_lo.shape),
                  _const_spec(br.shape), single(wg.shape), single(wu.shape), single(wd.shape)],
        out_specs=row,
        out_shape=jax.ShapeDtypeStruct((n, d), F32),
        compiler_params=pltpu.CompilerParams(dimension_semantics=("parallel",),
                                             vmem_limit_bytes=VMEM_LIMIT),
        name="moe",
    )(h2d, n2, wr_hi, wr_lo, br, wg, wu, wd)


def _t5_bias(dist, rel_bias):
    n = jnp.maximum(dist, 0)
    max_exact = N_BUCKETS // 2
    nf = jnp.maximum(n, 1).astype(F32)
    scaled = (jnp.log(nf / max_exact) / math.log(MAX_DISTANCE / max_exact)
              * (N_BUCKETS - max_exact))
    large = max_exact + jnp.floor(jnp.maximum(scaled, 0.0)).astype(jnp.int32)
    large = jnp.minimum(large, N_BUCKETS - 1)
    bucket = jnp.where(n < max_exact, n, large)
    rb = rel_bias.astype(F32)
    onehot = (bucket[..., None] == jnp.arange(N_BUCKETS)).astype(F32)
    bias = jnp.moveaxis(jnp.dot(onehot, rb, precision=lax.Precision.HIGHEST), -1, 0)
    far = rb[N_BUCKETS - 1].reshape((HEADS,) + (1,) * dist.ndim)
    return jnp.where(dist >= 0, (bias - far) * LOG2E, MASK_VALUE)


def kernel(x_prompt, x_sample, cache_k, cache_v, state_hgrn, page_table, meta_tokens, rel_bias,
           hg_lb_logits, norm1_w, w_in, hg_onorm_w, w_hg_out, q_norm_w, k_norm_w, da_lambda,
           da_onorm_w, w_da_out, w_o, norm2_w, w_router_group, b_router_group,
           w_router_expert, b_router_expert, w_gate, w_up, w_down):
    bsz, seq, d = x_prompt.shape
    nb, n_tok, _ = x_sample.shape
    depth = w_in.shape[0]
    assert depth == 1 and hg_lb_logits.shape[0] == 2
    n_pages = page_table.shape[1]
    psize = cache_k.shape[2]
    past_len = n_pages * psize
    blk = ATT_BLOCK
    assert blk >= MAX_DISTANCE and psize >= MAX_DISTANCE and n_tok <= 8
    lam_init = 0.8 - 0.6 * math.exp(-0.3 * 0)
    w_attn = HEADS * HEAD_W

    assert w_in.shape[2] == IN_A_COLS + 2 * d
    w_in_bf = w_in[0].astype(BF16)
    n1 = norm1_w[0].reshape(1, d)
    n2 = norm2_w[0].reshape(1, d)
    qw = (jnp.tile(q_norm_w[0], 2 * HEADS) * (DA_DH ** -0.5 * LOG2E)).reshape(1, w_attn)
    kw = jnp.tile(k_norm_w[0], 2 * HEADS).reshape(1, w_attn)
    gi = jnp.arange(w_attn) // DA_DH
    gmat = jnp.where(gi[:, None] == gi[None, :], 1.0 / DA_DH, 0.0).astype(BF16)
    hg_onw = jnp.tile(hg_onorm_w[0], HEADS).reshape(1, w_attn)
    da_onw = da_onorm_w[0].reshape(1, HEAD_W)
    lam_p = da_lambda[0].astype(F32)
    w_ha = w_hg_out[0].astype(BF16)
    w_da = w_da_out[0].astype(BF16)
    w_o0 = w_o[0].astype(BF16)
    wr = jnp.concatenate([w_router_group[0], w_router_expert[0],
                          jnp.zeros((d, 128 - N_GROUPS - N_EXPERTS), F32)], axis=1)
    wr_hi = wr.astype(BF16)
    wr_lo = (wr - wr_hi.astype(F32)).astype(BF16)
    br = jnp.concatenate([b_router_group[0], b_router_expert[0],
                          jnp.zeros((128 - N_GROUPS - N_EXPERTS,), F32)]).reshape(1, 128)
    wg = w_gate[0].astype(BF16)
    wu = w_up[0].astype(BF16)
    wd = w_down[0].astype(BF16)

    n_small = N_META + nb * n_tok
    x_small = jnp.concatenate([meta_tokens.astype(F32), x_sample.reshape(nb * n_tok, d)], axis=0)
    zh_s, q_s, kf_s, kb_s, vf_s, vb_s = _inproj_small(x_small, n1, w_in_bf, gmat, qw, kw)

    _, s_meta = _hgrn(zh_s[:N_META][None], hg_lb_logits, hg_onw,
                      jnp.zeros((1, HEADS, HEAD_W, HEAD_W), F32), shared_state=False, out_dtype=F32)
    oa_s, s_sample = _hgrn(zh_s[N_META:].reshape(nb, n_tok, 2048), hg_lb_logits, hg_onw,
                           state_hgrn[0], shared_state=False, out_dtype=F32)

    x_p = x_prompt.reshape(bsz * seq, d)
    zh_p, qt_p, kf_p, kb_p, vf_p, vt_p = _inproj_prompt(
        x_p, n1, w_in_bf, gmat, qw, kw, kf_s[:N_META].reshape(N_META * HEADS, HEAD_W),
        vf_s[:N_META].reshape(N_META * HEADS, HEAD_W), seq, 512)
    oa_p, s_prompt = _hgrn(zh_p.reshape(bsz, seq, 2048), hg_lb_logits, hg_onw, s_meta,
                           shared_state=True, out_dtype=BF16)

    r = jnp.arange(blk)
    tb = jnp.stack([_t5_bias(r[None, :] - r[:, None], rel_bias),
                    _t5_bias(blk + r[None, :] - r[:, None], rel_bias)])
    tb = jnp.concatenate([tb, tb], axis=3)
    mrow = jnp.arange(128)
    t_meta0 = _t5_bias(N_META + r[None, :] - mrow[:, None], rel_bias)
    t_meta = jnp.stack([t_meta0, jnp.zeros_like(t_meta0)])
    t_meta = jnp.where(mrow[:, None] < N_META, t_meta, MASK_VALUE)
    t_meta = jnp.concatenate([t_meta, t_meta], axis=3)
    pad_meta = lambda a: jnp.concatenate([a, jnp.zeros((128 - N_META, w_attn), a.dtype)], axis=0)
    vmt = pad_meta(vb_s[:N_META]).T.reshape(HEADS, HEAD_W, 128)
    vmt = jnp.concatenate([vmt, jnp.ones((HEADS, VT_PAD, 128), BF16)], axis=1).reshape(VT_ROWS, 128)
    ob_p = _pattn(qt_p, kb_p.reshape(bsz, seq, w_attn), vt_p, pad_meta(kb_s[:N_META]),
                  vmt, tb, t_meta, lam_p, da_onw.reshape(HEAD_W, 1), lam_init)

    q_tok = q_s[N_META:].reshape(nb, n_tok, HEADS, 1, 2, DA_DH)
    q_maps = q_tok * jnp.eye(2, dtype=BF16).reshape(1, 1, 1, 2, 2, 1)
    n_rows = 2 * HEADS * n_tok
    qall = jnp.transpose(q_maps, (0, 2, 3, 1, 4, 5)).reshape(nb, n_rows, HEAD_W)
    prow = psize * HEADS
    pad_new = lambda a: jnp.concatenate(
        [a.reshape(nb, n_tok * HEADS, HEAD_W),
         jnp.zeros((nb, prow - n_tok * HEADS, HEAD_W), a.dtype)], axis=1)
    s_idx = jnp.tile(jnp.arange(n_tok), 2 * HEADS)
    head_of_row = jnp.repeat(jnp.arange(HEADS), 2 * n_tok)
    key_r = jnp.arange(prow) // HEADS
    own_head = head_of_row[:, None] == (jnp.arange(prow) % HEADS)[None, :]
    dist_last = (past_len + s_idx[:, None]) - (past_len - psize + key_r[None, :])
    dist_new = jnp.where(key_r[None, :] < n_tok, s_idx[:, None] - key_r[None, :], -1)
    pick = lambda t: jnp.where(own_head, t[head_of_row, jnp.arange(n_rows)], MASK_VALUE)
    t_last = pick(_t5_bias(dist_last, rel_bias))
    t_new = pick(_t5_bias(dist_new, rel_bias))
    t_mask = jnp.where(own_head, 0.0, MASK_VALUE).astype(F32)
    ob_s = _sattn(page_table, qall, pad_new(kb_s[N_META:]), pad_new(vb_s[N_META:]), t_last, t_new,
                  t_mask, lam_p, da_onw, cache_k.reshape(-1, HEAD_W), cache_v.reshape(-1, HEAD_W),
                  lam_init, n_tok)

    h_p = _merge(x_p, oa_p.reshape(bsz * seq, w_attn), ob_p.reshape(bsz * seq, w_attn),
                 n1, w_in_bf, w_ha, w_da, w_o0, 512)
    y_p = _moe(h_p, n2, wr_hi, wr_lo, br, wg, wu, wd, 512)
    x_s = x_small[N_META:]
    h_s = _merge(x_s, oa_s.reshape(nb * n_tok, w_attn).astype(BF16),
                 ob_s.reshape(nb * n_tok, w_attn).astype(BF16), n1, w_in_bf, w_ha, w_da, w_o0,
                 nb * n_tok)
    y_s = _moe(h_s, n2, wr_hi, wr_lo, br, wg, wu, wd, nb * n_tok)

    return (y_p.reshape(bsz, seq, d),
            y_s.reshape(nb, n_tok, d),
            kf_p.reshape(1, bsz, seq + N_META, HEADS, HEAD_W),
            vf_p.reshape(1, bsz, seq + N_META, HEADS, HEAD_W),
            s_prompt[None],
            kf_s[N_META:].reshape(1, nb, n_tok, HEADS, HEAD_W),
            vf_s[N_META:].reshape(1, nb, n_tok, HEADS, HEAD_W),
            s_sample[None])
```

```python
import functools
import math

import jax
import jax.numpy as jnp
from jax import lax
from jax.experimental import pallas as pl
from jax.experimental.pallas import tpu as pltpu

F32 = jnp.float32
BF16 = jnp.bfloat16

EPS = 1e-6
N_META = 16
HEADS = 4
HEAD_W = 128
DA_DH = 64
N_BUCKETS = 32
MAX_DISTANCE = 128
N_GROUPS = 4
EXPERTS_PER_GROUP = 4
N_EXPERTS = N_GROUPS * EXPERTS_PER_GROUP
MASK_VALUE = -1e30
HG_CHUNK = 128
HG_STEP_CHUNKS = 4
HG_BASE = 8
ATT_BLOCK = 256
VT_PAD = 16
VT_ROWS = HEADS * (HEAD_W + VT_PAD)
FAR_UNROLL = 8
PAGES_PER_STEP = 16
WIDTH = HEADS * HEAD_W
HG_COLS = 4 * WIDTH
IN_A_COLS = HG_COLS + 3 * WIDTH
TOKEN_TILE = 512
LOG2E = math.log2(math.e)
VMEM_LIMIT = 56 * 1024 * 1024


def _dot(a, b):
    return jnp.dot(a, b, preferred_element_type=F32)


def _dot_nt(a, b):
    return lax.dot_general(a, b, (((1,), (1,)), ((), ())), preferred_element_type=F32)


def _dot_tn(a, b):
    return lax.dot_general(a, b, (((0,), (0,)), ((), ())), preferred_element_type=F32)


def _sigmoid(x):
    return 0.5 + 0.5 * jnp.tanh(0.5 * x)


def _silu(x):
    y = 0.5 * x
    return y + y * jnp.tanh(y)


def _rms_rows(x, w):
    return x * lax.rsqrt(jnp.mean(x * x, axis=-1, keepdims=True) + EPS) * w


def _const_spec(shape):
    zeros = (0,) * len(shape)
    return pl.BlockSpec(shape, lambda *_: zeros)


def _inproj_values(x_ref, n1_ref, w_ref, g_ref, qw_ref, kw_ref):
    xn = _rms_rows(x_ref[...], n1_ref[...]).astype(BF16)
    g = g_ref[...]

    def group_norm(z, w):
        sq = z * z
        hi = sq.astype(BF16)
        lo = (sq - hi.astype(F32)).astype(BF16)
        ms = _dot(hi, g) + _dot(lo, g)
        return z * lax.rsqrt(ms + EPS) * w

    q0, k0, v0 = HG_COLS, HG_COLS + WIDTH, HG_COLS + 2 * WIDTH
    zh = _dot(xn, w_ref[:, 0:HG_COLS])
    qn = group_norm(_dot(xn, w_ref[:, q0:k0]), qw_ref[...])
    kn = group_norm(_dot(xn, w_ref[:, k0:v0]), kw_ref[...])
    zv = _dot(xn, w_ref[:, v0:IN_A_COLS])
    return zh, qn, kn, zv


def _inproj_small_kernel(x_ref, n1_ref, w_ref, g_ref, qw_ref, kw_ref,
                         zh_ref, q_ref, kf_ref, kb_ref, vf_ref, vb_ref):
    zh, qn, kn, zv = _inproj_values(x_ref, n1_ref, w_ref, g_ref, qw_ref, kw_ref)
    zh_ref[...] = zh
    q_ref[...] = qn.astype(BF16)
    kf_ref[...] = kn
    kb_ref[...] = kn.astype(BF16)
    vf_ref[...] = zv
    vb_ref[...] = zv.astype(BF16)


def _inproj_prompt_kernel(x_ref, n1_ref, w_ref, g_ref, qw_ref, kw_ref, km_ref, vm_ref,
                          zh_ref, qt_ref, kf_hbm, kb_ref, vf_hbm, vt_ref,
                          kst, vst, sem, msem, *, seq):
    zh, qn, kn, zv = _inproj_values(x_ref, n1_ref, w_ref, g_ref, qw_ref, kw_ref)
    tm = zh.shape[0]
    tiles = seq // tm
    i = pl.program_id(0)
    n_steps = pl.num_programs(0)
    slot = i % 2

    def batch_row0(step):
        return (step // tiles) * (N_META + seq) * HEADS

    def tile_copies(step, from_slot):
        row0 = batch_row0(step) + (N_META + (step % tiles) * tm) * HEADS
        rows = pl.ds(pl.multiple_of(row0, 8 * HEADS), tm * HEADS)
        return [pltpu.make_async_copy(kst.at[from_slot], kf_hbm.at[rows, :], sem.at[from_slot, 0]),
                pltpu.make_async_copy(vst.at[from_slot], vf_hbm.at[rows, :], sem.at[from_slot, 1])]

    zh_ref[...] = zh
    kb_ref[...] = kn.astype(BF16)
    qt = qn.T
    vt = zv.T
    for c in range(tm // ATT_BLOCK):
        cols = slice(c * ATT_BLOCK, (c + 1) * ATT_BLOCK)
        qt_ref[c] = qt[:, cols].astype(BF16)
        for h in range(HEADS):
            r0 = h * (HEAD_W + VT_PAD)
            vt_ref[c, r0:r0 + HEAD_W, :] = vt[h * HEAD_W:(h + 1) * HEAD_W, cols].astype(BF16)
            vt_ref[c, r0 + HEAD_W:r0 + HEAD_W + VT_PAD, :] = jnp.ones((VT_PAD, ATT_BLOCK), BF16)

    @pl.when(i >= 2)
    def _():
        for cp in tile_copies(i - 2, slot):
            cp.wait()

    for h in range(HEADS):
        hs = slice(h * HEAD_W, (h + 1) * HEAD_W)
        kst[slot, pl.ds(h, tm, stride=HEADS), :] = kn[:, hs]
        vst[slot, pl.ds(h, tm, stride=HEADS), :] = zv[:, hs]
    for cp in tile_copies(i, slot):
        cp.start()

    @pl.when(i % tiles == 0)
    def _():
        rows = pl.ds(pl.multiple_of(batch_row0(i), 8 * HEADS), N_META * HEADS)
        meta = [pltpu.make_async_copy(km_ref, kf_hbm.at[rows, :], msem.at[0]),
                pltpu.make_async_copy(vm_ref, vf_hbm.at[rows, :], msem.at[1])]
        for cp in meta:
            cp.start()
        for cp in meta:
            cp.wait()

    @pl.when(i == n_steps - 1)
    def _():
        @pl.when(i >= 1)
        def _():
            for cp in tile_copies(i - 1, 1 - slot):
                cp.wait()
        for cp in tile_copies(i, slot):
            cp.wait()


def _w_in_spec(d):
    return pl.BlockSpec((d, IN_A_COLS), lambda i: (0, 0))


def _inproj_small(x2d, n1, w_in_bf, gmat, qw, kw):
    n, d = x2d.shape
    full = lambda width: _const_spec((n, width))
    return pl.pallas_call(
        _inproj_small_kernel,
        grid=(1,),
        in_specs=[full(d), _const_spec((1, d)), _w_in_spec(d), _const_spec(gmat.shape),
                  _const_spec((1, WIDTH)), _const_spec((1, WIDTH))],
        out_specs=[full(HG_COLS)] + [full(WIDTH)] * 5,
        out_shape=[jax.ShapeDtypeStruct((n, HG_COLS), F32), jax.ShapeDtypeStruct((n, WIDTH), BF16),
                   jax.ShapeDtypeStruct((n, WIDTH), F32), jax.ShapeDtypeStruct((n, WIDTH), BF16),
                   jax.ShapeDtypeStruct((n, WIDTH), F32), jax.ShapeDtypeStruct((n, WIDTH), BF16)],
        compiler_params=pltpu.CompilerParams(dimension_semantics=("arbitrary",),
                                             vmem_limit_bytes=VMEM_LIMIT),
        name="inproj_small",
    )(x2d, n1, w_in_bf, gmat, qw, kw)


def _inproj_prompt(x2d, n1, w_in_bf, gmat, qw, kw, k_meta, v_meta, seq, tm):
    n, d = x2d.shape
    cache_shape = jax.ShapeDtypeStruct(((n // seq) * (N_META + seq) * HEADS, HEAD_W), F32)
    assert seq % tm == 0 and tm % ATT_BLOCK == 0
    per = tm // ATT_BLOCK
    row = lambda width: pl.BlockSpec((tm, width), lambda i: (i, 0))
    t_spec = lambda rows: pl.BlockSpec((per, rows, ATT_BLOCK), lambda i: (i, 0, 0))
    t_shape = lambda rows: jax.ShapeDtypeStruct((n // ATT_BLOCK, rows, ATT_BLOCK), BF16)
    hbm = pl.BlockSpec(memory_space=pl.ANY)
    return pl.pallas_call(
        functools.partial(_inproj_prompt_kernel, seq=seq),
        grid=(n // tm,),
        in_specs=[row(d), _const_spec((1, d)), _w_in_spec(d), _const_spec(gmat.shape),
                  _const_spec((1, WIDTH)), _const_spec((1, WIDTH)),
                  _const_spec(k_meta.shape), _const_spec(v_meta.shape)],
        out_specs=[row(HG_COLS), t_spec(WIDTH), hbm, row(WIDTH), hbm, t_spec(VT_ROWS)],
        out_shape=[jax.ShapeDtypeStruct((n, HG_COLS), F32), t_shape(WIDTH), cache_shape,
                   jax.ShapeDtypeStruct((n, WIDTH), BF16), cache_shape, t_shape(VT_ROWS)],
        scratch_shapes=[pltpu.VMEM((2, tm * HEADS, HEAD_W), F32),
                        pltpu.VMEM((2, tm * HEADS, HEAD_W), F32),
                        pltpu.SemaphoreType.DMA((2, 2)), pltpu.SemaphoreType.DMA((2,))],
        compiler_params=pltpu.CompilerParams(dimension_semantics=("arbitrary",),
                                             vmem_limit_bytes=VMEM_LIMIT),
        name="inproj",
    )(x2d, n1, w_in_bf, gmat, qw, kw, k_meta, v_meta)


def _hgrn_kernel(z_ref, lbl_ref, onw_ref, s0_ref, o_ref, sout_ref, st_scr, *, rows):
    C = HG_CHUNK
    W = HEADS * HEAD_W
    n_sub = max(rows // C, 1)
    n_valid = min(rows, C)
    c = pl.program_id(1)
    heads = [slice(h * HEAD_W, (h + 1) * HEAD_W) for h in range(HEADS)]

    @pl.when(c == 0)
    def _():
        for h in range(HEADS):
            st_scr[h] = s0_ref[0, h].T

    lg = lbl_ref[...]
    e = jnp.exp(lg - jnp.max(lg, axis=0, keepdims=True))
    lb = e[0:1] / jnp.sum(e, axis=0, keepdims=True)
    half = 0.5 * (1.0 - lb)
    onw = onw_ref[...]
    row = lax.broadcasted_iota(jnp.int32, (C, C), 0)
    col = lax.broadcasted_iota(jnp.int32, (C, C), 1)
    tri = jnp.where(col <= row, 1.0, 0.0).astype(BF16)

    def shr(x, width):
        return lax.shift_right_logical(x, int(math.log2(width)))

    def front(z, n_valid):
        hq, hf, hi, hg = z[:, 0:W], z[:, W:2 * W], z[:, 2 * W:3 * W], z[:, 3 * W:4 * W]
        q = _silu(hq)
        f = (lb + half) + half * jnp.tanh(0.5 * hf)
        k = 1.0 - f
        logf = jnp.log2(f)
        if n_valid < C:
            valid = lax.broadcasted_iota(jnp.int32, (C, W), 0) < n_valid
            k = jnp.where(valid, k, 0.0)
            logf = jnp.where(valid, logf, 0.0)
        x1 = logf.astype(BF16)
        r1 = logf - x1.astype(F32)
        x2 = r1.astype(BF16)
        x3 = (r1 - x2.astype(F32)).astype(BF16)
        b = _dot(tri, x1) + _dot(tri, x2) + _dot(tri, x3)
        return q, k, hi.astype(BF16), b, _silu(hg)

    def intra(q, k, b):
        b_last = b[C - 1:C]

        def block_row(width, r):
            b3 = b.reshape(C // width, width, W)
            return jnp.broadcast_to(b3[:, r:r + 1, :], b3.shape).reshape(C, W)

        def next_block_first(width):
            b3 = b.reshape(C // width, width, W)
            nxt = jnp.concatenate([b3[1:, 0:1, :], b_last[None]], axis=0)
            return jnp.broadcast_to(nxt, b3.shape).reshape(C, W)

        mid = block_row(HG_BASE, HG_BASE // 2)
        qe = (q * jnp.exp2(b - mid)).astype(BF16)
        ke = (k * jnp.exp2(mid - b)).astype(BF16)
        mask = (shr(row, HG_BASE) == shr(col, HG_BASE)) & (col <= row)
        scores = [jnp.where(mask, _dot_nt(qe[:, hs], ke[:, hs]), 0.0) for hs in heads]
        width = HG_BASE
        while width < n_valid:
            qe = (q * jnp.exp2(b - block_row(width, 0))).astype(BF16)
            ke = (k * jnp.exp2(next_block_first(width) - b)).astype(BF16)
            mask = ((shr(row, width) == shr(col, width) + 1)
                    & (shr(row, 2 * width) == shr(col, 2 * width)))
            scores = [jnp.where(mask, _dot_nt(qe[:, hs], ke[:, hs]), sc)
                      for hs, sc in zip(heads, scores)]
            width *= 2
        qb = (q * jnp.exp2(b)).astype(BF16)
        kd = (k * jnp.exp2(b_last - b)).astype(BF16)
        return [sc.astype(BF16) for sc in scores], qb, kd, jnp.exp2(b_last)

    fronts = []
    for u in range(n_sub):
        z = z_ref[0, u * n_valid:(u + 1) * n_valid, :]
        if n_valid < C:
            z = jnp.concatenate([z, jnp.zeros((C - n_valid, z.shape[1]), F32)], axis=0)
        fronts.append(front(z, n_valid))
    intras = [intra(q, k, b) for q, k, _, b, _ in fronts]
    for u in range(n_sub):
        _, _, v_bf, _, gate = fronts[u]
        scores, qb, kd, decay = intras[u]
        outs = []
        for h, hs in enumerate(heads):
            st = st_scr[h]
            outs.append(_dot_nt(qb[:, hs], st.astype(BF16)) + _dot(scores[h], v_bf[:, hs]))
            st_scr[h] = st * decay[:, hs] + _dot_tn(v_bf[:, hs], kd[:, hs])
        for h, hs in enumerate(heads):
            on = _rms_rows(outs[h], onw[:, hs]) * gate[:, hs]
            o_ref[0, u * n_valid:(u + 1) * n_valid, hs] = on[:n_valid].astype(o_ref.dtype)

    @pl.when(c == pl.num_programs(1) - 1)
    def _():
        for h in range(HEADS):
            sout_ref[0, h] = st_scr[h].T


def _hgrn(zh, lbl, onw, s0, *, shared_state, out_dtype):
    nseq, t, _ = zh.shape
    rows = min(t, HG_STEP_CHUNKS * HG_CHUNK)
    assert t % rows == 0 and rows % 8 == 0 and (rows <= HG_CHUNK or rows % HG_CHUNK == 0)
    s_map = (lambda b, c: (0, 0, 0, 0)) if shared_state else (lambda b, c: (b, 0, 0, 0))
    return pl.pallas_call(
        functools.partial(_hgrn_kernel, rows=rows),
        grid=(nseq, t // rows),
        in_specs=[pl.BlockSpec((1, rows, HG_COLS), lambda b, c: (b, c, 0)),
                  _const_spec(lbl.shape), _const_spec((1, WIDTH)),
                  pl.BlockSpec((1, HEADS, HEAD_W, HEAD_W), s_map)],
        out_specs=[pl.BlockSpec((1, rows, WIDTH), lambda b, c: (b, c, 0)),
                   pl.BlockSpec((1, HEADS, HEAD_W, HEAD_W), lambda b, c: (b, 0, 0, 0))],
        out_shape=[jax.ShapeDtypeStruct((nseq, t, WIDTH), out_dtype),
                   jax.ShapeDtypeStruct((nseq, HEADS, HEAD_W, HEAD_W), F32)],
        scratch_shapes=[pltpu.VMEM((HEADS, HEAD_W, HEAD_W), F32)],
        compiler_params=pltpu.CompilerParams(dimension_semantics=("parallel", "arbitrary"),
                                             vmem_limit_bytes=VMEM_LIMIT),
        name="hgrn",
    )(zh, lbl, onw, s0)


def _lambda_value(lp, lam_init):
    s1 = jnp.sum(lp[0:1] * lp[1:2], axis=1, keepdims=True)
    s2 = jnp.sum(lp[2:3] * lp[3:4], axis=1, keepdims=True)
    return jnp.exp(s1) - jnp.exp(s2) + lam_init


def _softmax_update(s, v_bf, m_ref, l_ref, acc_ref, first):
    m_cur = jnp.max(s, axis=1, keepdims=True)
    if first:
        m_new = m_cur
        p = jnp.exp2(s - m_new)
        l_ref[...] = jnp.sum(p, axis=1, keepdims=True)
        acc_ref[...] = _dot(p.astype(BF16), v_bf)
    else:
        m_prev = m_ref[...]
        m_new = jnp.maximum(m_prev, m_cur)
        alpha = jnp.exp2(m_prev - m_new)
        p = jnp.exp2(s - m_new)
        l_ref[...] = alpha * l_ref[...] + jnp.sum(p, axis=1, keepdims=True)
        acc_ref[...] = alpha * acc_ref[...] + _dot(p.astype(BF16), v_bf)
    m_ref[...] = m_new


def _pattn_kernel(qt_ref, k_ref, vt_ref, km_ref, vmt_ref, tb_ref, tm_ref, lam_ref, onw_ref, o_ref,
                  qz_scr, m_scr, acc_scr, *, lam_init):
    blk = ATT_BLOCK
    i = pl.program_id(1)
    sub = lax.broadcasted_iota(jnp.int32, (HEAD_W, blk), 0)
    for h in range(HEADS):
        qh = qt_ref[0, h * HEAD_W:(h + 1) * HEAD_W, :]
        zero = jnp.zeros_like(qh)
        qz_scr[h] = jnp.concatenate([jnp.where(sub < DA_DH, qh, zero),
                                     jnp.where(sub >= DA_DH, qh, zero)], axis=1)

    hv = lambda h: slice(h * (HEAD_W + VT_PAD), (h + 1) * (HEAD_W + VT_PAD))

    meta_sel = jnp.minimum(i, 1)
    META = "meta"

    def key_blocks(blocks):
        def qk(j, table, h):
            hs = slice(h * HEAD_W, (h + 1) * HEAD_W)
            if j is META:
                return _dot(km_ref[:, hs], qz_scr[h]) + tm_ref[meta_sel, h]
            off = pl.multiple_of(j * blk, blk)
            s = _dot(k_ref[0, pl.ds(off, blk), hs], qz_scr[h])
            return s if table is None else s + tb_ref[table, h]

        def soft(j, h, s):
            m_cur = jnp.max(s, axis=0, keepdims=True)
            if j is META:
                m_scr[h] = m_cur
                return jnp.exp2(s - m_cur).astype(BF16), None
            m_prev = m_scr[h]
            m_new = jnp.maximum(m_prev, m_cur)
            m_scr[h] = m_new
            return jnp.exp2(s - m_new).astype(BF16), jnp.exp2(m_prev - m_new)

        def pv(j, h, p, alpha):
            if j is META:
                acc_scr[h] = _dot(vmt_ref[hv(h), :], p)
            else:
                acc_scr[h] = alpha * acc_scr[h] + _dot(vt_ref[j, hv(h), :], p)

        stages = [(j, table, h) for j, table in blocks for h in range(HEADS)]
        s_cur = qk(*stages[0])
        pend = None
        for n, (j, table, h) in enumerate(stages):
            s_next = qk(*stages[n + 1]) if n + 1 < len(stages) else None
            if pend is not None:
                pv(stages[n - 1][0], stages[n - 1][2], *pend)
            pend = soft(j, h, s_cur)
            s_cur = s_next
        pv(stages[-1][0], stages[-1][2], *pend)

    @pl.when(i >= 1)
    def _():
        key_blocks([(META, None), (i - 1, 1), (i, 0)])

    @pl.when(i == 0)
    def _():
        key_blocks([(META, None), (0, 0)])

    n_far = jnp.maximum(i - 1, 0)
    done = 0
    width = 1
    while width < FAR_UNROLL:
        start = done

        @pl.when((n_far & width) != 0)
        def _(start=start, width=width):
            key_blocks([(start + u, None) for u in range(width)])

        done = done + (n_far & width)
        width *= 2
    n_rem = done

    def far_body(t, carry):
        j = n_rem + FAR_UNROLL * t
        key_blocks([(j + u, None) for u in range(FAR_UNROLL)])
        return carry

    lax.fori_loop(0, n_far // FAR_UNROLL, far_body, 0)

    lam = _lambda_value(lam_ref[...], lam_init)
    onw = onw_ref[...]
    for h in range(HEADS):
        acc = acc_scr[h]
        a = acc[:HEAD_W] / acc[HEAD_W:HEAD_W + 1]
        ot = a[:, :blk] - lam * a[:, blk:]
        ot = ot * lax.rsqrt(jnp.mean(ot * ot, axis=0, keepdims=True) + EPS) * onw * (1.0 - lam_init)
        o_ref[0, :, h * HEAD_W:(h + 1) * HEAD_W] = ot.T.astype(o_ref.dtype)


def _pattn(qt, k, vt, km, vmt, tb, tmeta, lam_p, onw_col, lam_init):
    bsz, t, w = k.shape
    blk = ATT_BLOCK
    nblk = t // blk
    assert t % blk == 0
    return pl.pallas_call(
        functools.partial(_pattn_kernel, lam_init=lam_init),
        grid=(bsz, nblk),
        in_specs=[pl.BlockSpec((1, w, blk), lambda b, i: (b * nblk + i, 0, 0)),
                  pl.BlockSpec((1, t, w), lambda b, i: (b, 0, 0)),
                  pl.BlockSpec((nblk, VT_ROWS, blk), lambda b, i: (b, 0, 0)),
                  _const_spec(km.shape), _const_spec(vmt.shape), _const_spec(tb.shape),
                  _const_spec(tmeta.shape), _const_spec(lam_p.shape), _const_spec(onw_col.shape)],
        out_specs=pl.BlockSpec((1, blk, w), lambda b, i: (b, i, 0)),
        out_shape=jax.ShapeDtypeStruct((bsz, t, w), BF16),
        scratch_shapes=[pltpu.VMEM((HEADS, HEAD_W, 2 * blk), BF16),
                        pltpu.VMEM((HEADS, 1, 2 * blk), F32),
                        pltpu.VMEM((HEADS, HEAD_W + VT_PAD, 2 * blk), F32)],
        compiler_params=pltpu.CompilerParams(dimension_semantics=("parallel", "arbitrary"),
                                             vmem_limit_bytes=VMEM_LIMIT),
        name="prompt_attn",
    )(qt, k, vt, km, vmt, tb, tmeta, lam_p, onw_col)


def _sattn_kernel(pt_ref, q_ref, kn_ref, vn_ref, tl_ref, tn_ref, tmask_ref, lam_ref, onw_ref,
                  ck_hbm, cv_hbm, o_ref, kbuf, vbuf, sem, m_scr, l_scr, acc_scr,
                  *, lam_init, n_tok, n_pages):
    pg = PAGES_PER_STEP
    prow = kbuf.shape[2]
    b, g = pl.program_id(0), pl.program_id(1)
    n_steps = pl.num_programs(1)
    t = b * n_steps + g
    slot = t % 2

    def page_copies(step, to_slot):
        base = (step // n_steps) * n_pages + (step % n_steps) * pg
        copies = []
        for u in range(pg):
            rows = pl.ds(pl.multiple_of(pt_ref[base + u] * prow, prow), prow)
            copies.append(pltpu.make_async_copy(ck_hbm.at[rows, :], kbuf.at[to_slot, u],
                                                sem.at[to_slot, 0]))
            copies.append(pltpu.make_async_copy(cv_hbm.at[rows, :], vbuf.at[to_slot, u],
                                                sem.at[to_slot, 1]))
        return copies

    @pl.when(t == 0)
    def _():
        for cp in page_copies(t, slot):
            cp.start()

    @pl.when(t + 1 < pl.num_programs(0) * n_steps)
    def _():
        for cp in page_copies(t + 1, 1 - slot):
            cp.start()

    q = q_ref[0]

    @pl.when(g == 0)
    def _():
        s = _dot_nt(q, kn_ref[0]) + tn_ref[...]
        _softmax_update(s, vn_ref[0], m_scr, l_scr, acc_scr, True)

    for cp in page_copies(t, slot):
        cp.wait()

    is_last = g == n_steps - 1
    tmask = tmask_ref[...]
    ss = []
    for u in range(pg):
        s = _dot_nt(q, kbuf[slot, u].astype(BF16))
        ss.append(s + (jnp.where(is_last, tl_ref[...], tmask) if u == pg - 1 else tmask))
    m_prev = m_scr[...]
    m_new = m_prev
    for s in ss:
        m_new = jnp.maximum(m_new, jnp.max(s, axis=1, keepdims=True))
    alpha = jnp.exp2(m_prev - m_new)
    l_new = alpha * l_scr[...]
    acc = alpha * acc_scr[...]
    for u in range(pg):
        p = jnp.exp2(ss[u] - m_new)
        l_new = l_new + jnp.sum(p, axis=1, keepdims=True)
        acc = acc + _dot(p.astype(BF16), vbuf[slot, u].astype(BF16))
    m_scr[...] = m_new
    l_scr[...] = l_new
    acc_scr[...] = acc

    @pl.when(is_last)
    def _():
        lam = _lambda_value(lam_ref[...], lam_init)
        onw = onw_ref[...]
        a = acc_scr[...] / l_scr[...]
        for h in range(HEADS):
            r1 = (2 * h) * n_tok
            r2 = (2 * h + 1) * n_tok
            o = a[r1:r1 + n_tok] - lam * a[r2:r2 + n_tok]
            o_ref[0, :, h * HEAD_W:(h + 1) * HEAD_W] = (
                _rms_rows(o, onw) * (1.0 - lam_init)).astype(o_ref.dtype)


def _sattn(page_table, qall, knew, vnew, tlast, tnew, tmask, lam_p, onw, ck2, cv2, lam_init, n_tok):
    nb, n_pages = page_table.shape
    pg = PAGES_PER_STEP
    assert n_pages % pg == 0
    nrow = qall.shape[1]
    prow = knew.shape[1]
    pt_flat = page_table.reshape(-1)
    per_b = lambda shape: pl.BlockSpec((1,) + shape, lambda b, g, pt: (b, 0, 0))
    const = lambda shape: pl.BlockSpec(shape, lambda b, g, pt: (0,) * len(shape))
    hbm = pl.BlockSpec(memory_space=pl.ANY)
    grid_spec = pltpu.PrefetchScalarGridSpec(
        num_scalar_prefetch=1,
        grid=(nb, n_pages // pg),
        in_specs=[per_b((nrow, HEAD_W)), per_b((prow, HEAD_W)), per_b((prow, HEAD_W)),
                  const(tlast.shape), const(tnew.shape), const(tmask.shape),
                  const(lam_p.shape), const(onw.shape), hbm, hbm],
        out_specs=per_b((n_tok, HEADS * HEAD_W)),
        scratch_shapes=[pltpu.VMEM((2, pg, prow, HEAD_W), F32), pltpu.VMEM((2, pg, prow, HEAD_W), F32),
                        pltpu.SemaphoreType.DMA((2, 2)),
                        pltpu.VMEM((nrow, 1), F32), pltpu.VMEM((nrow, 1), F32),
                        pltpu.VMEM((nrow, HEAD_W), F32)],
    )
    return pl.pallas_call(
        functools.partial(_sattn_kernel, lam_init=lam_init, n_tok=n_tok, n_pages=n_pages),
        grid_spec=grid_spec,
        out_shape=jax.ShapeDtypeStruct((nb, n_tok, HEADS * HEAD_W), F32),
        compiler_params=pltpu.CompilerParams(dimension_semantics=("arbitrary", "arbitrary"),
                                             vmem_limit_bytes=VMEM_LIMIT),
        name="sample_attn",
    )(pt_flat, qall, knew, vnew, tlast, tnew, tmask, lam_p, onw, ck2, cv2)


def _merge_kernel(x_ref, oa_ref, ob_ref, n1_ref, wg_ref, wha_ref, wda_ref, wo_ref, h_ref):
    x = x_ref[...]
    d = x.shape[1]
    xn = _rms_rows(x, n1_ref[...]).astype(BF16)
    gates = _dot(xn, wg_ref[...])
    ya = _dot(oa_ref[...], wha_ref[...])
    yb = _dot(ob_ref[...], wda_ref[...])
    merged = _sigmoid(gates[:, :d]) * ya + _sigmoid(gates[:, d:]) * yb
    h_ref[...] = x + _dot(merged.astype(BF16), wo_ref[...])


def _merge(x2d, oa, ob, n1, w_in_bf, w_ha, w_da, w_o, tm):
    n, d = x2d.shape
    row = lambda width: pl.BlockSpec((tm, width), lambda i: (i, 0))
    gate_spec = pl.BlockSpec((pl.Element(d), pl.Element(2 * d)), lambda i: (0, IN_A_COLS))
    return pl.pallas_call(
        _merge_kernel,
        grid=(n // tm,),
        in_specs=[row(d), row(WIDTH), row(WIDTH), _const_spec((1, d)), gate_spec,
                  _const_spec(w_ha.shape), _const_spec(w_da.shape), _const_spec(w_o.shape)],
        out_specs=row(d),
        out_shape=jax.ShapeDtypeStruct((n, d), F32),
        compiler_params=pltpu.CompilerParams(dimension_semantics=("parallel",),
                                             vmem_limit_bytes=VMEM_LIMIT),
        name="merge",
    )(x2d, oa, ob, n1, w_in_bf, w_ha, w_da, w_o)


def _moe_kernel(h_ref, n2_ref, wr_hi_ref, wr_lo_ref, br_ref, wg_ref, wu_ref, wd_ref, y_ref):
    h = h_ref[...]
    hn = _rms_rows(h, n2_ref[...])
    hn_hi = hn.astype(BF16)
    hn_lo = (hn - hn_hi.astype(F32)).astype(BF16)
    logits = (_dot(hn_hi, wr_hi_ref[...]) + _dot(hn_lo, wr_hi_ref[...])
              + _dot(hn_hi, wr_lo_ref[...]) + br_ref[...])
    lane = lax.broadcasted_iota(jnp.int32, logits.shape, 1)
    big = jnp.int32(1 << 20)
    ninf = -jnp.inf

    def first_lane(mask):
        return jnp.min(jnp.where(mask, lane, big), axis=1, keepdims=True)

    is_g = lane < N_GROUPS
    g_max = jnp.max(jnp.where(is_g, logits, ninf), axis=1, keepdims=True)
    g_idx = first_lane(is_g & (logits == g_max))
    g_sum = jnp.sum(jnp.where(is_g, jnp.exp(logits - g_max), 0.0), axis=1, keepdims=True)
    p_g = 1.0 / g_sum
    e_lane = lane - N_GROUPS
    in_group = ((e_lane >= 0) & (e_lane < N_EXPERTS)
                & (lax.shift_right_arithmetic(e_lane, 2) == g_idx))
    e_max = jnp.max(jnp.where(in_group, logits, ninf), axis=1, keepdims=True)
    e_exp = jnp.where(in_group, jnp.exp(logits - e_max), 0.0)
    e_prob = e_exp / jnp.sum(e_exp, axis=1, keepdims=True)
    p1 = jnp.max(jnp.where(in_group, e_prob, -1.0), axis=1, keepdims=True)
    i1 = first_lane(in_group & (e_prob == p1))
    rest = in_group & (lane != i1)
    p2 = jnp.max(jnp.where(rest, e_prob, -1.0), axis=1, keepdims=True)
    i2 = first_lane(rest & (e_prob == p2))
    denom = p1 + p2
    combine = (jnp.where(lane == i1, p_g * p1 / denom, 0.0)
               + jnp.where(lane == i2, p_g * p2 / denom, 0.0))

    acc = jnp.zeros_like(h)
    for ex in range(N_EXPERTS):
        gt = _dot(hn_hi, wg_ref[ex])
        up = _dot(hn_hi, wu_ref[ex])
        hh = _silu(gt) * up * combine[:, N_GROUPS + ex:N_GROUPS + ex + 1]
        acc = acc + _dot(hh.astype(BF16), wd_ref[ex])
    y_ref[...] = h + acc


def _moe(h2d, n2, wr_hi, wr_lo, br, wg, wu, wd, tm):
    n, d = h2d.shape
    row = pl.BlockSpec((tm, d), lambda i: (i, 0))
    single = lambda shape: pl.BlockSpec(shape, lambda i: (0,) * len(shape),
                                        pipeline_mode=pl.Buffered(1))
    return pl.pallas_call(
        _moe_kernel,
        grid=(n // tm,),
        in_specs=[row, _const_spec((1, d)), _const_spec(wr_hi.shape), _const_spec(wr_lo.shape),
                  _const_spec(br.shape), single(wg.shape), single(wu.shape), single(wd.shape)],
        out_specs=row,
        out_shape=jax.ShapeDtypeStruct((n, d), F32),
        compiler_params=pltpu.CompilerParams(dimension_semantics=("parallel",),
                                             vmem_limit_bytes=VMEM_LIMIT),
        name="moe",
    )(h2d, n2, wr_hi, wr_lo, br, wg, wu, wd)


def _t5_bias(dist, rel_bias):
    n = jnp.maximum(dist, 0)
    max_exact = N_BUCKETS // 2
    nf = jnp.maximum(n, 1).astype(F32)
    scaled = (jnp.log(nf / max_exact) / math.log(MAX_DISTANCE / max_exact)
              * (N_BUCKETS - max_exact))
    large = max_exact + jnp.floor(jnp.maximum(scaled, 0.0)).astype(jnp.int32)
    large = jnp.minimum(large, N_BUCKETS - 1)
    bucket = jnp.where(n < max_exact, n, large)
    rb = rel_bias.astype(F32)
    onehot = (bucket[..., None] == jnp.arange(N_BUCKETS)).astype(F32)
    bias = jnp.moveaxis(jnp.dot(onehot, rb, precision=lax.Precision.HIGHEST), -1, 0)
    far = rb[N_BUCKETS - 1].reshape((HEADS,) + (1,) * dist.ndim)
    return jnp.where(dist >= 0, (bias - far) * LOG2E, MASK_VALUE)


def kernel(x_prompt, x_sample, cache_k, cache_v, state_hgrn, page_table, meta_tokens, rel_bias,
           hg_lb_logits, norm1_w, w_in, hg_onorm_w, w_hg_out, q_norm_w, k_norm_w, da_lambda,
           da_onorm_w, w_da_out, w_o, norm2_w, w_router_group, b_router_group,
           w_router_expert, b_router_expert, w_gate, w_up, w_down):
    bsz, seq, d = x_prompt.shape
    nb, n_tok, _ = x_sample.shape
    depth = w_in.shape[0]
    assert depth == 1 and hg_lb_logits.shape[0] == 2
    n_pages = page_table.shape[1]
    psize = cache_k.shape[2]
    past_len = n_pages * psize
    blk = ATT_BLOCK
    assert blk >= MAX_DISTANCE and psize >= MAX_DISTANCE and n_tok <= 8
    lam_init = 0.8 - 0.6 * math.exp(-0.3 * 0)
    w_attn = WIDTH

    assert w_in.shape[2] == IN_A_COLS + 2 * d
    w_in_bf = w_in[0].astype(BF16)
    n1 = norm1_w[0].reshape(1, d)
    n2 = norm2_w[0].reshape(1, d)
    qw = (jnp.tile(q_norm_w[0], 2 * HEADS) * (DA_DH ** -0.5 * LOG2E)).reshape(1, w_attn)
    kw = jnp.tile(k_norm_w[0], 2 * HEADS).reshape(1, w_attn)
    gi = jnp.arange(w_attn) // DA_DH
    gmat = jnp.where(gi[:, None] == gi[None, :], 1.0 / DA_DH, 0.0).astype(BF16)
    hg_onw = jnp.tile(hg_onorm_w[0], HEADS).reshape(1, w_attn)
    da_onw = da_onorm_w[0].reshape(1, HEAD_W)
    lam_p = da_lambda[0].astype(F32)
    w_ha = w_hg_out[0].astype(BF16)
    w_da = w_da_out[0].astype(BF16)
    w_o0 = w_o[0].astype(BF16)
    wr = jnp.concatenate([w_router_group[0], w_router_expert[0],
                          jnp.zeros((d, 128 - N_GROUPS - N_EXPERTS), F32)], axis=1)
    wr_hi = wr.astype(BF16)
    wr_lo = (wr - wr_hi.astype(F32)).astype(BF16)
    br = jnp.concatenate([b_router_group[0], b_router_expert[0],
                          jnp.zeros((128 - N_GROUPS - N_EXPERTS,), F32)]).reshape(1, 128)
    wg = w_gate[0].astype(BF16)
    wu = w_up[0].astype(BF16)
    wd = w_down[0].astype(BF16)

    n_small = N_META + nb * n_tok
    x_small = jnp.concatenate([meta_tokens.astype(F32), x_sample.reshape(nb * n_tok, d)], axis=0)
    zh_s, q_s, kf_s, kb_s, vf_s, vb_s = _inproj_small(x_small, n1, w_in_bf, gmat, qw, kw)

    _, s_meta = _hgrn(zh_s[:N_META][None], hg_lb_logits, hg_onw,
                      jnp.zeros((1, HEADS, HEAD_W, HEAD_W), F32), shared_state=False, out_dtype=F32)
    oa_s, s_sample = _hgrn(zh_s[N_META:].reshape(nb, n_tok, HG_COLS), hg_lb_logits, hg_onw,
                           state_hgrn[0], shared_state=False, out_dtype=F32)

    x_p = x_prompt.reshape(bsz * seq, d)
    zh_p, qt_p, kf_p, kb_p, vf_p, vt_p = _inproj_prompt(
        x_p, n1, w_in_bf, gmat, qw, kw, kf_s[:N_META].reshape(N_META * HEADS, HEAD_W),
        vf_s[:N_META].reshape(N_META * HEADS, HEAD_W), seq, TOKEN_TILE)
    oa_p, s_prompt = _hgrn(zh_p.reshape(bsz, seq, HG_COLS), hg_lb_logits, hg_onw, s_meta,
                           shared_state=True, out_dtype=BF16)

    r = jnp.arange(blk)
    tb = jnp.stack([_t5_bias(r[None, :] - r[:, None], rel_bias),
                    _t5_bias(blk + r[None, :] - r[:, None], rel_bias)])
    tb = jnp.concatenate([tb, tb], axis=3)
    mrow = jnp.arange(128)
    t_meta0 = _t5_bias(N_META + r[None, :] - mrow[:, None], rel_bias)
    t_meta = jnp.stack([t_meta0, jnp.zeros_like(t_meta0)])
    t_meta = jnp.where(mrow[:, None] < N_META, t_meta, MASK_VALUE)
    t_meta = jnp.concatenate([t_meta, t_meta], axis=3)
    pad_meta = lambda a: jnp.concatenate([a, jnp.zeros((128 - N_META, w_attn), a.dtype)], axis=0)
    vmt = pad_meta(vb_s[:N_META]).T.reshape(HEADS, HEAD_W, 128)
    vmt = jnp.concatenate([vmt, jnp.ones((HEADS, VT_PAD, 128), BF16)], axis=1).reshape(VT_ROWS, 128)
    ob_p = _pattn(qt_p, kb_p.reshape(bsz, seq, w_attn), vt_p, pad_meta(kb_s[:N_META]),
                  vmt, tb, t_meta, lam_p, da_onw.reshape(HEAD_W, 1), lam_init)

    q_tok = q_s[N_META:].reshape(nb, n_tok, HEADS, 1, 2, DA_DH)
    q_maps = q_tok * jnp.eye(2, dtype=BF16).reshape(1, 1, 1, 2, 2, 1)
    n_rows = 2 * HEADS * n_tok
    qall = jnp.transpose(q_maps, (0, 2, 3, 1, 4, 5)).reshape(nb, n_rows, HEAD_W)
    prow = psize * HEADS
    pad_new = lambda a: jnp.concatenate(
        [a.reshape(nb, n_tok * HEADS, HEAD_W),
         jnp.zeros((nb, prow - n_tok * HEADS, HEAD_W), a.dtype)], axis=1)
    s_idx = jnp.tile(jnp.arange(n_tok), 2 * HEADS)
    head_of_row = jnp.repeat(jnp.arange(HEADS), 2 * n_tok)
    key_r = jnp.arange(prow) // HEADS
    own_head = head_of_row[:, None] == (jnp.arange(prow) % HEADS)[None, :]
    dist_last = (past_len + s_idx[:, None]) - (past_len - psize + key_r[None, :])
    dist_new = jnp.where(key_r[None, :] < n_tok, s_idx[:, None] - key_r[None, :], -1)
    pick = lambda t: jnp.where(own_head, t[head_of_row, jnp.arange(n_rows)], MASK_VALUE)
    t_last = pick(_t5_bias(dist_last, rel_bias))
    t_new = pick(_t5_bias(dist_new, rel_bias))
    t_mask = jnp.where(own_head, 0.0, MASK_VALUE).astype(F32)
    ob_s = _sattn(page_table, qall, pad_new(kb_s[N_META:]), pad_new(vb_s[N_META:]), t_last, t_new,
                  t_mask, lam_p, da_onw, cache_k.reshape(-1, HEAD_W), cache_v.reshape(-1, HEAD_W),
                  lam_init, n_tok)

    h_p = _merge(x_p, oa_p.reshape(bsz * seq, w_attn), ob_p.reshape(bsz * seq, w_attn),
                 n1, w_in_bf, w_ha, w_da, w_o0, TOKEN_TILE)
    y_p = _moe(h_p, n2, wr_hi, wr_lo, br, wg, wu, wd, TOKEN_TILE)
    x_s = x_small[N_META:]
    h_s = _merge(x_s, oa_s.reshape(nb * n_tok, w_attn).astype(BF16),
                 ob_s.reshape(nb * n_tok, w_attn).astype(BF16), n1, w_in_bf, w_ha, w_da, w_o0,
                 nb * n_tok)
    y_s = _moe(h_s, n2, wr_hi, wr_lo, br, wg, wu, wd, nb * n_tok)

    return (y_p.reshape(bsz, seq, d),
            y_s.reshape(nb, n_tok, d),
            kf_p.reshape(1, bsz, seq + N_META, HEADS, HEAD_W),
            vf_p.reshape(1, bsz, seq + N_META, HEADS, HEAD_W),
            s_prompt[None],
            kf_s[N_META:].reshape(1, nb, n_tok, HEADS, HEAD_W),
            vf_s[N_META:].reshape(1, nb, n_tok, HEADS, HEAD_W),
            s_sample[None])
```

```python
import functools
import math

import jax
import jax.numpy as jnp
from jax import lax
from jax.experimental import pallas as pl
from jax.experimental.pallas import tpu as pltpu

F32 = jnp.float32
BF16 = jnp.bfloat16

EPS = 1e-6
N_META = 16
HEADS = 4
HEAD_W = 128
DA_DH = 64
N_BUCKETS = 32
MAX_DISTANCE = 128
N_GROUPS = 4
EXPERTS_PER_GROUP = 4
N_EXPERTS = N_GROUPS * EXPERTS_PER_GROUP
MASK_VALUE = -1e30
HG_CHUNK = 128
HG_STEP_CHUNKS = 4
HG_BASE = 8
ATT_BLOCK = 256
VT_PAD = 16
VT_ROWS = HEADS * (HEAD_W + VT_PAD)
FAR_UNROLL = 8
PAGES_PER_STEP = 16
WIDTH = HEADS * HEAD_W
HG_COLS = 4 * WIDTH
IN_A_COLS = HG_COLS + 3 * WIDTH
TOKEN_TILE = 512
LOG2E = math.log2(math.e)
VMEM_LIMIT = 56 * 1024 * 1024


def _dot(a, b):
    return jnp.dot(a, b, preferred_element_type=F32)


def _dot_nt(a, b):
    return lax.dot_general(a, b, (((1,), (1,)), ((), ())), preferred_element_type=F32)


def _dot_tn(a, b):
    return lax.dot_general(a, b, (((0,), (0,)), ((), ())), preferred_element_type=F32)


def _sigmoid(x):
    return 0.5 + 0.5 * jnp.tanh(0.5 * x)


def _silu(x):
    y = 0.5 * x
    return y + y * jnp.tanh(y)


def _rms_rows(x, w):
    return x * lax.rsqrt(jnp.mean(x * x, axis=-1, keepdims=True) + EPS) * w


def _const_spec(shape):
    zeros = (0,) * len(shape)
    return pl.BlockSpec(shape, lambda *_: zeros)


def _inproj_values(x_ref, n1_ref, w_ref, g_ref, qw_ref, kw_ref):
    xn = _rms_rows(x_ref[...], n1_ref[...]).astype(BF16)
    g = g_ref[...]

    def group_norm(z, w):
        sq = z * z
        hi = sq.astype(BF16)
        lo = (sq - hi.astype(F32)).astype(BF16)
        ms = _dot(hi, g) + _dot(lo, g)
        return z * lax.rsqrt(ms + EPS) * w

    q0, k0, v0 = HG_COLS, HG_COLS + WIDTH, HG_COLS + 2 * WIDTH
    zh = _dot(xn, w_ref[:, 0:HG_COLS])
    qn = group_norm(_dot(xn, w_ref[:, q0:k0]), qw_ref[...])
    kn = group_norm(_dot(xn, w_ref[:, k0:v0]), kw_ref[...])
    zv = _dot(xn, w_ref[:, v0:IN_A_COLS])
    return zh, qn, kn, zv


def _inproj_small_kernel(x_ref, n1_ref, w_ref, g_ref, qw_ref, kw_ref,
                         zh_ref, q_ref, kf_ref, kb_ref, vf_ref, vb_ref):
    zh, qn, kn, zv = _inproj_values(x_ref, n1_ref, w_ref, g_ref, qw_ref, kw_ref)
    zh_ref[...] = zh
    q_ref[...] = qn.astype(BF16)
    kf_ref[...] = kn
    kb_ref[...] = kn.astype(BF16)
    vf_ref[...] = zv
    vb_ref[...] = zv.astype(BF16)


def _inproj_prompt_kernel(x_ref, n1_ref, w_ref, g_ref, qw_ref, kw_ref, km_ref, vm_ref,
                          zh_ref, qt_ref, kf_hbm, kb_ref, vf_hbm, vt_ref,
                          kst, vst, sem, msem, *, seq):
    zh, qn, kn, zv = _inproj_values(x_ref, n1_ref, w_ref, g_ref, qw_ref, kw_ref)
    tm = zh.shape[0]
    tiles = seq // tm
    i = pl.program_id(0)
    n_steps = pl.num_programs(0)
    slot = i % 2

    def batch_row0(step):
        return (step // tiles) * (N_META + seq) * HEADS

    def tile_copies(step, from_slot):
        row0 = batch_row0(step) + (N_META + (step % tiles) * tm) * HEADS
        rows = pl.ds(pl.multiple_of(row0, 8 * HEADS), tm * HEADS)
        return [pltpu.make_async_copy(kst.at[from_slot], kf_hbm.at[rows, :], sem.at[from_slot, 0]),
                pltpu.make_async_copy(vst.at[from_slot], vf_hbm.at[rows, :], sem.at[from_slot, 1])]

    zh_ref[...] = zh
    kb_ref[...] = kn.astype(BF16)
    qt = qn.T
    vt = zv.T
    for c in range(tm // ATT_BLOCK):
        cols = slice(c * ATT_BLOCK, (c + 1) * ATT_BLOCK)
        qt_ref[c] = qt[:, cols].astype(BF16)
        for h in range(HEADS):
            r0 = h * (HEAD_W + VT_PAD)
            vt_ref[c, r0:r0 + HEAD_W, :] = vt[h * HEAD_W:(h + 1) * HEAD_W, cols].astype(BF16)
            vt_ref[c, r0 + HEAD_W:r0 + HEAD_W + VT_PAD, :] = jnp.ones((VT_PAD, ATT_BLOCK), BF16)

    @pl.when(i >= 2)
    def _():
        for cp in tile_copies(i - 2, slot):
            cp.wait()

    for h in range(HEADS):
        hs = slice(h * HEAD_W, (h + 1) * HEAD_W)
        kst[slot, pl.ds(h, tm, stride=HEADS), :] = kn[:, hs]
        vst[slot, pl.ds(h, tm, stride=HEADS), :] = zv[:, hs]
    for cp in tile_copies(i, slot):
        cp.start()

    @pl.when(i % tiles == 0)
    def _():
        rows = pl.ds(pl.multiple_of(batch_row0(i), 8 * HEADS), N_META * HEADS)
        meta = [pltpu.make_async_copy(km_ref, kf_hbm.at[rows, :], msem.at[0]),
                pltpu.make_async_copy(vm_ref, vf_hbm.at[rows, :], msem.at[1])]
        for cp in meta:
            cp.start()
        for cp in meta:
            cp.wait()

    @pl.when(i == n_steps - 1)
    def _():
        @pl.when(i >= 1)
        def _():
            for cp in tile_copies(i - 1, 1 - slot):
                cp.wait()
        for cp in tile_copies(i, slot):
            cp.wait()


def _w_in_spec(d):
    return pl.BlockSpec((d, IN_A_COLS), lambda i: (0, 0))


def _inproj_small(x2d, n1, w_in_bf, gmat, qw, kw):
    n, d = x2d.shape
    full = lambda width: _const_spec((n, width))
    return pl.pallas_call(
        _inproj_small_kernel,
        grid=(1,),
        in_specs=[full(d), _const_spec((1, d)), _w_in_spec(d), _const_spec(gmat.shape),
                  _const_spec((1, WIDTH)), _const_spec((1, WIDTH))],
        out_specs=[full(HG_COLS)] + [full(WIDTH)] * 5,
        out_shape=[jax.ShapeDtypeStruct((n, HG_COLS), F32), jax.ShapeDtypeStruct((n, WIDTH), BF16),
                   jax.ShapeDtypeStruct((n, WIDTH), F32), jax.ShapeDtypeStruct((n, WIDTH), BF16),
                   jax.ShapeDtypeStruct((n, WIDTH), F32), jax.ShapeDtypeStruct((n, WIDTH), BF16)],
        compiler_params=pltpu.CompilerParams(dimension_semantics=("arbitrary",),
                                             vmem_limit_bytes=VMEM_LIMIT),
        name="inproj_small",
    )(x2d, n1, w_in_bf, gmat, qw, kw)


def _inproj_prompt(x2d, n1, w_in_bf, gmat, qw, kw, k_meta, v_meta, seq, tm):
    n, d = x2d.shape
    cache_shape = jax.ShapeDtypeStruct(((n // seq) * (N_META + seq) * HEADS, HEAD_W), F32)
    assert seq % tm == 0 and tm % ATT_BLOCK == 0
    per = tm // ATT_BLOCK
    row = lambda width: pl.BlockSpec((tm, width), lambda i: (i, 0))
    t_spec = lambda rows: pl.BlockSpec((per, rows, ATT_BLOCK), lambda i: (i, 0, 0))
    t_shape = lambda rows: jax.ShapeDtypeStruct((n // ATT_BLOCK, rows, ATT_BLOCK), BF16)
    hbm = pl.BlockSpec(memory_space=pl.ANY)
    return pl.pallas_call(
        functools.partial(_inproj_prompt_kernel, seq=seq),
        grid=(n // tm,),
        in_specs=[row(d), _const_spec((1, d)), _w_in_spec(d), _const_spec(gmat.shape),
                  _const_spec((1, WIDTH)), _const_spec((1, WIDTH)),
                  _const_spec(k_meta.shape), _const_spec(v_meta.shape)],
        out_specs=[row(HG_COLS), t_spec(WIDTH), hbm, row(WIDTH), hbm, t_spec(VT_ROWS)],
        out_shape=[jax.ShapeDtypeStruct((n, HG_COLS), F32), t_shape(WIDTH), cache_shape,
                   jax.ShapeDtypeStruct((n, WIDTH), BF16), cache_shape, t_shape(VT_ROWS)],
        scratch_shapes=[pltpu.VMEM((2, tm * HEADS, HEAD_W), F32),
                        pltpu.VMEM((2, tm * HEADS, HEAD_W), F32),
                        pltpu.SemaphoreType.DMA((2, 2)), pltpu.SemaphoreType.DMA((2,))],
        compiler_params=pltpu.CompilerParams(dimension_semantics=("arbitrary",),
                                             vmem_limit_bytes=VMEM_LIMIT),
        name="inproj",
    )(x2d, n1, w_in_bf, gmat, qw, kw, k_meta, v_meta)


def _hgrn_kernel(z_ref, lbl_ref, onw_ref, s0_ref, o_ref, sout_ref, st_scr, *, rows):
    C = HG_CHUNK
    W = HEADS * HEAD_W
    n_sub = max(rows // C, 1)
    n_valid = min(rows, C)
    c = pl.program_id(1)
    heads = [slice(h * HEAD_W, (h + 1) * HEAD_W) for h in range(HEADS)]

    @pl.when(c == 0)
    def _():
        for h in range(HEADS):
            st_scr[h] = s0_ref[0, h].T

    lg = lbl_ref[...]
    e = jnp.exp(lg - jnp.max(lg, axis=0, keepdims=True))
    lb = e[0:1] / jnp.sum(e, axis=0, keepdims=True)
    half = 0.5 * (1.0 - lb)
    onw = onw_ref[...]
    row = lax.broadcasted_iota(jnp.int32, (C, C), 0)
    col = lax.broadcasted_iota(jnp.int32, (C, C), 1)
    tri = jnp.where(col <= row, 1.0, 0.0).astype(BF16)

    def shr(x, width):
        return lax.shift_right_logical(x, int(math.log2(width)))

    def front(z, n_valid):
        hq, hf, hi, hg = z[:, 0:W], z[:, W:2 * W], z[:, 2 * W:3 * W], z[:, 3 * W:4 * W]
        q = _silu(hq)
        f = (lb + half) + half * jnp.tanh(0.5 * hf)
        k = 1.0 - f
        logf = jnp.log2(f)
        if n_valid < C:
            valid = lax.broadcasted_iota(jnp.int32, (C, W), 0) < n_valid
            k = jnp.where(valid, k, 0.0)
            logf = jnp.where(valid, logf, 0.0)
        x1 = logf.astype(BF16)
        r1 = logf - x1.astype(F32)
        x2 = r1.astype(BF16)
        x3 = (r1 - x2.astype(F32)).astype(BF16)
        b = _dot(tri, x1) + _dot(tri, x2) + _dot(tri, x3)
        return q, k, hi.astype(BF16), b, _silu(hg)

    def intra(q, k, b):
        b_last = b[C - 1:C]

        def block_row(width, r):
            b3 = b.reshape(C // width, width, W)
            return jnp.broadcast_to(b3[:, r:r + 1, :], b3.shape).reshape(C, W)

        def next_block_first(width):
            b3 = b.reshape(C // width, width, W)
            nxt = jnp.concatenate([b3[1:, 0:1, :], b_last[None]], axis=0)
            return jnp.broadcast_to(nxt, b3.shape).reshape(C, W)

        mid = block_row(HG_BASE, HG_BASE // 2)
        qe = (q * jnp.exp2(b - mid)).astype(BF16)
        ke = (k * jnp.exp2(mid - b)).astype(BF16)
        mask = (shr(row, HG_BASE) == shr(col, HG_BASE)) & (col <= row)
        scores = [jnp.where(mask, _dot_nt(qe[:, hs], ke[:, hs]), 0.0) for hs in heads]
        width = HG_BASE
        while width < n_valid:
            qe = (q * jnp.exp2(b - block_row(width, 0))).astype(BF16)
            ke = (k * jnp.exp2(next_block_first(width) - b)).astype(BF16)
            mask = ((shr(row, width) == shr(col, width) + 1)
                    & (shr(row, 2 * width) == shr(col, 2 * width)))
            scores = [jnp.where(mask, _dot_nt(qe[:, hs], ke[:, hs]), sc)
                      for hs, sc in zip(heads, scores)]
            width *= 2
        qb = (q * jnp.exp2(b)).astype(BF16)
        kd = (k * jnp.exp2(b_last - b)).astype(BF16)
        return [sc.astype(BF16) for sc in scores], qb, kd, jnp.exp2(b_last)

    fronts = []
    for u in range(n_sub):
        z = z_ref[0, u * n_valid:(u + 1) * n_valid, :]
        if n_valid < C:
            z = jnp.concatenate([z, jnp.zeros((C - n_valid, z.shape[1]), F32)], axis=0)
        fronts.append(front(z, n_valid))
    intras = [intra(q, k, b) for q, k, _, b, _ in fronts]
    for u in range(n_sub):
        _, _, v_bf, _, gate = fronts[u]
        scores, qb, kd, decay = intras[u]
        outs = []
        for h, hs in enumerate(heads):
            st = st_scr[h]
            outs.append(_dot_nt(qb[:, hs], st.astype(BF16)) + _dot(scores[h], v_bf[:, hs]))
            st_scr[h] = st * decay[:, hs] + _dot_tn(v_bf[:, hs], kd[:, hs])
        for h, hs in enumerate(heads):
            on = _rms_rows(outs[h], onw[:, hs]) * gate[:, hs]
            o_ref[0, u * n_valid:(u + 1) * n_valid, hs] = on[:n_valid].astype(o_ref.dtype)

    @pl.when(c == pl.num_programs(1) - 1)
    def _():
        for h in range(HEADS):
            sout_ref[0, h] = st_scr[h].T


def _hgrn(zh, lbl, onw, s0, *, shared_state, out_dtype):
    nseq, t, _ = zh.shape
    rows = min(t, HG_STEP_CHUNKS * HG_CHUNK)
    assert t % rows == 0 and rows % 8 == 0 and (rows <= HG_CHUNK or rows % HG_CHUNK == 0)
    s_map = (lambda b, c: (0, 0, 0, 0)) if shared_state else (lambda b, c: (b, 0, 0, 0))
    return pl.pallas_call(
        functools.partial(_hgrn_kernel, rows=rows),
        grid=(nseq, t // rows),
        in_specs=[pl.BlockSpec((1, rows, HG_COLS), lambda b, c: (b, c, 0)),
                  _const_spec(lbl.shape), _const_spec((1, WIDTH)),
                  pl.BlockSpec((1, HEADS, HEAD_W, HEAD_W), s_map)],
        out_specs=[pl.BlockSpec((1, rows, WIDTH), lambda b, c: (b, c, 0)),
                   pl.BlockSpec((1, HEADS, HEAD_W, HEAD_W), lambda b, c: (b, 0, 0, 0))],
        out_shape=[jax.ShapeDtypeStruct((nseq, t, WIDTH), out_dtype),
                   jax.ShapeDtypeStruct((nseq, HEADS, HEAD_W, HEAD_W), F32)],
        scratch_shapes=[pltpu.VMEM((HEADS, HEAD_W, HEAD_W), F32)],
        compiler_params=pltpu.CompilerParams(dimension_semantics=("parallel", "arbitrary"),
                                             vmem_limit_bytes=VMEM_LIMIT),
        name="hgrn",
    )(zh, lbl, onw, s0)


def _lambda_value(lp, lam_init):
    s1 = jnp.sum(lp[0:1] * lp[1:2], axis=1, keepdims=True)
    s2 = jnp.sum(lp[2:3] * lp[3:4], axis=1, keepdims=True)
    return jnp.exp(s1) - jnp.exp(s2) + lam_init


def _softmax_update(s, v_bf, m_ref, l_ref, acc_ref, first):
    m_cur = jnp.max(s, axis=1, keepdims=True)
    if first:
        m_new = m_cur
        p = jnp.exp2(s - m_new)
        l_ref[...] = jnp.sum(p, axis=1, keepdims=True)
        acc_ref[...] = _dot(p.astype(BF16), v_bf)
    else:
        m_prev = m_ref[...]
        m_new = jnp.maximum(m_prev, m_cur)
        alpha = jnp.exp2(m_prev - m_new)
        p = jnp.exp2(s - m_new)
        l_ref[...] = alpha * l_ref[...] + jnp.sum(p, axis=1, keepdims=True)
        acc_ref[...] = alpha * acc_ref[...] + _dot(p.astype(BF16), v_bf)
    m_ref[...] = m_new


def _pattn_kernel(qt_ref, k_ref, vt_ref, km_ref, vmt_ref, tb_ref, tm_ref, lam_ref, onw_ref, o_ref,
                  qz_scr, m_scr, acc_scr, s_scr, *, lam_init):
    blk = ATT_BLOCK
    i = pl.program_id(1)
    sub = lax.broadcasted_iota(jnp.int32, (HEAD_W, blk), 0)
    for h in range(HEADS):
        qh = qt_ref[0, h * HEAD_W:(h + 1) * HEAD_W, :]
        zero = jnp.zeros_like(qh)
        qz_scr[h] = jnp.concatenate([jnp.where(sub < DA_DH, qh, zero),
                                     jnp.where(sub >= DA_DH, qh, zero)], axis=1)

    hv = lambda h: slice(h * (HEAD_W + VT_PAD), (h + 1) * (HEAD_W + VT_PAD))

    meta_sel = jnp.minimum(i, 1)
    META = "meta"

    def key_blocks(blocks, first=False, last=False):
        def qk_raw(j, h):
            hs = slice(h * HEAD_W, (h + 1) * HEAD_W)
            if j is META:
                return _dot(km_ref[:, hs], qz_scr[h])
            off = pl.multiple_of(j * blk, blk)
            return _dot(k_ref[0, pl.ds(off, blk), hs], qz_scr[h])

        def biased(s, j, table, h):
            if j is META:
                return s + tm_ref[meta_sel, h]
            return s if table is None else s + tb_ref[table, h]

        def soft(j, h, s):
            m_cur = jnp.max(s, axis=0, keepdims=True)
            if j is META:
                m_scr[h] = m_cur
                return jnp.exp2(s - m_cur).astype(BF16), None
            m_prev = m_scr[h]
            m_new = jnp.maximum(m_prev, m_cur)
            m_scr[h] = m_new
            return jnp.exp2(s - m_new).astype(BF16), jnp.exp2(m_prev - m_new)

        def pv(j, h, p, alpha):
            if j is META:
                acc_scr[h] = _dot(vmt_ref[hv(h), :], p)
            else:
                acc_scr[h] = alpha * acc_scr[h] + _dot(vt_ref[j, hv(h), :], p)

        stages = [(j, table, h) for j, table in blocks for h in range(HEADS)]
        j_after = 0 if blocks[-1][0] is META else blocks[-1][0] + 1
        s_cur = biased(qk_raw(stages[0][0], 0) if first else s_scr[...], *stages[0])
        pend = None
        for n, (j, table, h) in enumerate(stages):
            if n + 1 < len(stages):
                jn, tn, hn = stages[n + 1]
                s_next = biased(qk_raw(jn, hn), jn, tn, hn)
            else:
                s_next = None
                if not last:
                    s_scr[...] = qk_raw(j_after, 0)
            if pend is not None:
                pv(stages[n - 1][0], stages[n - 1][2], *pend)
            pend = soft(j, h, s_cur)
            s_cur = s_next
        pv(stages[-1][0], stages[-1][2], *pend)

    key_blocks([(META, None)], first=True)
    n_far = jnp.maximum(i - 1, 0)
    done = 0
    width = 1
    while width < FAR_UNROLL:
        start = done

        @pl.when((n_far & width) != 0)
        def _(start=start, width=width):
            key_blocks([(start + u, None) for u in range(width)])

        done = done + (n_far & width)
        width *= 2
    n_rem = done

    def far_body(t, carry):
        j = n_rem + FAR_UNROLL * t
        key_blocks([(j + u, None) for u in range(FAR_UNROLL)])
        return carry

    lax.fori_loop(0, n_far // FAR_UNROLL, far_body, 0)

    @pl.when(i >= 1)
    def _():
        key_blocks([(i - 1, 1), (i, 0)], last=True)

    @pl.when(i == 0)
    def _():
        key_blocks([(0, 0)], last=True)

    lam = _lambda_value(lam_ref[...], lam_init)
    onw = onw_ref[...]
    for h in range(HEADS):
        acc = acc_scr[h]
        a = acc[:HEAD_W] / acc[HEAD_W:HEAD_W + 1]
        ot = a[:, :blk] - lam * a[:, blk:]
        ot = ot * lax.rsqrt(jnp.mean(ot * ot, axis=0, keepdims=True) + EPS) * onw * (1.0 - lam_init)
        o_ref[0, :, h * HEAD_W:(h + 1) * HEAD_W] = ot.T.astype(o_ref.dtype)


def _pattn(qt, k, vt, km, vmt, tb, tmeta, lam_p, onw_col, lam_init):
    bsz, t, w = k.shape
    blk = ATT_BLOCK
    nblk = t // blk
    assert t % blk == 0
    return pl.pallas_call(
        functools.partial(_pattn_kernel, lam_init=lam_init),
        grid=(bsz, nblk),
        in_specs=[pl.BlockSpec((1, w, blk), lambda b, i: (b * nblk + i, 0, 0)),
                  pl.BlockSpec((1, t, w), lambda b, i: (b, 0, 0)),
                  pl.BlockSpec((nblk, VT_ROWS, blk), lambda b, i: (b, 0, 0)),
                  _const_spec(km.shape), _const_spec(vmt.shape), _const_spec(tb.shape),
                  _const_spec(tmeta.shape), _const_spec(lam_p.shape), _const_spec(onw_col.shape)],
        out_specs=pl.BlockSpec((1, blk, w), lambda b, i: (b, i, 0)),
        out_shape=jax.ShapeDtypeStruct((bsz, t, w), BF16),
        scratch_shapes=[pltpu.VMEM((HEADS, HEAD_W, 2 * blk), BF16),
                        pltpu.VMEM((HEADS, 1, 2 * blk), F32),
                        pltpu.VMEM((HEADS, HEAD_W + VT_PAD, 2 * blk), F32),
                        pltpu.VMEM((blk, 2 * blk), F32)],
        compiler_params=pltpu.CompilerParams(dimension_semantics=("parallel", "arbitrary"),
                                             vmem_limit_bytes=VMEM_LIMIT),
        name="prompt_attn",
    )(qt, k, vt, km, vmt, tb, tmeta, lam_p, onw_col)


def _sattn_kernel(pt_ref, q_ref, kn_ref, vn_ref, tl_ref, tn_ref, tmask_ref, lam_ref, onw_ref,
                  ck_hbm, cv_hbm, o_ref, kbuf, vbuf, sem, m_scr, l_scr, acc_scr,
                  *, lam_init, n_tok, n_pages):
    pg = PAGES_PER_STEP
    prow = kbuf.shape[2]
    b, g = pl.program_id(0), pl.program_id(1)
    n_steps = pl.num_programs(1)
    t = b * n_steps + g
    slot = t % 2

    def page_copies(step, to_slot):
        base = (step // n_steps) * n_pages + (step % n_steps) * pg
        copies = []
        for u in range(pg):
            rows = pl.ds(pl.multiple_of(pt_ref[base + u] * prow, prow), prow)
            copies.append(pltpu.make_async_copy(ck_hbm.at[rows, :], kbuf.at[to_slot, u],
                                                sem.at[to_slot, 0]))
            copies.append(pltpu.make_async_copy(cv_hbm.at[rows, :], vbuf.at[to_slot, u],
                                                sem.at[to_slot, 1]))
        return copies

    @pl.when(t == 0)
    def _():
        for cp in page_copies(t, slot):
            cp.start()

    @pl.when(t + 1 < pl.num_programs(0) * n_steps)
    def _():
        for cp in page_copies(t + 1, 1 - slot):
            cp.start()

    q = q_ref[0]

    @pl.when(g == 0)
    def _():
        s = _dot_nt(q, kn_ref[0]) + tn_ref[...]
        _softmax_update(s, vn_ref[0], m_scr, l_scr, acc_scr, True)

    for cp in page_copies(t, slot):
        cp.wait()

    is_last = g == n_steps - 1
    tmask = tmask_ref[...]
    ss = []
    for u in range(pg):
        s = _dot_nt(q, kbuf[slot, u].astype(BF16))
        ss.append(s + (jnp.where(is_last, tl_ref[...], tmask) if u == pg - 1 else tmask))
    m_prev = m_scr[...]
    m_new = m_prev
    for s in ss:
        m_new = jnp.maximum(m_new, jnp.max(s, axis=1, keepdims=True))
    alpha = jnp.exp2(m_prev - m_new)
    l_new = alpha * l_scr[...]
    acc = alpha * acc_scr[...]
    for u in range(pg):
        p = jnp.exp2(ss[u] - m_new)
        l_new = l_new + jnp.sum(p, axis=1, keepdims=True)
        acc = acc + _dot(p.astype(BF16), vbuf[slot, u].astype(BF16))
    m_scr[...] = m_new
    l_scr[...] = l_new
    acc_scr[...] = acc

    @pl.when(is_last)
    def _():
        lam = _lambda_value(lam_ref[...], lam_init)
        onw = onw_ref[...]
        a = acc_scr[...] / l_scr[...]
        for h in range(HEADS):
            r1 = (2 * h) * n_tok
            r2 = (2 * h + 1) * n_tok
            o = a[r1:r1 + n_tok] - lam * a[r2:r2 + n_tok]
            o_ref[0, :, h * HEAD_W:(h + 1) * HEAD_W] = (
                _rms_rows(o, onw) * (1.0 - lam_init)).astype(o_ref.dtype)


def _sattn(page_table, qall, knew, vnew, tlast, tnew, tmask, lam_p, onw, ck2, cv2, lam_init, n_tok):
    nb, n_pages = page_table.shape
    pg = PAGES_PER_STEP
    assert n_pages % pg == 0
    nrow = qall.shape[1]
    prow = knew.shape[1]
    pt_flat = page_table.reshape(-1)
    per_b = lambda shape: pl.BlockSpec((1,) + shape, lambda b, g, pt: (b, 0, 0))
    const = lambda shape: pl.BlockSpec(shape, lambda b, g, pt: (0,) * len(shape))
    hbm = pl.BlockSpec(memory_space=pl.ANY)
    grid_spec = pltpu.PrefetchScalarGridSpec(
        num_scalar_prefetch=1,
        grid=(nb, n_pages // pg),
        in_specs=[per_b((nrow, HEAD_W)), per_b((prow, HEAD_W)), per_b((prow, HEAD_W)),
                  const(tlast.shape), const(tnew.shape), const(tmask.shape),
                  const(lam_p.shape), const(onw.shape), hbm, hbm],
        out_specs=per_b((n_tok, HEADS * HEAD_W)),
        scratch_shapes=[pltpu.VMEM((2, pg, prow, HEAD_W), F32), pltpu.VMEM((2, pg, prow, HEAD_W), F32),
                        pltpu.SemaphoreType.DMA((2, 2)),
                        pltpu.VMEM((nrow, 1), F32), pltpu.VMEM((nrow, 1), F32),
                        pltpu.VMEM((nrow, HEAD_W), F32)],
    )
    return pl.pallas_call(
        functools.partial(_sattn_kernel, lam_init=lam_init, n_tok=n_tok, n_pages=n_pages),
        grid_spec=grid_spec,
        out_shape=jax.ShapeDtypeStruct((nb, n_tok, HEADS * HEAD_W), F32),
        compiler_params=pltpu.CompilerParams(dimension_semantics=("arbitrary", "arbitrary"),
                                             vmem_limit_bytes=VMEM_LIMIT),
        name="sample_attn",
    )(pt_flat, qall, knew, vnew, tlast, tnew, tmask, lam_p, onw, ck2, cv2)


def _merge_kernel(x_ref, oa_ref, ob_ref, n1_ref, wg_ref, wha_ref, wda_ref, wo_ref, h_ref):
    x = x_ref[...]
    d = x.shape[1]
    xn = _rms_rows(x, n1_ref[...]).astype(BF16)
    gates = _dot(xn, wg_ref[...])
    ya = _dot(oa_ref[...], wha_ref[...])
    yb = _dot(ob_ref[...], wda_ref[...])
    merged = _sigmoid(gates[:, :d]) * ya + _sigmoid(gates[:, d:]) * yb
    h_ref[...] = x + _dot(merged.astype(BF16), wo_ref[...])


def _merge(x2d, oa, ob, n1, w_in_bf, w_ha, w_da, w_o, tm):
    n, d = x2d.shape
    row = lambda width: pl.BlockSpec((tm, width), lambda i: (i, 0))
    gate_spec = pl.BlockSpec((pl.Element(d), pl.Element(2 * d)), lambda i: (0, IN_A_COLS))
    return pl.pallas_call(
        _merge_kernel,
        grid=(n // tm,),
        in_specs=[row(d), row(WIDTH), row(WIDTH), _const_spec((1, d)), gate_spec,
                  _const_spec(w_ha.shape), _const_spec(w_da.shape), _const_spec(w_o.shape)],
        out_specs=row(d),
        out_shape=jax.ShapeDtypeStruct((n, d), F32),
        compiler_params=pltpu.CompilerParams(dimension_semantics=("parallel",),
                                             vmem_limit_bytes=VMEM_LIMIT),
        name="merge",
    )(x2d, oa, ob, n1, w_in_bf, w_ha, w_da, w_o)


def _moe_kernel(h_ref, n2_ref, wr_hi_ref, wr_lo_ref, br_ref, wg_ref, wu_ref, wd_ref, y_ref):
    h = h_ref[...]
    hn = _rms_rows(h, n2_ref[...])
    hn_hi = hn.astype(BF16)
    hn_lo = (hn - hn_hi.astype(F32)).astype(BF16)
    logits = (_dot(hn_hi, wr_hi_ref[...]) + _dot(hn_lo, wr_hi_ref[...])
              + _dot(hn_hi, wr_lo_ref[...]) + br_ref[...])
    lane = lax.broadcasted_iota(jnp.int32, logits.shape, 1)
    big = jnp.int32(1 << 20)
    ninf = -jnp.inf

    def first_lane(mask):
        return jnp.min(jnp.where(mask, lane, big), axis=1, keepdims=True)

    is_g = lane < N_GROUPS
    g_max = jnp.max(jnp.where(is_g, logits, ninf), axis=1, keepdims=True)
    g_idx = first_lane(is_g & (logits == g_max))
    g_sum = jnp.sum(jnp.where(is_g, jnp.exp(logits - g_max), 0.0), axis=1, keepdims=True)
    p_g = 1.0 / g_sum
    e_lane = lane - N_GROUPS
    in_group = ((e_lane >= 0) & (e_lane < N_EXPERTS)
                & (lax.shift_right_arithmetic(e_lane, 2) == g_idx))
    e_max = jnp.max(jnp.where(in_group, logits, ninf), axis=1, keepdims=True)
    e_exp = jnp.where(in_group, jnp.exp(logits - e_max), 0.0)
    e_prob = e_exp / jnp.sum(e_exp, axis=1, keepdims=True)
    p1 = jnp.max(jnp.where(in_group, e_prob, -1.0), axis=1, keepdims=True)
    i1 = first_lane(in_group & (e_prob == p1))
    rest = in_group & (lane != i1)
    p2 = jnp.max(jnp.where(rest, e_prob, -1.0), axis=1, keepdims=True)
    i2 = first_lane(rest & (e_prob == p2))
    denom = p1 + p2
    combine = (jnp.where(lane == i1, p_g * p1 / denom, 0.0)
               + jnp.where(lane == i2, p_g * p2 / denom, 0.0))

    acc = jnp.zeros_like(h)
    for ex in range(N_EXPERTS):
        gt = _dot(hn_hi, wg_ref[ex])
        up = _dot(hn_hi, wu_ref[ex])
        hh = _silu(gt) * up * combine[:, N_GROUPS + ex:N_GROUPS + ex + 1]
        acc = acc + _dot(hh.astype(BF16), wd_ref[ex])
    y_ref[...] = h + acc


def _moe(h2d, n2, wr_hi, wr_lo, br, wg, wu, wd, tm):
    n, d = h2d.shape
    row = pl.BlockSpec((tm, d), lambda i: (i, 0))
    single = lambda shape: pl.BlockSpec(shape, lambda i: (0,) * len(shape),
                                        pipeline_mode=pl.Buffered(1))
    return pl.pallas_call(
        _moe_kernel,
        grid=(n // tm,),
        in_specs=[row, _const_spec((1, d)), _const_spec(wr_hi.shape), _const_spec(wr_lo.shape),
                  _const_spec(br.shape), single(wg.shape), single(wu.shape), single(wd.shape)],
        out_specs=row,
        out_shape=jax.ShapeDtypeStruct((n, d), F32),
        compiler_params=pltpu.CompilerParams(dimension_semantics=("parallel",),
                                             vmem_limit_bytes=VMEM_LIMIT),
        name="moe",
    )(h2d, n2, wr_hi, wr_lo, br, wg, wu, wd)


def _t5_bias(dist, rel_bias):
    n = jnp.maximum(dist, 0)
    max_exact = N_BUCKETS // 2
    nf = jnp.maximum(n, 1).astype(F32)
    scaled = (jnp.log(nf / max_exact) / math.log(MAX_DISTANCE / max_exact)
              * (N_BUCKETS - max_exact))
    large = max_exact + jnp.floor(jnp.maximum(scaled, 0.0)).astype(jnp.int32)
    large = jnp.minimum(large, N_BUCKETS - 1)
    bucket = jnp.where(n < max_exact, n, large)
    rb = rel_bias.astype(F32)
    onehot = (bucket[..., None] == jnp.arange(N_BUCKETS)).astype(F32)
    bias = jnp.moveaxis(jnp.dot(onehot, rb, precision=lax.Precision.HIGHEST), -1, 0)
    far = rb[N_BUCKETS - 1].reshape((HEADS,) + (1,) * dist.ndim)
    return jnp.where(dist >= 0, (bias - far) * LOG2E, MASK_VALUE)


def kernel(x_prompt, x_sample, cache_k, cache_v, state_hgrn, page_table, meta_tokens, rel_bias,
           hg_lb_logits, norm1_w, w_in, hg_onorm_w, w_hg_out, q_norm_w, k_norm_w, da_lambda,
           da_onorm_w, w_da_out, w_o, norm2_w, w_router_group, b_router_group,
           w_router_expert, b_router_expert, w_gate, w_up, w_down):
    bsz, seq, d = x_prompt.shape
    nb, n_tok, _ = x_sample.shape
    depth = w_in.shape[0]
    assert depth == 1 and hg_lb_logits.shape[0] == 2
    n_pages = page_table.shape[1]
    psize = cache_k.shape[2]
    past_len = n_pages * psize
    blk = ATT_BLOCK
    assert blk >= MAX_DISTANCE and psize >= MAX_DISTANCE and n_tok <= 8
    lam_init = 0.8 - 0.6 * math.exp(-0.3 * 0)
    w_attn = WIDTH

    assert w_in.shape[2] == IN_A_COLS + 2 * d
    w_in_bf = w_in[0].astype(BF16)
    n1 = norm1_w[0].reshape(1, d)
    n2 = norm2_w[0].reshape(1, d)
    qw = (jnp.tile(q_norm_w[0], 2 * HEADS) * (DA_DH ** -0.5 * LOG2E)).reshape(1, w_attn)
    kw = jnp.tile(k_norm_w[0], 2 * HEADS).reshape(1, w_attn)
    gi = jnp.arange(w_attn) // DA_DH
    gmat = jnp.where(gi[:, None] == gi[None, :], 1.0 / DA_DH, 0.0).astype(BF16)
    hg_onw = jnp.tile(hg_onorm_w[0], HEADS).reshape(1, w_attn)
    da_onw = da_onorm_w[0].reshape(1, HEAD_W)
    lam_p = da_lambda[0].astype(F32)
    w_ha = w_hg_out[0].astype(BF16)
    w_da = w_da_out[0].astype(BF16)
    w_o0 = w_o[0].astype(BF16)
    wr = jnp.concatenate([w_router_group[0], w_router_expert[0],
                          jnp.zeros((d, 128 - N_GROUPS - N_EXPERTS), F32)], axis=1)
    wr_hi = wr.astype(BF16)
    wr_lo = (wr - wr_hi.astype(F32)).astype(BF16)
    br = jnp.concatenate([b_router_group[0], b_router_expert[0],
                          jnp.zeros((128 - N_GROUPS - N_EXPERTS,), F32)]).reshape(1, 128)
    wg = w_gate[0].astype(BF16)
    wu = w_up[0].astype(BF16)
    wd = w_down[0].astype(BF16)

    n_small = N_META + nb * n_tok
    x_small = jnp.concatenate([meta_tokens.astype(F32), x_sample.reshape(nb * n_tok, d)], axis=0)
    zh_s, q_s, kf_s, kb_s, vf_s, vb_s = _inproj_small(x_small, n1, w_in_bf, gmat, qw, kw)

    _, s_meta = _hgrn(zh_s[:N_META][None], hg_lb_logits, hg_onw,
                      jnp.zeros((1, HEADS, HEAD_W, HEAD_W), F32), shared_state=False, out_dtype=F32)
    oa_s, s_sample = _hgrn(zh_s[N_META:].reshape(nb, n_tok, HG_COLS), hg_lb_logits, hg_onw,
                           state_hgrn[0], shared_state=False, out_dtype=F32)

    x_p = x_prompt.reshape(bsz * seq, d)
    zh_p, qt_p, kf_p, kb_p, vf_p, vt_p = _inproj_prompt(
        x_p, n1, w_in_bf, gmat, qw, kw, kf_s[:N_META].reshape(N_META * HEADS, HEAD_W),
        vf_s[:N_META].reshape(N_META * HEADS, HEAD_W), seq, TOKEN_TILE)
    oa_p, s_prompt = _hgrn(zh_p.reshape(bsz, seq, HG_COLS), hg_lb_logits, hg_onw, s_meta,
                           shared_state=True, out_dtype=BF16)

    r = jnp.arange(blk)
    tb = jnp.stack([_t5_bias(r[None, :] - r[:, None], rel_bias),
                    _t5_bias(blk + r[None, :] - r[:, None], rel_bias)])
    tb = jnp.concatenate([tb, tb], axis=3)
    mrow = jnp.arange(128)
    t_meta0 = _t5_bias(N_META + r[None, :] - mrow[:, None], rel_bias)
    t_meta = jnp.stack([t_meta0, jnp.zeros_like(t_meta0)])
    t_meta = jnp.where(mrow[:, None] < N_META, t_meta, MASK_VALUE)
    t_meta = jnp.concatenate([t_meta, t_meta], axis=3)
    pad_meta = lambda a: jnp.concatenate([a, jnp.zeros((128 - N_META, w_attn), a.dtype)], axis=0)
    vmt = pad_meta(vb_s[:N_META]).T.reshape(HEADS, HEAD_W, 128)
    vmt = jnp.concatenate([vmt, jnp.ones((HEADS, VT_PAD, 128), BF16)], axis=1).reshape(VT_ROWS, 128)
    ob_p = _pattn(qt_p, kb_p.reshape(bsz, seq, w_attn), vt_p, pad_meta(kb_s[:N_META]),
                  vmt, tb, t_meta, lam_p, da_onw.reshape(HEAD_W, 1), lam_init)

    q_tok = q_s[N_META:].reshape(nb, n_tok, HEADS, 1, 2, DA_DH)
    q_maps = q_tok * jnp.eye(2, dtype=BF16).reshape(1, 1, 1, 2, 2, 1)
    n_rows = 2 * HEADS * n_tok
    qall = jnp.transpose(q_maps, (0, 2, 3, 1, 4, 5)).reshape(nb, n_rows, HEAD_W)
    prow = psize * HEADS
    pad_new = lambda a: jnp.concatenate(
        [a.reshape(nb, n_tok * HEADS, HEAD_W),
         jnp.zeros((nb, prow - n_tok * HEADS, HEAD_W), a.dtype)], axis=1)
    s_idx = jnp.tile(jnp.arange(n_tok), 2 * HEADS)
    head_of_row = jnp.repeat(jnp.arange(HEADS), 2 * n_tok)
    key_r = jnp.arange(prow) // HEADS
    own_head = head_of_row[:, None] == (jnp.arange(prow) % HEADS)[None, :]
    dist_last = (past_len + s_idx[:, None]) - (past_len - psize + key_r[None, :])
    dist_new = jnp.where(key_r[None, :] < n_tok, s_idx[:, None] - key_r[None, :], -1)
    pick = lambda t: jnp.where(own_head, t[head_of_row, jnp.arange(n_rows)], MASK_VALUE)
    t_last = pick(_t5_bias(dist_last, rel_bias))
    t_new = pick(_t5_bias(dist_new, rel_bias))
    t_mask = jnp.where(own_head, 0.0, MASK_VALUE).astype(F32)
    ob_s = _sattn(page_table, qall, pad_new(kb_s[N_META:]), pad_new(vb_s[N_META:]), t_last, t_new,
                  t_mask, lam_p, da_onw, cache_k.reshape(-1, HEAD_W), cache_v.reshape(-1, HEAD_W),
                  lam_init, n_tok)

    h_p = _merge(x_p, oa_p.reshape(bsz * seq, w_attn), ob_p.reshape(bsz * seq, w_attn),
                 n1, w_in_bf, w_ha, w_da, w_o0, TOKEN_TILE)
    y_p = _moe(h_p, n2, wr_hi, wr_lo, br, wg, wu, wd, TOKEN_TILE)
    x_s = x_small[N_META:]
    h_s = _merge(x_s, oa_s.reshape(nb * n_tok, w_attn).astype(BF16),
                 ob_s.reshape(nb * n_tok, w_attn).astype(BF16), n1, w_in_bf, w_ha, w_da, w_o0,
                 nb * n_tok)
    y_s = _moe(h_s, n2, wr_hi, wr_lo, br, wg, wu, wd, nb * n_tok)

    return (y_p.reshape(bsz, seq, d),
            y_s.reshape(nb, n_tok, d),
            kf_p.reshape(1, bsz, seq + N_META, HEADS, HEAD_W),
            vf_p.reshape(1, bsz, seq + N_META, HEADS, HEAD_W),
            s_prompt[None],
            kf_s[N_META:].reshape(1, nb, n_tok, HEADS, HEAD_W),
            vf_s[N_META:].reshape(1, nb, n_tok, HEADS, HEAD_W),
            s_sample[None])
```

```python
import functools
import math

import jax
import jax.numpy as jnp
from jax import lax
from jax.experimental import pallas as pl
from jax.experimental.pallas import tpu as pltpu

F32 = jnp.float32
BF16 = jnp.bfloat16

EPS = 1e-6
N_META = 16
HEADS = 4
HEAD_W = 128
DA_DH = 64
N_BUCKETS = 32
MAX_DISTANCE = 128
N_GROUPS = 4
EXPERTS_PER_GROUP = 4
N_EXPERTS = N_GROUPS * EXPERTS_PER_GROUP
MASK_VALUE = -1e30
HG_CHUNK = 128
HG_STEP_CHUNKS = 4
HG_BASE = 8
ATT_BLOCK = 256
VT_PAD = 16
VT_ROWS = HEADS * (HEAD_W + VT_PAD)
FAR_UNROLL = 8
PAGES_PER_STEP = 16
PAGE_SLOTS = 3
WIDTH = HEADS * HEAD_W
HG_COLS = 4 * WIDTH
IN_A_COLS = HG_COLS + 3 * WIDTH
TOKEN_TILE = 512
LOG2E = math.log2(math.e)
VMEM_LIMIT = 56 * 1024 * 1024


def _dot(a, b):
    return jnp.dot(a, b, preferred_element_type=F32)


def _dot_nt(a, b):
    return lax.dot_general(a, b, (((1,), (1,)), ((), ())), preferred_element_type=F32)


def _dot_tn(a, b):
    return lax.dot_general(a, b, (((0,), (0,)), ((), ())), preferred_element_type=F32)


def _sigmoid(x):
    return 0.5 + 0.5 * jnp.tanh(0.5 * x)


def _silu(x):
    y = 0.5 * x
    return y + y * jnp.tanh(y)


def _rms_rows(x, w):
    return x * lax.rsqrt(jnp.mean(x * x, axis=-1, keepdims=True) + EPS) * w


def _const_spec(shape):
    zeros = (0,) * len(shape)
    return pl.BlockSpec(shape, lambda *_: zeros)


def _inproj_values(x_ref, n1_ref, w_ref, g_ref, qw_ref, kw_ref):
    xn = _rms_rows(x_ref[...], n1_ref[...]).astype(BF16)
    g = g_ref[...]

    def group_norm(z, w):
        sq = z * z
        hi = sq.astype(BF16)
        lo = (sq - hi.astype(F32)).astype(BF16)
        ms = _dot(hi, g) + _dot(lo, g)
        return z * lax.rsqrt(ms + EPS) * w

    q0, k0, v0 = HG_COLS, HG_COLS + WIDTH, HG_COLS + 2 * WIDTH
    zh = _dot(xn, w_ref[:, 0:HG_COLS])
    qn = group_norm(_dot(xn, w_ref[:, q0:k0]), qw_ref[...])
    kn = group_norm(_dot(xn, w_ref[:, k0:v0]), kw_ref[...])
    zv = _dot(xn, w_ref[:, v0:IN_A_COLS])
    return zh, qn, kn, zv


def _inproj_small_kernel(x_ref, n1_ref, w_ref, g_ref, qw_ref, kw_ref,
                         zh_ref, q_ref, kf_ref, kb_ref, vf_ref, vb_ref):
    zh, qn, kn, zv = _inproj_values(x_ref, n1_ref, w_ref, g_ref, qw_ref, kw_ref)
    zh_ref[...] = zh
    q_ref[...] = qn.astype(BF16)
    kf_ref[...] = kn
    kb_ref[...] = kn.astype(BF16)
    vf_ref[...] = zv
    vb_ref[...] = zv.astype(BF16)


def _inproj_prompt_kernel(x_ref, n1_ref, w_ref, g_ref, qw_ref, kw_ref, km_ref, vm_ref,
                          zh_ref, qt_ref, kf_hbm, kb_ref, vf_hbm, vt_ref,
                          kst, vst, sem, msem, *, seq):
    zh, qn, kn, zv = _inproj_values(x_ref, n1_ref, w_ref, g_ref, qw_ref, kw_ref)
    tm = zh.shape[0]
    tiles = seq // tm
    i = pl.program_id(0)
    n_steps = pl.num_programs(0)
    slot = i % 2

    def batch_row0(step):
        return (step // tiles) * (N_META + seq) * HEADS

    def tile_copies(step, from_slot):
        row0 = batch_row0(step) + (N_META + (step % tiles) * tm) * HEADS
        rows = pl.ds(pl.multiple_of(row0, 8 * HEADS), tm * HEADS)
        return [pltpu.make_async_copy(kst.at[from_slot], kf_hbm.at[rows, :], sem.at[from_slot, 0]),
                pltpu.make_async_copy(vst.at[from_slot], vf_hbm.at[rows, :], sem.at[from_slot, 1])]

    zh_ref[...] = zh
    kb_ref[...] = kn.astype(BF16)
    qt = qn.T
    vt = zv.T
    for c in range(tm // ATT_BLOCK):
        cols = slice(c * ATT_BLOCK, (c + 1) * ATT_BLOCK)
        qt_ref[c] = qt[:, cols].astype(BF16)
        for h in range(HEADS):
            r0 = h * (HEAD_W + VT_PAD)
            vt_ref[c, r0:r0 + HEAD_W, :] = vt[h * HEAD_W:(h + 1) * HEAD_W, cols].astype(BF16)
            vt_ref[c, r0 + HEAD_W:r0 + HEAD_W + VT_PAD, :] = jnp.ones((VT_PAD, ATT_BLOCK), BF16)

    @pl.when(i >= 2)
    def _():
        for cp in tile_copies(i - 2, slot):
            cp.wait()

    for h in range(HEADS):
        hs = slice(h * HEAD_W, (h + 1) * HEAD_W)
        kst[slot, pl.ds(h, tm, stride=HEADS), :] = kn[:, hs]
        vst[slot, pl.ds(h, tm, stride=HEADS), :] = zv[:, hs]
    for cp in tile_copies(i, slot):
        cp.start()

    @pl.when(i % tiles == 0)
    def _():
        rows = pl.ds(pl.multiple_of(batch_row0(i), 8 * HEADS), N_META * HEADS)
        meta = [pltpu.make_async_copy(km_ref, kf_hbm.at[rows, :], msem.at[0]),
                pltpu.make_async_copy(vm_ref, vf_hbm.at[rows, :], msem.at[1])]
        for cp in meta:
            cp.start()
        for cp in meta:
            cp.wait()

    @pl.when(i == n_steps - 1)
    def _():
        @pl.when(i >= 1)
        def _():
            for cp in tile_copies(i - 1, 1 - slot):
                cp.wait()
        for cp in tile_copies(i, slot):
            cp.wait()


def _w_in_spec(d):
    return pl.BlockSpec((d, IN_A_COLS), lambda i: (0, 0))


def _inproj_small(x2d, n1, w_in_bf, gmat, qw, kw):
    n, d = x2d.shape
    full = lambda width: _const_spec((n, width))
    return pl.pallas_call(
        _inproj_small_kernel,
        grid=(1,),
        in_specs=[full(d), _const_spec((1, d)), _w_in_spec(d), _const_spec(gmat.shape),
                  _const_spec((1, WIDTH)), _const_spec((1, WIDTH))],
        out_specs=[full(HG_COLS)] + [full(WIDTH)] * 5,
        out_shape=[jax.ShapeDtypeStruct((n, HG_COLS), F32), jax.ShapeDtypeStruct((n, WIDTH), BF16),
                   jax.ShapeDtypeStruct((n, WIDTH), F32), jax.ShapeDtypeStruct((n, WIDTH), BF16),
                   jax.ShapeDtypeStruct((n, WIDTH), F32), jax.ShapeDtypeStruct((n, WIDTH), BF16)],
        compiler_params=pltpu.CompilerParams(dimension_semantics=("arbitrary",),
                                             vmem_limit_bytes=VMEM_LIMIT),
        name="inproj_small",
    )(x2d, n1, w_in_bf, gmat, qw, kw)


def _inproj_prompt(x2d, n1, w_in_bf, gmat, qw, kw, k_meta, v_meta, seq, tm):
    n, d = x2d.shape
    cache_shape = jax.ShapeDtypeStruct(((n // seq) * (N_META + seq) * HEADS, HEAD_W), F32)
    assert seq % tm == 0 and tm % ATT_BLOCK == 0
    per = tm // ATT_BLOCK
    row = lambda width: pl.BlockSpec((tm, width), lambda i: (i, 0))
    t_spec = lambda rows: pl.BlockSpec((per, rows, ATT_BLOCK), lambda i: (i, 0, 0))
    t_shape = lambda rows: jax.ShapeDtypeStruct((n // ATT_BLOCK, rows, ATT_BLOCK), BF16)
    hbm = pl.BlockSpec(memory_space=pl.ANY)
    return pl.pallas_call(
        functools.partial(_inproj_prompt_kernel, seq=seq),
        grid=(n // tm,),
        in_specs=[row(d), _const_spec((1, d)), _w_in_spec(d), _const_spec(gmat.shape),
                  _const_spec((1, WIDTH)), _const_spec((1, WIDTH)),
                  _const_spec(k_meta.shape), _const_spec(v_meta.shape)],
        out_specs=[row(HG_COLS), t_spec(WIDTH), hbm, row(WIDTH), hbm, t_spec(VT_ROWS)],
        out_shape=[jax.ShapeDtypeStruct((n, HG_COLS), F32), t_shape(WIDTH), cache_shape,
                   jax.ShapeDtypeStruct((n, WIDTH), BF16), cache_shape, t_shape(VT_ROWS)],
        scratch_shapes=[pltpu.VMEM((2, tm * HEADS, HEAD_W), F32),
                        pltpu.VMEM((2, tm * HEADS, HEAD_W), F32),
                        pltpu.SemaphoreType.DMA((2, 2)), pltpu.SemaphoreType.DMA((2,))],
        compiler_params=pltpu.CompilerParams(dimension_semantics=("arbitrary",),
                                             vmem_limit_bytes=VMEM_LIMIT),
        name="inproj",
    )(x2d, n1, w_in_bf, gmat, qw, kw, k_meta, v_meta)


def _hgrn_kernel(z_ref, lbl_ref, onw_ref, s0_ref, o_ref, sout_ref, st_scr, *, rows):
    C = HG_CHUNK
    W = HEADS * HEAD_W
    n_sub = max(rows // C, 1)
    n_valid = min(rows, C)
    c = pl.program_id(1)
    heads = [slice(h * HEAD_W, (h + 1) * HEAD_W) for h in range(HEADS)]

    @pl.when(c == 0)
    def _():
        for h in range(HEADS):
            st_scr[h] = s0_ref[0, h].T

    lg = lbl_ref[...]
    e = jnp.exp(lg - jnp.max(lg, axis=0, keepdims=True))
    lb = e[0:1] / jnp.sum(e, axis=0, keepdims=True)
    half = 0.5 * (1.0 - lb)
    onw = onw_ref[...]
    row = lax.broadcasted_iota(jnp.int32, (C, C), 0)
    col = lax.broadcasted_iota(jnp.int32, (C, C), 1)
    tri = jnp.where(col <= row, 1.0, 0.0).astype(BF16)

    def shr(x, width):
        return lax.shift_right_logical(x, int(math.log2(width)))

    def front(z, n_valid):
        hq, hf, hi, hg = z[:, 0:W], z[:, W:2 * W], z[:, 2 * W:3 * W], z[:, 3 * W:4 * W]
        q = _silu(hq)
        f = (lb + half) + half * jnp.tanh(0.5 * hf)
        k = 1.0 - f
        logf = jnp.log2(f)
        if n_valid < C:
            valid = lax.broadcasted_iota(jnp.int32, (C, W), 0) < n_valid
            k = jnp.where(valid, k, 0.0)
            logf = jnp.where(valid, logf, 0.0)
        x1 = logf.astype(BF16)
        r1 = logf - x1.astype(F32)
        x2 = r1.astype(BF16)
        x3 = (r1 - x2.astype(F32)).astype(BF16)
        b = _dot(tri, x1) + _dot(tri, x2) + _dot(tri, x3)
        return q, k, hi.astype(BF16), b, _silu(hg)

    def intra(q, k, b):
        b_last = b[C - 1:C]

        def block_row(width, r):
            b3 = b.reshape(C // width, width, W)
            return jnp.broadcast_to(b3[:, r:r + 1, :], b3.shape).reshape(C, W)

        def next_block_first(width):
            b3 = b.reshape(C // width, width, W)
            nxt = jnp.concatenate([b3[1:, 0:1, :], b_last[None]], axis=0)
            return jnp.broadcast_to(nxt, b3.shape).reshape(C, W)

        mid = block_row(HG_BASE, HG_BASE // 2)
        qe = (q * jnp.exp2(b - mid)).astype(BF16)
        ke = (k * jnp.exp2(mid - b)).astype(BF16)
        mask = (shr(row, HG_BASE) == shr(col, HG_BASE)) & (col <= row)
        scores = [jnp.where(mask, _dot_nt(qe[:, hs], ke[:, hs]), 0.0) for hs in heads]
        width = HG_BASE
        while width < n_valid:
            qe = (q * jnp.exp2(b - block_row(width, 0))).astype(BF16)
            ke = (k * jnp.exp2(next_block_first(width) - b)).astype(BF16)
            mask = ((shr(row, width) == shr(col, width) + 1)
                    & (shr(row, 2 * width) == shr(col, 2 * width)))
            scores = [jnp.where(mask, _dot_nt(qe[:, hs], ke[:, hs]), sc)
                      for hs, sc in zip(heads, scores)]
            width *= 2
        qb = (q * jnp.exp2(b)).astype(BF16)
        kd = (k * jnp.exp2(b_last - b)).astype(BF16)
        return [sc.astype(BF16) for sc in scores], qb, kd, jnp.exp2(b_last)

    fronts = []
    for u in range(n_sub):
        z = z_ref[0, u * n_valid:(u + 1) * n_valid, :]
        if n_valid < C:
            z = jnp.concatenate([z, jnp.zeros((C - n_valid, z.shape[1]), F32)], axis=0)
        fronts.append(front(z, n_valid))
    intras = [intra(q, k, b) for q, k, _, b, _ in fronts]
    for u in range(n_sub):
        _, _, v_bf, _, gate = fronts[u]
        scores, qb, kd, decay = intras[u]
        outs = []
        for h, hs in enumerate(heads):
            st = st_scr[h]
            outs.append(_dot_nt(qb[:, hs], st.astype(BF16)) + _dot(scores[h], v_bf[:, hs]))
            st_scr[h] = st * decay[:, hs] + _dot_tn(v_bf[:, hs], kd[:, hs])
        for h, hs in enumerate(heads):
            on = _rms_rows(outs[h], onw[:, hs]) * gate[:, hs]
            o_ref[0, u * n_valid:(u + 1) * n_valid, hs] = on[:n_valid].astype(o_ref.dtype)

    @pl.when(c == pl.num_programs(1) - 1)
    def _():
        for h in range(HEADS):
            sout_ref[0, h] = st_scr[h].T


def _hgrn(zh, lbl, onw, s0, *, shared_state, out_dtype):
    nseq, t, _ = zh.shape
    rows = min(t, HG_STEP_CHUNKS * HG_CHUNK)
    assert t % rows == 0 and rows % 8 == 0 and (rows <= HG_CHUNK or rows % HG_CHUNK == 0)
    s_map = (lambda b, c: (0, 0, 0, 0)) if shared_state else (lambda b, c: (b, 0, 0, 0))
    return pl.pallas_call(
        functools.partial(_hgrn_kernel, rows=rows),
        grid=(nseq, t // rows),
        in_specs=[pl.BlockSpec((1, rows, HG_COLS), lambda b, c: (b, c, 0)),
                  _const_spec(lbl.shape), _const_spec((1, WIDTH)),
                  pl.BlockSpec((1, HEADS, HEAD_W, HEAD_W), s_map)],
        out_specs=[pl.BlockSpec((1, rows, WIDTH), lambda b, c: (b, c, 0)),
                   pl.BlockSpec((1, HEADS, HEAD_W, HEAD_W), lambda b, c: (b, 0, 0, 0))],
        out_shape=[jax.ShapeDtypeStruct((nseq, t, WIDTH), out_dtype),
                   jax.ShapeDtypeStruct((nseq, HEADS, HEAD_W, HEAD_W), F32)],
        scratch_shapes=[pltpu.VMEM((HEADS, HEAD_W, HEAD_W), F32)],
        compiler_params=pltpu.CompilerParams(dimension_semantics=("parallel", "arbitrary"),
                                             vmem_limit_bytes=VMEM_LIMIT),
        name="hgrn",
    )(zh, lbl, onw, s0)


def _lambda_value(lp, lam_init):
    s1 = jnp.sum(lp[0:1] * lp[1:2], axis=1, keepdims=True)
    s2 = jnp.sum(lp[2:3] * lp[3:4], axis=1, keepdims=True)
    return jnp.exp(s1) - jnp.exp(s2) + lam_init


def _softmax_update(s, v_bf, m_ref, l_ref, acc_ref, first):
    m_cur = jnp.max(s, axis=1, keepdims=True)
    if first:
        m_new = m_cur
        p = jnp.exp2(s - m_new)
        l_ref[...] = jnp.sum(p, axis=1, keepdims=True)
        acc_ref[...] = _dot(p.astype(BF16), v_bf)
    else:
        m_prev = m_ref[...]
        m_new = jnp.maximum(m_prev, m_cur)
        alpha = jnp.exp2(m_prev - m_new)
        p = jnp.exp2(s - m_new)
        l_ref[...] = alpha * l_ref[...] + jnp.sum(p, axis=1, keepdims=True)
        acc_ref[...] = alpha * acc_ref[...] + _dot(p.astype(BF16), v_bf)
    m_ref[...] = m_new


def _pattn_kernel(qt_ref, k_ref, vt_ref, km_ref, vmt_ref, tb_ref, tm_ref, lam_ref, onw_ref, o_ref,
                  qz_scr, m_scr, acc_scr, s_scr, *, lam_init):
    blk = ATT_BLOCK
    i = pl.program_id(1)
    sub = lax.broadcasted_iota(jnp.int32, (HEAD_W, blk), 0)
    for h in range(HEADS):
        qh = qt_ref[0, h * HEAD_W:(h + 1) * HEAD_W, :]
        zero = jnp.zeros_like(qh)
        qz_scr[h] = jnp.concatenate([jnp.where(sub < DA_DH, qh, zero),
                                     jnp.where(sub >= DA_DH, qh, zero)], axis=1)

    hv = lambda h: slice(h * (HEAD_W + VT_PAD), (h + 1) * (HEAD_W + VT_PAD))

    meta_sel = jnp.minimum(i, 1)
    META = "meta"

    def key_blocks(blocks, first=False, last=False):
        def qk_raw(j, h):
            hs = slice(h * HEAD_W, (h + 1) * HEAD_W)
            if j is META:
                return _dot(km_ref[:, hs], qz_scr[h])
            off = pl.multiple_of(j * blk, blk)
            return _dot(k_ref[0, pl.ds(off, blk), hs], qz_scr[h])

        def biased(s, j, table, h):
            if j is META:
                return s + tm_ref[meta_sel, h]
            return s if table is None else s + tb_ref[table, h]

        def soft(j, h, s):
            m_cur = jnp.max(s, axis=0, keepdims=True)
            if j is META:
                m_scr[h] = m_cur
                return jnp.exp2(s - m_cur).astype(BF16), None
            m_prev = m_scr[h]
            m_new = jnp.maximum(m_prev, m_cur)
            m_scr[h] = m_new
            return jnp.exp2(s - m_new).astype(BF16), jnp.exp2(m_prev - m_new)

        def pv(j, h, p, alpha):
            if j is META:
                acc_scr[h] = _dot(vmt_ref[hv(h), :], p)
            else:
                acc_scr[h] = alpha * acc_scr[h] + _dot(vt_ref[j, hv(h), :], p)

        stages = [(j, table, h) for j, table in blocks for h in range(HEADS)]
        j_after = 0 if blocks[-1][0] is META else blocks[-1][0] + 1
        s_cur = biased(qk_raw(stages[0][0], 0) if first else s_scr[...], *stages[0])
        pend = None
        for n, (j, table, h) in enumerate(stages):
            if n + 1 < len(stages):
                jn, tn, hn = stages[n + 1]
                s_next = biased(qk_raw(jn, hn), jn, tn, hn)
            else:
                s_next = None
                if not last:
                    s_scr[...] = qk_raw(j_after, 0)
            if pend is not None:
                pv(stages[n - 1][0], stages[n - 1][2], *pend)
            pend = soft(j, h, s_cur)
            s_cur = s_next
        pv(stages[-1][0], stages[-1][2], *pend)

    key_blocks([(META, None)], first=True)
    n_far = jnp.maximum(i - 1, 0)
    done = 0
    width = 1
    while width < FAR_UNROLL:
        start = done

        @pl.when((n_far & width) != 0)
        def _(start=start, width=width):
            key_blocks([(start + u, None) for u in range(width)])

        done = done + (n_far & width)
        width *= 2
    n_rem = done

    def far_body(t, carry):
        j = n_rem + FAR_UNROLL * t
        key_blocks([(j + u, None) for u in range(FAR_UNROLL)])
        return carry

    lax.fori_loop(0, n_far // FAR_UNROLL, far_body, 0)

    @pl.when(i >= 1)
    def _():
        key_blocks([(i - 1, 1), (i, 0)], last=True)

    @pl.when(i == 0)
    def _():
        key_blocks([(0, 0)], last=True)

    lam = _lambda_value(lam_ref[...], lam_init)
    onw = onw_ref[...]
    for h in range(HEADS):
        acc = acc_scr[h]
        a = acc[:HEAD_W] / acc[HEAD_W:HEAD_W + 1]
        ot = a[:, :blk] - lam * a[:, blk:]
        ot = ot * lax.rsqrt(jnp.mean(ot * ot, axis=0, keepdims=True) + EPS) * onw * (1.0 - lam_init)
        o_ref[0, :, h * HEAD_W:(h + 1) * HEAD_W] = ot.T.astype(o_ref.dtype)


def _pattn(qt, k, vt, km, vmt, tb, tmeta, lam_p, onw_col, lam_init):
    bsz, t, w = k.shape
    blk = ATT_BLOCK
    nblk = t // blk
    assert t % blk == 0
    return pl.pallas_call(
        functools.partial(_pattn_kernel, lam_init=lam_init),
        grid=(bsz, nblk),
        in_specs=[pl.BlockSpec((1, w, blk), lambda b, i: (b * nblk + i, 0, 0)),
                  pl.BlockSpec((1, t, w), lambda b, i: (b, 0, 0)),
                  pl.BlockSpec((nblk, VT_ROWS, blk), lambda b, i: (b, 0, 0)),
                  _const_spec(km.shape), _const_spec(vmt.shape), _const_spec(tb.shape),
                  _const_spec(tmeta.shape), _const_spec(lam_p.shape), _const_spec(onw_col.shape)],
        out_specs=pl.BlockSpec((1, blk, w), lambda b, i: (b, i, 0)),
        out_shape=jax.ShapeDtypeStruct((bsz, t, w), BF16),
        scratch_shapes=[pltpu.VMEM((HEADS, HEAD_W, 2 * blk), BF16),
                        pltpu.VMEM((HEADS, 1, 2 * blk), F32),
                        pltpu.VMEM((HEADS, HEAD_W + VT_PAD, 2 * blk), F32),
                        pltpu.VMEM((blk, 2 * blk), F32)],
        compiler_params=pltpu.CompilerParams(dimension_semantics=("parallel", "arbitrary"),
                                             vmem_limit_bytes=VMEM_LIMIT),
        name="prompt_attn",
    )(qt, k, vt, km, vmt, tb, tmeta, lam_p, onw_col)


def _sattn_kernel(pt_ref, q_ref, kn_ref, vn_ref, tl_ref, tn_ref, tmask_ref, lam_ref, onw_ref,
                  ck_hbm, cv_hbm, o_ref, kbuf, vbuf, sem, m_scr, l_scr, acc_scr,
                  *, lam_init, n_tok, n_pages):
    pg = PAGES_PER_STEP
    prow = kbuf.shape[2]
    b, g = pl.program_id(0), pl.program_id(1)
    n_steps = pl.num_programs(1)
    t = b * n_steps + g
    n_total = pl.num_programs(0) * n_steps
    slot = t % PAGE_SLOTS

    def page_copies(step, to_slot):
        base = (step // n_steps) * n_pages + (step % n_steps) * pg
        copies = []
        for u in range(pg):
            rows = pl.ds(pl.multiple_of(pt_ref[base + u] * prow, prow), prow)
            copies.append(pltpu.make_async_copy(ck_hbm.at[rows, :], kbuf.at[to_slot, u],
                                                sem.at[to_slot, 0]))
            copies.append(pltpu.make_async_copy(cv_hbm.at[rows, :], vbuf.at[to_slot, u],
                                                sem.at[to_slot, 1]))
        return copies

    ahead = PAGE_SLOTS - 1

    @pl.when(t == 0)
    def _():
        for d in range(ahead):
            @pl.when(d < n_total)
            def _(d=d):
                for cp in page_copies(d, d):
                    cp.start()

    @pl.when(t + ahead < n_total)
    def _():
        for cp in page_copies(t + ahead, (t + ahead) % PAGE_SLOTS):
            cp.start()

    q = q_ref[0]

    @pl.when(g == 0)
    def _():
        s = _dot_nt(q, kn_ref[0]) + tn_ref[...]
        _softmax_update(s, vn_ref[0], m_scr, l_scr, acc_scr, True)

    for cp in page_copies(t, slot):
        cp.wait()

    is_last = g == n_steps - 1
    tmask = tmask_ref[...]
    ss = []
    for u in range(pg):
        s = _dot_nt(q, kbuf[slot, u].astype(BF16))
        ss.append(s + (jnp.where(is_last, tl_ref[...], tmask) if u == pg - 1 else tmask))
    m_prev = m_scr[...]
    m_new = m_prev
    for s in ss:
        m_new = jnp.maximum(m_new, jnp.max(s, axis=1, keepdims=True))
    alpha = jnp.exp2(m_prev - m_new)
    l_new = alpha * l_scr[...]
    acc = alpha * acc_scr[...]
    for u in range(pg):
        p = jnp.exp2(ss[u] - m_new)
        l_new = l_new + jnp.sum(p, axis=1, keepdims=True)
        acc = acc + _dot(p.astype(BF16), vbuf[slot, u].astype(BF16))
    m_scr[...] = m_new
    l_scr[...] = l_new
    acc_scr[...] = acc

    @pl.when(is_last)
    def _():
        lam = _lambda_value(lam_ref[...], lam_init)
        onw = onw_ref[...]
        a = acc_scr[...] / l_scr[...]
        for h in range(HEADS):
            r1 = (2 * h) * n_tok
            r2 = (2 * h + 1) * n_tok
            o = a[r1:r1 + n_tok] - lam * a[r2:r2 + n_tok]
            o_ref[0, :, h * HEAD_W:(h + 1) * HEAD_W] = (
                _rms_rows(o, onw) * (1.0 - lam_init)).astype(o_ref.dtype)


def _sattn(page_table, qall, knew, vnew, tlast, tnew, tmask, lam_p, onw, ck2, cv2, lam_init, n_tok):
    nb, n_pages = page_table.shape
    pg = PAGES_PER_STEP
    assert n_pages % pg == 0
    nrow = qall.shape[1]
    prow = knew.shape[1]
    pt_flat = page_table.reshape(-1)
    per_b = lambda shape: pl.BlockSpec((1,) + shape, lambda b, g, pt: (b, 0, 0))
    const = lambda shape: pl.BlockSpec(shape, lambda b, g, pt: (0,) * len(shape))
    hbm = pl.BlockSpec(memory_space=pl.ANY)
    grid_spec = pltpu.PrefetchScalarGridSpec(
        num_scalar_prefetch=1,
        grid=(nb, n_pages // pg),
        in_specs=[per_b((nrow, HEAD_W)), per_b((prow, HEAD_W)), per_b((prow, HEAD_W)),
                  const(tlast.shape), const(tnew.shape), const(tmask.shape),
                  const(lam_p.shape), const(onw.shape), hbm, hbm],
        out_specs=per_b((n_tok, HEADS * HEAD_W)),
        scratch_shapes=[pltpu.VMEM((PAGE_SLOTS, pg, prow, HEAD_W), F32),
                        pltpu.VMEM((PAGE_SLOTS, pg, prow, HEAD_W), F32),
                        pltpu.SemaphoreType.DMA((PAGE_SLOTS, 2)),
                        pltpu.VMEM((nrow, 1), F32), pltpu.VMEM((nrow, 1), F32),
                        pltpu.VMEM((nrow, HEAD_W), F32)],
    )
    return pl.pallas_call(
        functools.partial(_sattn_kernel, lam_init=lam_init, n_tok=n_tok, n_pages=n_pages),
        grid_spec=grid_spec,
        out_shape=jax.ShapeDtypeStruct((nb, n_tok, HEADS * HEAD_W), F32),
        compiler_params=pltpu.CompilerParams(dimension_semantics=("arbitrary", "arbitrary"),
                                             vmem_limit_bytes=VMEM_LIMIT),
        name="sample_attn",
    )(pt_flat, qall, knew, vnew, tlast, tnew, tmask, lam_p, onw, ck2, cv2)


def _merge_kernel(x_ref, oa_ref, ob_ref, n1_ref, wg_ref, wha_ref, wda_ref, wo_ref, h_ref):
    x = x_ref[...]
    d = x.shape[1]
    xn = _rms_rows(x, n1_ref[...]).astype(BF16)
    gates = _dot(xn, wg_ref[...])
    ya = _dot(oa_ref[...], wha_ref[...])
    yb = _dot(ob_ref[...], wda_ref[...])
    merged = _sigmoid(gates[:, :d]) * ya + _sigmoid(gates[:, d:]) * yb
    h_ref[...] = x + _dot(merged.astype(BF16), wo_ref[...])


def _merge(x2d, oa, ob, n1, w_in_bf, w_ha, w_da, w_o, tm):
    n, d = x2d.shape
    row = lambda width: pl.BlockSpec((tm, width), lambda i: (i, 0))
    gate_spec = pl.BlockSpec((pl.Element(d), pl.Element(2 * d)), lambda i: (0, IN_A_COLS))
    return pl.pallas_call(
        _merge_kernel,
        grid=(n // tm,),
        in_specs=[row(d), row(WIDTH), row(WIDTH), _const_spec((1, d)), gate_spec,
                  _const_spec(w_ha.shape), _const_spec(w_da.shape), _const_spec(w_o.shape)],
        out_specs=row(d),
        out_shape=jax.ShapeDtypeStruct((n, d), F32),
        compiler_params=pltpu.CompilerParams(dimension_semantics=("parallel",),
                                             vmem_limit_bytes=VMEM_LIMIT),
        name="merge",
    )(x2d, oa, ob, n1, w_in_bf, w_ha, w_da, w_o)


def _moe_kernel(h_ref, n2_ref, wr_hi_ref, wr_lo_ref, br_ref, wg_ref, wu_ref, wd_ref, y_ref):
    h = h_ref[...]
    hn = _rms_rows(h, n2_ref[...])
    hn_hi = hn.astype(BF16)
    hn_lo = (hn - hn_hi.astype(F32)).astype(BF16)
    logits = (_dot(hn_hi, wr_hi_ref[...]) + _dot(hn_lo, wr_hi_ref[...])
              + _dot(hn_hi, wr_lo_ref[...]) + br_ref[...])
    lane = lax.broadcasted_iota(jnp.int32, logits.shape, 1)
    big = jnp.int32(1 << 20)
    ninf = -jnp.inf

    def first_lane(mask):
        return jnp.min(jnp.where(mask, lane, big), axis=1, keepdims=True)

    is_g = lane < N_GROUPS
    g_max = jnp.max(jnp.where(is_g, logits, ninf), axis=1, keepdims=True)
    g_idx = first_lane(is_g & (logits == g_max))
    g_sum = jnp.sum(jnp.where(is_g, jnp.exp(logits - g_max), 0.0), axis=1, keepdims=True)
    p_g = 1.0 / g_sum
    e_lane = lane - N_GROUPS
    in_group = ((e_lane >= 0) & (e_lane < N_EXPERTS)
                & (lax.shift_right_arithmetic(e_lane, 2) == g_idx))
    e_max = jnp.max(jnp.where(in_group, logits, ninf), axis=1, keepdims=True)
    e_exp = jnp.where(in_group, jnp.exp(logits - e_max), 0.0)
    e_prob = e_exp / jnp.sum(e_exp, axis=1, keepdims=True)
    p1 = jnp.max(jnp.where(in_group, e_prob, -1.0), axis=1, keepdims=True)
    i1 = first_lane(in_group & (e_prob == p1))
    rest = in_group & (lane != i1)
    p2 = jnp.max(jnp.where(rest, e_prob, -1.0), axis=1, keepdims=True)
    i2 = first_lane(rest & (e_prob == p2))
    denom = p1 + p2
    combine = (jnp.where(lane == i1, p_g * p1 / denom, 0.0)
               + jnp.where(lane == i2, p_g * p2 / denom, 0.0))

    acc = jnp.zeros_like(h)
    for ex in range(N_EXPERTS):
        gt = _dot(hn_hi, wg_ref[ex])
        up = _dot(hn_hi, wu_ref[ex])
        hh = _silu(gt) * up * combine[:, N_GROUPS + ex:N_GROUPS + ex + 1]
        acc = acc + _dot(hh.astype(BF16), wd_ref[ex])
    y_ref[...] = h + acc


def _moe(h2d, n2, wr_hi, wr_lo, br, wg, wu, wd, tm):
    n, d = h2d.shape
    row = pl.BlockSpec((tm, d), lambda i: (i, 0))
    single = lambda shape: pl.BlockSpec(shape, lambda i: (0,) * len(shape),
                                        pipeline_mode=pl.Buffered(1))
    return pl.pallas_call(
        _moe_kernel,
        grid=(n // tm,),
        in_specs=[row, _const_spec((1, d)), _const_spec(wr_hi.shape), _const_spec(wr_lo.shape),
                  _const_spec(br.shape), single(wg.shape), single(wu.shape), single(wd.shape)],
        out_specs=row,
        out_shape=jax.ShapeDtypeStruct((n, d), F32),
        compiler_params=pltpu.CompilerParams(dimension_semantics=("parallel",),
                                             vmem_limit_bytes=VMEM_LIMIT),
        name="moe",
    )(h2d, n2, wr_hi, wr_lo, br, wg, wu, wd)


def _t5_bias(dist, rel_bias):
    n = jnp.maximum(dist, 0)
    max_exact = N_BUCKETS // 2
    nf = jnp.maximum(n, 1).astype(F32)
    scaled = (jnp.log(nf / max_exact) / math.log(MAX_DISTANCE / max_exact)
              * (N_BUCKETS - max_exact))
    large = max_exact + jnp.floor(jnp.maximum(scaled, 0.0)).astype(jnp.int32)
    large = jnp.minimum(large, N_BUCKETS - 1)
    bucket = jnp.where(n < max_exact, n, large)
    rb = rel_bias.astype(F32)
    onehot = (bucket[..., None] == jnp.arange(N_BUCKETS)).astype(F32)
    bias = jnp.moveaxis(jnp.dot(onehot, rb, precision=lax.Precision.HIGHEST), -1, 0)
    far = rb[N_BUCKETS - 1].reshape((HEADS,) + (1,) * dist.ndim)
    return jnp.where(dist >= 0, (bias - far) * LOG2E, MASK_VALUE)


def kernel(x_prompt, x_sample, cache_k, cache_v, state_hgrn, page_table, meta_tokens, rel_bias,
           hg_lb_logits, norm1_w, w_in, hg_onorm_w, w_hg_out, q_norm_w, k_norm_w, da_lambda,
           da_onorm_w, w_da_out, w_o, norm2_w, w_router_group, b_router_group,
           w_router_expert, b_router_expert, w_gate, w_up, w_down):
    bsz, seq, d = x_prompt.shape
    nb, n_tok, _ = x_sample.shape
    depth = w_in.shape[0]
    assert depth == 1 and hg_lb_logits.shape[0] == 2
    n_pages = page_table.shape[1]
    psize = cache_k.shape[2]
    past_len = n_pages * psize
    blk = ATT_BLOCK
    assert blk >= MAX_DISTANCE and psize >= MAX_DISTANCE and n_tok <= 8
    lam_init = 0.8 - 0.6 * math.exp(-0.3 * 0)
    w_attn = WIDTH

    assert w_in.shape[2] == IN_A_COLS + 2 * d
    w_in_bf = w_in[0].astype(BF16)
    n1 = norm1_w[0].reshape(1, d)
    n2 = norm2_w[0].reshape(1, d)
    qw = (jnp.tile(q_norm_w[0], 2 * HEADS) * (DA_DH ** -0.5 * LOG2E)).reshape(1, w_attn)
    kw = jnp.tile(k_norm_w[0], 2 * HEADS).reshape(1, w_attn)
    gi = jnp.arange(w_attn) // DA_DH
    gmat = jnp.where(gi[:, None] == gi[None, :], 1.0 / DA_DH, 0.0).astype(BF16)
    hg_onw = jnp.tile(hg_onorm_w[0], HEADS).reshape(1, w_attn)
    da_onw = da_onorm_w[0].reshape(1, HEAD_W)
    lam_p = da_lambda[0].astype(F32)
    w_ha = w_hg_out[0].astype(BF16)
    w_da = w_da_out[0].astype(BF16)
    w_o0 = w_o[0].astype(BF16)
    wr = jnp.concatenate([w_router_group[0], w_router_expert[0],
                          jnp.zeros((d, 128 - N_GROUPS - N_EXPERTS), F32)], axis=1)
    wr_hi = wr.astype(BF16)
    wr_lo = (wr - wr_hi.astype(F32)).astype(BF16)
    br = jnp.concatenate([b_router_group[0], b_router_expert[0],
                          jnp.zeros((128 - N_GROUPS - N_EXPERTS,), F32)]).reshape(1, 128)
    wg = w_gate[0].astype(BF16)
    wu = w_up[0].astype(BF16)
    wd = w_down[0].astype(BF16)

    n_small = N_META + nb * n_tok
    x_small = jnp.concatenate([meta_tokens.astype(F32), x_sample.reshape(nb * n_tok, d)], axis=0)
    zh_s, q_s, kf_s, kb_s, vf_s, vb_s = _inproj_small(x_small, n1, w_in_bf, gmat, qw, kw)

    _, s_meta = _hgrn(zh_s[:N_META][None], hg_lb_logits, hg_onw,
                      jnp.zeros((1, HEADS, HEAD_W, HEAD_W), F32), shared_state=False, out_dtype=F32)
    oa_s, s_sample = _hgrn(zh_s[N_META:].reshape(nb, n_tok, HG_COLS), hg_lb_logits, hg_onw,
                           state_hgrn[0], shared_state=False, out_dtype=F32)

    x_p = x_prompt.reshape(bsz * seq, d)
    zh_p, qt_p, kf_p, kb_p, vf_p, vt_p = _inproj_prompt(
        x_p, n1, w_in_bf, gmat, qw, kw, kf_s[:N_META].reshape(N_META * HEADS, HEAD_W),
        vf_s[:N_META].reshape(N_META * HEADS, HEAD_W), seq, TOKEN_TILE)
    oa_p, s_prompt = _hgrn(zh_p.reshape(bsz, seq, HG_COLS), hg_lb_logits, hg_onw, s_meta,
                           shared_state=True, out_dtype=BF16)

    r = jnp.arange(blk)
    tb = jnp.stack([_t5_bias(r[None, :] - r[:, None], rel_bias),
                    _t5_bias(blk + r[None, :] - r[:, None], rel_bias)])
    tb = jnp.concatenate([tb, tb], axis=3)
    mrow = jnp.arange(128)
    t_meta0 = _t5_bias(N_META + r[None, :] - mrow[:, None], rel_bias)
    t_meta = jnp.stack([t_meta0, jnp.zeros_like(t_meta0)])
    t_meta = jnp.where(mrow[:, None] < N_META, t_meta, MASK_VALUE)
    t_meta = jnp.concatenate([t_meta, t_meta], axis=3)
    pad_meta = lambda a: jnp.concatenate([a, jnp.zeros((128 - N_META, w_attn), a.dtype)], axis=0)
    vmt = pad_meta(vb_s[:N_META]).T.reshape(HEADS, HEAD_W, 128)
    vmt = jnp.concatenate([vmt, jnp.ones((HEADS, VT_PAD, 128), BF16)], axis=1).reshape(VT_ROWS, 128)
    ob_p = _pattn(qt_p, kb_p.reshape(bsz, seq, w_attn), vt_p, pad_meta(kb_s[:N_META]),
                  vmt, tb, t_meta, lam_p, da_onw.reshape(HEAD_W, 1), lam_init)

    q_tok = q_s[N_META:].reshape(nb, n_tok, HEADS, 1, 2, DA_DH)
    q_maps = q_tok * jnp.eye(2, dtype=BF16).reshape(1, 1, 1, 2, 2, 1)
    n_rows = 2 * HEADS * n_tok
    qall = jnp.transpose(q_maps, (0, 2, 3, 1, 4, 5)).reshape(nb, n_rows, HEAD_W)
    prow = psize * HEADS
    pad_new = lambda a: jnp.concatenate(
        [a.reshape(nb, n_tok * HEADS, HEAD_W),
         jnp.zeros((nb, prow - n_tok * HEADS, HEAD_W), a.dtype)], axis=1)
    s_idx = jnp.tile(jnp.arange(n_tok), 2 * HEADS)
    head_of_row = jnp.repeat(jnp.arange(HEADS), 2 * n_tok)
    key_r = jnp.arange(prow) // HEADS
    own_head = head_of_row[:, None] == (jnp.arange(prow) % HEADS)[None, :]
    dist_last = (past_len + s_idx[:, None]) - (past_len - psize + key_r[None, :])
    dist_new = jnp.where(key_r[None, :] < n_tok, s_idx[:, None] - key_r[None, :], -1)
    pick = lambda t: jnp.where(own_head, t[head_of_row, jnp.arange(n_rows)], MASK_VALUE)
    t_last = pick(_t5_bias(dist_last, rel_bias))
    t_new = pick(_t5_bias(dist_new, rel_bias))
    t_mask = jnp.where(own_head, 0.0, MASK_VALUE).astype(F32)
    ob_s = _sattn(page_table, qall, pad_new(kb_s[N_META:]), pad_new(vb_s[N_META:]), t_last, t_new,
                  t_mask, lam_p, da_onw, cache_k.reshape(-1, HEAD_W), cache_v.reshape(-1, HEAD_W),
                  lam_init, n_tok)

    h_p = _merge(x_p, oa_p.reshape(bsz * seq, w_attn), ob_p.reshape(bsz * seq, w_attn),
                 n1, w_in_bf, w_ha, w_da, w_o0, TOKEN_TILE)
    y_p = _moe(h_p, n2, wr_hi, wr_lo, br, wg, wu, wd, TOKEN_TILE)
    x_s = x_small[N_META:]
    h_s = _merge(x_s, oa_s.reshape(nb * n_tok, w_attn).astype(BF16),
                 ob_s.reshape(nb * n_tok, w_attn).astype(BF16), n1, w_in_bf, w_ha, w_da, w_o0,
                 nb * n_tok)
    y_s = _moe(h_s, n2, wr_hi, wr_lo, br, wg, wu, wd, nb * n_tok)

    return (y_p.reshape(bsz, seq, d),
            y_s.reshape(nb, n_tok, d),
            kf_p.reshape(1, bsz, seq + N_META, HEADS, HEAD_W),
            vf_p.reshape(1, bsz, seq + N_META, HEADS, HEAD_W),
            s_prompt[None],
            kf_s[N_META:].reshape(1, nb, n_tok, HEADS, HEAD_W),
            vf_s[N_META:].reshape(1, nb, n_tok, HEADS, HEAD_W),
            s_sample[None])
```

```python
import functools
import math

import jax
import jax.numpy as jnp
from jax import lax
from jax.experimental import pallas as pl
from jax.experimental.pallas import tpu as pltpu

F32 = jnp.float32
BF16 = jnp.bfloat16

EPS = 1e-6
N_META = 16
HEADS = 4
HEAD_W = 128
DA_DH = 64
N_BUCKETS = 32
MAX_DISTANCE = 128
N_GROUPS = 4
EXPERTS_PER_GROUP = 4
N_EXPERTS = N_GROUPS * EXPERTS_PER_GROUP
MASK_VALUE = -1e30
HG_CHUNK = 128
HG_STEP_CHUNKS = 4
HG_BASE = 8
ATT_BLOCK = 256
VT_PAD = 16
VT_ROWS = HEADS * (HEAD_W + VT_PAD)
FAR_UNROLL = 8
PAGES_PER_STEP = 16
PAGE_SLOTS = 3
WIDTH = HEADS * HEAD_W
HG_COLS = 4 * WIDTH
IN_A_COLS = HG_COLS + 3 * WIDTH
TOKEN_TILE = 512
LOG2E = math.log2(math.e)
VMEM_LIMIT = 56 * 1024 * 1024


def _dot(a, b):
    return jnp.dot(a, b, preferred_element_type=F32)


def _dot_nt(a, b):
    return lax.dot_general(a, b, (((1,), (1,)), ((), ())), preferred_element_type=F32)


def _dot_tn(a, b):
    return lax.dot_general(a, b, (((0,), (0,)), ((), ())), preferred_element_type=F32)


def _sigmoid(x):
    return 0.5 + 0.5 * jnp.tanh(0.5 * x)


def _silu(x):
    y = 0.5 * x
    return y + y * jnp.tanh(y)


def _rms_rows(x, w):
    return x * lax.rsqrt(jnp.mean(x * x, axis=-1, keepdims=True) + EPS) * w


def _const_spec(shape):
    zeros = (0,) * len(shape)
    return pl.BlockSpec(shape, lambda *_: zeros)


def _inproj_values(x_ref, n1_ref, w_ref, g_ref, qw_ref, kw_ref):
    xn = _rms_rows(x_ref[...], n1_ref[...]).astype(BF16)
    g = g_ref[...]

    def group_norm(z, w):
        sq = z * z
        hi = sq.astype(BF16)
        lo = (sq - hi.astype(F32)).astype(BF16)
        ms = _dot(hi, g) + _dot(lo, g)
        return z * lax.rsqrt(ms + EPS) * w

    q0, k0, v0 = HG_COLS, HG_COLS + WIDTH, HG_COLS + 2 * WIDTH
    zh = _dot(xn, w_ref[:, 0:HG_COLS])
    qn = group_norm(_dot(xn, w_ref[:, q0:k0]), qw_ref[...])
    kn = group_norm(_dot(xn, w_ref[:, k0:v0]), kw_ref[...])
    zv = _dot(xn, w_ref[:, v0:IN_A_COLS])
    return zh, qn, kn, zv


def _inproj_small_kernel(x_ref, n1_ref, w_ref, g_ref, qw_ref, kw_ref,
                         zh_ref, q_ref, kf_ref, kb_ref, vf_ref, vb_ref):
    zh, qn, kn, zv = _inproj_values(x_ref, n1_ref, w_ref, g_ref, qw_ref, kw_ref)
    zh_ref[...] = zh
    q_ref[...] = qn.astype(BF16)
    kf_ref[...] = kn
    kb_ref[...] = kn.astype(BF16)
    vf_ref[...] = zv
    vb_ref[...] = zv.astype(BF16)


def _inproj_prompt_kernel(x_ref, n1_ref, w_ref, g_ref, qw_ref, kw_ref, km_ref, vm_ref,
                          zh_ref, qt_ref, kf_hbm, kb_ref, vf_hbm, vt_ref,
                          kst, vst, sem, msem, *, seq):
    zh, qn, kn, zv = _inproj_values(x_ref, n1_ref, w_ref, g_ref, qw_ref, kw_ref)
    tm = zh.shape[0]
    tiles = seq // tm
    i = pl.program_id(0)
    n_steps = pl.num_programs(0)
    slot = i % 2

    def batch_row0(step):
        return (step // tiles) * (N_META + seq) * HEADS

    def tile_copies(step, from_slot):
        row0 = batch_row0(step) + (N_META + (step % tiles) * tm) * HEADS
        rows = pl.ds(pl.multiple_of(row0, 8 * HEADS), tm * HEADS)
        return [pltpu.make_async_copy(kst.at[from_slot], kf_hbm.at[rows, :], sem.at[from_slot, 0]),
                pltpu.make_async_copy(vst.at[from_slot], vf_hbm.at[rows, :], sem.at[from_slot, 1])]

    zh_ref[...] = zh
    kb_ref[...] = kn.astype(BF16)
    qt = qn.T
    vt = zv.T
    for c in range(tm // ATT_BLOCK):
        cols = slice(c * ATT_BLOCK, (c + 1) * ATT_BLOCK)
        qt_ref[c] = qt[:, cols].astype(BF16)
        for h in range(HEADS):
            r0 = h * (HEAD_W + VT_PAD)
            vt_ref[c, r0:r0 + HEAD_W, :] = vt[h * HEAD_W:(h + 1) * HEAD_W, cols].astype(BF16)
            vt_ref[c, r0 + HEAD_W:r0 + HEAD_W + VT_PAD, :] = jnp.ones((VT_PAD, ATT_BLOCK), BF16)

    @pl.when(i >= 2)
    def _():
        for cp in tile_copies(i - 2, slot):
            cp.wait()

    for h in range(HEADS):
        hs = slice(h * HEAD_W, (h + 1) * HEAD_W)
        kst[slot, pl.ds(h, tm, stride=HEADS), :] = kn[:, hs]
        vst[slot, pl.ds(h, tm, stride=HEADS), :] = zv[:, hs]
    for cp in tile_copies(i, slot):
        cp.start()

    @pl.when(i % tiles == 0)
    def _():
        rows = pl.ds(pl.multiple_of(batch_row0(i), 8 * HEADS), N_META * HEADS)
        meta = [pltpu.make_async_copy(km_ref, kf_hbm.at[rows, :], msem.at[0]),
                pltpu.make_async_copy(vm_ref, vf_hbm.at[rows, :], msem.at[1])]
        for cp in meta:
            cp.start()
        for cp in meta:
            cp.wait()

    @pl.when(i == n_steps - 1)
    def _():
        @pl.when(i >= 1)
        def _():
            for cp in tile_copies(i - 1, 1 - slot):
                cp.wait()
        for cp in tile_copies(i, slot):
            cp.wait()


def _w_in_spec(d):
    return pl.BlockSpec((d, IN_A_COLS), lambda i: (0, 0))


def _inproj_small(x2d, n1, w_in_bf, gmat, qw, kw):
    n, d = x2d.shape
    full = lambda width: _const_spec((n, width))
    return pl.pallas_call(
        _inproj_small_kernel,
        grid=(1,),
        in_specs=[full(d), _const_spec((1, d)), _w_in_spec(d), _const_spec(gmat.shape),
                  _const_spec((1, WIDTH)), _const_spec((1, WIDTH))],
        out_specs=[full(HG_COLS)] + [full(WIDTH)] * 5,
        out_shape=[jax.ShapeDtypeStruct((n, HG_COLS), F32), jax.ShapeDtypeStruct((n, WIDTH), BF16),
                   jax.ShapeDtypeStruct((n, WIDTH), F32), jax.ShapeDtypeStruct((n, WIDTH), BF16),
                   jax.ShapeDtypeStruct((n, WIDTH), F32), jax.ShapeDtypeStruct((n, WIDTH), BF16)],
        compiler_params=pltpu.CompilerParams(dimension_semantics=("arbitrary",),
                                             vmem_limit_bytes=VMEM_LIMIT),
        name="inproj_small",
    )(x2d, n1, w_in_bf, gmat, qw, kw)


def _inproj_prompt(x2d, n1, w_in_bf, gmat, qw, kw, k_meta, v_meta, seq, tm):
    n, d = x2d.shape
    cache_shape = jax.ShapeDtypeStruct(((n // seq) * (N_META + seq) * HEADS, HEAD_W), F32)
    assert seq % tm == 0 and tm % ATT_BLOCK == 0
    per = tm // ATT_BLOCK
    row = lambda width: pl.BlockSpec((tm, width), lambda i: (i, 0))
    t_spec = lambda rows: pl.BlockSpec((per, rows, ATT_BLOCK), lambda i: (i, 0, 0))
    t_shape = lambda rows: jax.ShapeDtypeStruct((n // ATT_BLOCK, rows, ATT_BLOCK), BF16)
    hbm = pl.BlockSpec(memory_space=pl.ANY)
    return pl.pallas_call(
        functools.partial(_inproj_prompt_kernel, seq=seq),
        grid=(n // tm,),
        in_specs=[row(d), _const_spec((1, d)), _w_in_spec(d), _const_spec(gmat.shape),
                  _const_spec((1, WIDTH)), _const_spec((1, WIDTH)),
                  _const_spec(k_meta.shape), _const_spec(v_meta.shape)],
        out_specs=[row(HG_COLS), t_spec(WIDTH), hbm, row(WIDTH), hbm, t_spec(VT_ROWS)],
        out_shape=[jax.ShapeDtypeStruct((n, HG_COLS), F32), t_shape(WIDTH), cache_shape,
                   jax.ShapeDtypeStruct((n, WIDTH), BF16), cache_shape, t_shape(VT_ROWS)],
        scratch_shapes=[pltpu.VMEM((2, tm * HEADS, HEAD_W), F32),
                        pltpu.VMEM((2, tm * HEADS, HEAD_W), F32),
                        pltpu.SemaphoreType.DMA((2, 2)), pltpu.SemaphoreType.DMA((2,))],
        compiler_params=pltpu.CompilerParams(dimension_semantics=("arbitrary",),
                                             vmem_limit_bytes=VMEM_LIMIT),
        name="inproj",
    )(x2d, n1, w_in_bf, gmat, qw, kw, k_meta, v_meta)


def _hgrn_kernel(z_ref, lbl_ref, onw_ref, s0_ref, o_ref, sout_ref, st_scr, *, rows):
    C = HG_CHUNK
    W = HEADS * HEAD_W
    n_sub = max(rows // C, 1)
    n_valid = min(rows, C)
    c = pl.program_id(1)
    heads = [slice(h * HEAD_W, (h + 1) * HEAD_W) for h in range(HEADS)]

    @pl.when(c == 0)
    def _():
        for h in range(HEADS):
            st_scr[h] = s0_ref[0, h].T

    lg = lbl_ref[...]
    e = jnp.exp(lg - jnp.max(lg, axis=0, keepdims=True))
    lb = e[0:1] / jnp.sum(e, axis=0, keepdims=True)
    half = 0.5 * (1.0 - lb)
    onw = onw_ref[...]
    row = lax.broadcasted_iota(jnp.int32, (C, C), 0)
    col = lax.broadcasted_iota(jnp.int32, (C, C), 1)
    tri = jnp.where(col <= row, 1.0, 0.0).astype(BF16)

    def shr(x, width):
        return lax.shift_right_logical(x, int(math.log2(width)))

    def front(z, n_valid):
        hq, hf, hi, hg = z[:, 0:W], z[:, W:2 * W], z[:, 2 * W:3 * W], z[:, 3 * W:4 * W]
        q = _silu(hq)
        f = (lb + half) + half * jnp.tanh(0.5 * hf)
        k = 1.0 - f
        logf = jnp.log2(f)
        if n_valid < C:
            valid = lax.broadcasted_iota(jnp.int32, (C, W), 0) < n_valid
            k = jnp.where(valid, k, 0.0)
            logf = jnp.where(valid, logf, 0.0)
        x1 = logf.astype(BF16)
        r1 = logf - x1.astype(F32)
        x2 = r1.astype(BF16)
        x3 = (r1 - x2.astype(F32)).astype(BF16)
        b = _dot(tri, x1) + _dot(tri, x2) + _dot(tri, x3)
        return q, k, hi.astype(BF16), b, _silu(hg)

    def intra(q, k, b):
        b_last = b[C - 1:C]

        def block_row(width, r):
            b3 = b.reshape(C // width, width, W)
            return jnp.broadcast_to(b3[:, r:r + 1, :], b3.shape).reshape(C, W)

        def next_block_first(width):
            b3 = b.reshape(C // width, width, W)
            nxt = jnp.concatenate([b3[1:, 0:1, :], b_last[None]], axis=0)
            return jnp.broadcast_to(nxt, b3.shape).reshape(C, W)

        mid = block_row(HG_BASE, HG_BASE // 2)
        qe = (q * jnp.exp2(b - mid)).astype(BF16)
        ke = (k * jnp.exp2(mid - b)).astype(BF16)
        mask = (shr(row, HG_BASE) == shr(col, HG_BASE)) & (col <= row)
        scores = [jnp.where(mask, _dot_nt(qe[:, hs], ke[:, hs]), 0.0) for hs in heads]
        width = HG_BASE
        while width < n_valid:
            qe = (q * jnp.exp2(b - block_row(width, 0))).astype(BF16)
            ke = (k * jnp.exp2(next_block_first(width) - b)).astype(BF16)
            mask = ((shr(row, width) == shr(col, width) + 1)
                    & (shr(row, 2 * width) == shr(col, 2 * width)))
            scores = [jnp.where(mask, _dot_nt(qe[:, hs], ke[:, hs]), sc)
                      for hs, sc in zip(heads, scores)]
            width *= 2
        qb = (q * jnp.exp2(b)).astype(BF16)
        kd = (k * jnp.exp2(b_last - b)).astype(BF16)
        return [sc.astype(BF16) for sc in scores], qb, kd, jnp.exp2(b_last)

    fronts = []
    for u in range(n_sub):
        z = z_ref[0, u * n_valid:(u + 1) * n_valid, :]
        if n_valid < C:
            z = jnp.concatenate([z, jnp.zeros((C - n_valid, z.shape[1]), F32)], axis=0)
        fronts.append(front(z, n_valid))
    intras = [intra(q, k, b) for q, k, _, b, _ in fronts]
    for u in range(n_sub):
        _, _, v_bf, _, gate = fronts[u]
        scores, qb, kd, decay = intras[u]
        outs = []
        for h, hs in enumerate(heads):
            st = st_scr[h]
            outs.append(_dot_nt(qb[:, hs], st.astype(BF16)) + _dot(scores[h], v_bf[:, hs]))
            st_scr[h] = st * decay[:, hs] + _dot_tn(v_bf[:, hs], kd[:, hs])
        for h, hs in enumerate(heads):
            on = _rms_rows(outs[h], onw[:, hs]) * gate[:, hs]
            o_ref[0, u * n_valid:(u + 1) * n_valid, hs] = on[:n_valid].astype(o_ref.dtype)

    @pl.when(c == pl.num_programs(1) - 1)
    def _():
        for h in range(HEADS):
            sout_ref[0, h] = st_scr[h].T


def _hgrn(zh, lbl, onw, s0, *, shared_state, out_dtype):
    nseq, t, _ = zh.shape
    rows = min(t, HG_STEP_CHUNKS * HG_CHUNK)
    assert t % rows == 0 and rows % 8 == 0 and (rows <= HG_CHUNK or rows % HG_CHUNK == 0)
    s_map = (lambda b, c: (0, 0, 0, 0)) if shared_state else (lambda b, c: (b, 0, 0, 0))
    return pl.pallas_call(
        functools.partial(_hgrn_kernel, rows=rows),
        grid=(nseq, t // rows),
        in_specs=[pl.BlockSpec((1, rows, HG_COLS), lambda b, c: (b, c, 0)),
                  _const_spec(lbl.shape), _const_spec((1, WIDTH)),
                  pl.BlockSpec((1, HEADS, HEAD_W, HEAD_W), s_map)],
        out_specs=[pl.BlockSpec((1, rows, WIDTH), lambda b, c: (b, c, 0)),
                   pl.BlockSpec((1, HEADS, HEAD_W, HEAD_W), lambda b, c: (b, 0, 0, 0))],
        out_shape=[jax.ShapeDtypeStruct((nseq, t, WIDTH), out_dtype),
                   jax.ShapeDtypeStruct((nseq, HEADS, HEAD_W, HEAD_W), F32)],
        scratch_shapes=[pltpu.VMEM((HEADS, HEAD_W, HEAD_W), F32)],
        compiler_params=pltpu.CompilerParams(dimension_semantics=("parallel", "arbitrary"),
                                             vmem_limit_bytes=VMEM_LIMIT),
        name="hgrn",
    )(zh, lbl, onw, s0)


def _lambda_value(lp, lam_init):
    s1 = jnp.sum(lp[0:1] * lp[1:2], axis=1, keepdims=True)
    s2 = jnp.sum(lp[2:3] * lp[3:4], axis=1, keepdims=True)
    return jnp.exp(s1) - jnp.exp(s2) + lam_init


def _softmax_update(s, v_bf, m_ref, l_ref, acc_ref, first):
    m_cur = jnp.max(s, axis=1, keepdims=True)
    if first:
        m_new = m_cur
        p = jnp.exp2(s - m_new)
        l_ref[...] = jnp.sum(p, axis=1, keepdims=True)
        acc_ref[...] = _dot(p.astype(BF16), v_bf)
    else:
        m_prev = m_ref[...]
        m_new = jnp.maximum(m_prev, m_cur)
        alpha = jnp.exp2(m_prev - m_new)
        p = jnp.exp2(s - m_new)
        l_ref[...] = alpha * l_ref[...] + jnp.sum(p, axis=1, keepdims=True)
        acc_ref[...] = alpha * acc_ref[...] + _dot(p.astype(BF16), v_bf)
    m_ref[...] = m_new


def _pattn_kernel(qt_ref, k_ref, vt_ref, km_ref, vmt_ref, tb_ref, tm_ref, lam_ref, onw_ref, o_ref,
                  qz_scr, m_scr, acc_scr, s_scr, *, lam_init):
    blk = ATT_BLOCK
    i = pl.program_id(1)
    sub = lax.broadcasted_iota(jnp.int32, (HEAD_W, blk), 0)
    for h in range(HEADS):
        qh = qt_ref[0, h * HEAD_W:(h + 1) * HEAD_W, :]
        zero = jnp.zeros_like(qh)
        qz_scr[h] = jnp.concatenate([jnp.where(sub < DA_DH, qh, zero),
                                     jnp.where(sub >= DA_DH, qh, zero)], axis=1)

    hv = lambda h: slice(h * (HEAD_W + VT_PAD), (h + 1) * (HEAD_W + VT_PAD))

    meta_sel = jnp.minimum(i, 1)
    meta_keys = "meta"

    def key_blocks(blocks, first=False, last=False):
        def qk_raw(j, h):
            hs = slice(h * HEAD_W, (h + 1) * HEAD_W)
            if j is meta_keys:
                return _dot(km_ref[:, hs], qz_scr[h])
            off = pl.multiple_of(j * blk, blk)
            return _dot(k_ref[0, pl.ds(off, blk), hs], qz_scr[h])

        def biased(s, j, table, h):
            if j is meta_keys:
                return s + tm_ref[meta_sel, h]
            return s if table is None else s + tb_ref[table, h]

        def soft(j, h, s):
            m_cur = jnp.max(s, axis=0, keepdims=True)
            if j is meta_keys:
                m_scr[h] = m_cur
                return jnp.exp2(s - m_cur).astype(BF16), None
            m_prev = m_scr[h]
            m_new = jnp.maximum(m_prev, m_cur)
            m_scr[h] = m_new
            return jnp.exp2(s - m_new).astype(BF16), jnp.exp2(m_prev - m_new)

        def pv(j, h, p, alpha):
            if j is meta_keys:
                acc_scr[h] = _dot(vmt_ref[hv(h), :], p)
            else:
                acc_scr[h] = alpha * acc_scr[h] + _dot(vt_ref[j, hv(h), :], p)

        stages = [(j, table, h) for j, table in blocks for h in range(HEADS)]
        j_after = 0 if blocks[-1][0] is meta_keys else blocks[-1][0] + 1
        s_cur = biased(qk_raw(stages[0][0], 0) if first else s_scr[...], *stages[0])
        pend = None
        for n, (j, table, h) in enumerate(stages):
            if n + 1 < len(stages):
                jn, tn, hn = stages[n + 1]
                s_next = biased(qk_raw(jn, hn), jn, tn, hn)
            else:
                s_next = None
                if not last:
                    s_scr[...] = qk_raw(j_after, 0)
            if pend is not None:
                pv(stages[n - 1][0], stages[n - 1][2], *pend)
            pend = soft(j, h, s_cur)
            s_cur = s_next
        pv(stages[-1][0], stages[-1][2], *pend)

    key_blocks([(meta_keys, None)], first=True)
    n_far = jnp.maximum(i - 1, 0)
    done = 0
    width = 1
    while width < FAR_UNROLL:
        start = done

        @pl.when((n_far & width) != 0)
        def _(start=start, width=width):
            key_blocks([(start + u, None) for u in range(width)])

        done = done + (n_far & width)
        width *= 2
    n_rem = done

    def far_body(t, carry):
        j = n_rem + FAR_UNROLL * t
        key_blocks([(j + u, None) for u in range(FAR_UNROLL)])
        return carry

    lax.fori_loop(0, n_far // FAR_UNROLL, far_body, 0)

    @pl.when(i >= 1)
    def _():
        key_blocks([(i - 1, 1), (i, 0)], last=True)

    @pl.when(i == 0)
    def _():
        key_blocks([(0, 0)], last=True)

    lam = _lambda_value(lam_ref[...], lam_init)
    onw = onw_ref[...]
    for h in range(HEADS):
        acc = acc_scr[h]
        a = acc[:HEAD_W] / acc[HEAD_W:HEAD_W + 1]
        ot = a[:, :blk] - lam * a[:, blk:]
        ot = ot * lax.rsqrt(jnp.mean(ot * ot, axis=0, keepdims=True) + EPS) * onw * (1.0 - lam_init)
        o_ref[0, :, h * HEAD_W:(h + 1) * HEAD_W] = ot.T.astype(o_ref.dtype)


def _pattn(qt, k, vt, km, vmt, tb, tmeta, lam_p, onw_col, lam_init):
    bsz, t, w = k.shape
    blk = ATT_BLOCK
    nblk = t // blk
    assert t % blk == 0
    return pl.pallas_call(
        functools.partial(_pattn_kernel, lam_init=lam_init),
        grid=(bsz, nblk),
        in_specs=[pl.BlockSpec((1, w, blk), lambda b, i: (b * nblk + i, 0, 0)),
                  pl.BlockSpec((1, t, w), lambda b, i: (b, 0, 0)),
                  pl.BlockSpec((nblk, VT_ROWS, blk), lambda b, i: (b, 0, 0)),
                  _const_spec(km.shape), _const_spec(vmt.shape), _const_spec(tb.shape),
                  _const_spec(tmeta.shape), _const_spec(lam_p.shape), _const_spec(onw_col.shape)],
        out_specs=pl.BlockSpec((1, blk, w), lambda b, i: (b, i, 0)),
        out_shape=jax.ShapeDtypeStruct((bsz, t, w), BF16),
        scratch_shapes=[pltpu.VMEM((HEADS, HEAD_W, 2 * blk), BF16),
                        pltpu.VMEM((HEADS, 1, 2 * blk), F32),
                        pltpu.VMEM((HEADS, HEAD_W + VT_PAD, 2 * blk), F32),
                        pltpu.VMEM((blk, 2 * blk), F32)],
        compiler_params=pltpu.CompilerParams(dimension_semantics=("parallel", "arbitrary"),
                                             vmem_limit_bytes=VMEM_LIMIT),
        name="prompt_attn",
    )(qt, k, vt, km, vmt, tb, tmeta, lam_p, onw_col)


def _sattn_kernel(pt_ref, q_ref, kn_ref, vn_ref, tl_ref, tn_ref, tmask_ref, lam_ref, onw_ref,
                  ck_hbm, cv_hbm, o_ref, kbuf, vbuf, sem, m_scr, l_scr, acc_scr,
                  *, lam_init, n_tok, n_pages):
    pg = PAGES_PER_STEP
    prow = kbuf.shape[2]
    b, g = pl.program_id(0), pl.program_id(1)
    n_steps = pl.num_programs(1)
    t = b * n_steps + g
    n_total = pl.num_programs(0) * n_steps
    slot = t % PAGE_SLOTS

    def page_copies(step, to_slot):
        base = (step // n_steps) * n_pages + (step % n_steps) * pg
        copies = []
        for u in range(pg):
            rows = pl.ds(pl.multiple_of(pt_ref[base + u] * prow, prow), prow)
            copies.append(pltpu.make_async_copy(ck_hbm.at[rows, :], kbuf.at[to_slot, u],
                                                sem.at[to_slot, 0]))
            copies.append(pltpu.make_async_copy(cv_hbm.at[rows, :], vbuf.at[to_slot, u],
                                                sem.at[to_slot, 1]))
        return copies

    ahead = PAGE_SLOTS - 1

    @pl.when(t == 0)
    def _():
        for d in range(ahead):
            @pl.when(d < n_total)
            def _(d=d):
                for cp in page_copies(d, d):
                    cp.start()

    @pl.when(t + ahead < n_total)
    def _():
        for cp in page_copies(t + ahead, (t + ahead) % PAGE_SLOTS):
            cp.start()

    q = q_ref[0]

    @pl.when(g == 0)
    def _():
        s = _dot_nt(q, kn_ref[0]) + tn_ref[...]
        _softmax_update(s, vn_ref[0], m_scr, l_scr, acc_scr, True)

    for cp in page_copies(t, slot):
        cp.wait()

    is_last = g == n_steps - 1
    tmask = tmask_ref[...]
    ss = []
    for u in range(pg):
        s = _dot_nt(q, kbuf[slot, u].astype(BF16))
        ss.append(s + (jnp.where(is_last, tl_ref[...], tmask) if u == pg - 1 else tmask))
    m_prev = m_scr[...]
    m_new = m_prev
    for s in ss:
        m_new = jnp.maximum(m_new, jnp.max(s, axis=1, keepdims=True))
    alpha = jnp.exp2(m_prev - m_new)
    l_new = alpha * l_scr[...]
    acc = alpha * acc_scr[...]
    for u in range(pg):
        p = jnp.exp2(ss[u] - m_new)
        l_new = l_new + jnp.sum(p, axis=1, keepdims=True)
        acc = acc + _dot(p.astype(BF16), vbuf[slot, u].astype(BF16))
    m_scr[...] = m_new
    l_scr[...] = l_new
    acc_scr[...] = acc

    @pl.when(is_last)
    def _():
        lam = _lambda_value(lam_ref[...], lam_init)
        onw = onw_ref[...]
        a = acc_scr[...] / l_scr[...]
        for h in range(HEADS):
            r1 = (2 * h) * n_tok
            r2 = (2 * h + 1) * n_tok
            o = a[r1:r1 + n_tok] - lam * a[r2:r2 + n_tok]
            o_ref[0, :, h * HEAD_W:(h + 1) * HEAD_W] = (
                _rms_rows(o, onw) * (1.0 - lam_init)).astype(o_ref.dtype)


def _sattn(page_table, qall, knew, vnew, tlast, tnew, tmask, lam_p, onw, ck2, cv2, lam_init, n_tok):
    nb, n_pages = page_table.shape
    pg = PAGES_PER_STEP
    assert n_pages % pg == 0
    nrow = qall.shape[1]
    prow = knew.shape[1]
    pt_flat = page_table.reshape(-1)
    per_b = lambda shape: pl.BlockSpec((1,) + shape, lambda b, g, pt: (b, 0, 0))
    const = lambda shape: pl.BlockSpec(shape, lambda b, g, pt: (0,) * len(shape))
    hbm = pl.BlockSpec(memory_space=pl.ANY)
    grid_spec = pltpu.PrefetchScalarGridSpec(
        num_scalar_prefetch=1,
        grid=(nb, n_pages // pg),
        in_specs=[per_b((nrow, HEAD_W)), per_b((prow, HEAD_W)), per_b((prow, HEAD_W)),
                  const(tlast.shape), const(tnew.shape), const(tmask.shape),
                  const(lam_p.shape), const(onw.shape), hbm, hbm],
        out_specs=per_b((n_tok, HEADS * HEAD_W)),
        scratch_shapes=[pltpu.VMEM((PAGE_SLOTS, pg, prow, HEAD_W), F32),
                        pltpu.VMEM((PAGE_SLOTS, pg, prow, HEAD_W), F32),
                        pltpu.SemaphoreType.DMA((PAGE_SLOTS, 2)),
                        pltpu.VMEM((nrow, 1), F32), pltpu.VMEM((nrow, 1), F32),
                        pltpu.VMEM((nrow, HEAD_W), F32)],
    )
    return pl.pallas_call(
        functools.partial(_sattn_kernel, lam_init=lam_init, n_tok=n_tok, n_pages=n_pages),
        grid_spec=grid_spec,
        out_shape=jax.ShapeDtypeStruct((nb, n_tok, HEADS * HEAD_W), F32),
        compiler_params=pltpu.CompilerParams(dimension_semantics=("arbitrary", "arbitrary"),
                                             vmem_limit_bytes=VMEM_LIMIT),
        name="sample_attn",
    )(pt_flat, qall, knew, vnew, tlast, tnew, tmask, lam_p, onw, ck2, cv2)


def _merge_kernel(x_ref, oa_ref, ob_ref, n1_ref, wg_ref, wha_ref, wda_ref, wo_ref, h_ref):
    x = x_ref[...]
    d = x.shape[1]
    xn = _rms_rows(x, n1_ref[...]).astype(BF16)
    gates = _dot(xn, wg_ref[...])
    ya = _dot(oa_ref[...], wha_ref[...])
    yb = _dot(ob_ref[...], wda_ref[...])
    merged = _sigmoid(gates[:, :d]) * ya + _sigmoid(gates[:, d:]) * yb
    h_ref[...] = x + _dot(merged.astype(BF16), wo_ref[...])


def _merge(x2d, oa, ob, n1, w_in_bf, w_ha, w_da, w_o, tm):
    n, d = x2d.shape
    row = lambda width: pl.BlockSpec((tm, width), lambda i: (i, 0))
    gate_spec = pl.BlockSpec((pl.Element(d), pl.Element(2 * d)), lambda i: (0, IN_A_COLS))
    return pl.pallas_call(
        _merge_kernel,
        grid=(n // tm,),
        in_specs=[row(d), row(WIDTH), row(WIDTH), _const_spec((1, d)), gate_spec,
                  _const_spec(w_ha.shape), _const_spec(w_da.shape), _const_spec(w_o.shape)],
        out_specs=row(d),
        out_shape=jax.ShapeDtypeStruct((n, d), F32),
        compiler_params=pltpu.CompilerParams(dimension_semantics=("parallel",),
                                             vmem_limit_bytes=VMEM_LIMIT),
        name="merge",
    )(x2d, oa, ob, n1, w_in_bf, w_ha, w_da, w_o)


def _moe_kernel(h_ref, n2_ref, wr_hi_ref, wr_lo_ref, br_ref, wg_ref, wu_ref, wd_ref, y_ref):
    h = h_ref[...]
    hn = _rms_rows(h, n2_ref[...])
    hn_hi = hn.astype(BF16)
    hn_lo = (hn - hn_hi.astype(F32)).astype(BF16)
    logits = (_dot(hn_hi, wr_hi_ref[...]) + _dot(hn_lo, wr_hi_ref[...])
              + _dot(hn_hi, wr_lo_ref[...]) + br_ref[...])
    lane = lax.broadcasted_iota(jnp.int32, logits.shape, 1)
    big = jnp.int32(1 << 20)
    ninf = -jnp.inf

    def first_lane(mask):
        return jnp.min(jnp.where(mask, lane, big), axis=1, keepdims=True)

    is_g = lane < N_GROUPS
    g_max = jnp.max(jnp.where(is_g, logits, ninf), axis=1, keepdims=True)
    g_idx = first_lane(is_g & (logits == g_max))
    g_sum = jnp.sum(jnp.where(is_g, jnp.exp(logits - g_max), 0.0), axis=1, keepdims=True)
    p_g = 1.0 / g_sum
    e_lane = lane - N_GROUPS
    in_group = ((e_lane >= 0) & (e_lane < N_EXPERTS)
                & (lax.shift_right_arithmetic(e_lane, 2) == g_idx))
    e_max = jnp.max(jnp.where(in_group, logits, ninf), axis=1, keepdims=True)
    e_exp = jnp.where(in_group, jnp.exp(logits - e_max), 0.0)
    e_prob = e_exp / jnp.sum(e_exp, axis=1, keepdims=True)
    p1 = jnp.max(jnp.where(in_group, e_prob, -1.0), axis=1, keepdims=True)
    i1 = first_lane(in_group & (e_prob == p1))
    rest = in_group & (lane != i1)
    p2 = jnp.max(jnp.where(rest, e_prob, -1.0), axis=1, keepdims=True)
    i2 = first_lane(rest & (e_prob == p2))
    denom = p1 + p2
    combine = (jnp.where(lane == i1, p_g * p1 / denom, 0.0)
               + jnp.where(lane == i2, p_g * p2 / denom, 0.0))

    acc = jnp.zeros_like(h)
    for ex in range(N_EXPERTS):
        gt = _dot(hn_hi, wg_ref[ex])
        up = _dot(hn_hi, wu_ref[ex])
        hh = _silu(gt) * up * combine[:, N_GROUPS + ex:N_GROUPS + ex + 1]
        acc = acc + _dot(hh.astype(BF16), wd_ref[ex])
    y_ref[...] = h + acc


def _moe(h2d, n2, wr_hi, wr_lo, br, wg, wu, wd, tm):
    n, d = h2d.shape
    row = pl.BlockSpec((tm, d), lambda i: (i, 0))
    single = lambda shape: pl.BlockSpec(shape, lambda i: (0,) * len(shape),
                                        pipeline_mode=pl.Buffered(1))
    return pl.pallas_call(
        _moe_kernel,
        grid=(n // tm,),
        in_specs=[row, _const_spec((1, d)), _const_spec(wr_hi.shape), _const_spec(wr_lo.shape),
                  _const_spec(br.shape), single(wg.shape), single(wu.shape), single(wd.shape)],
        out_specs=row,
        out_shape=jax.ShapeDtypeStruct((n, d), F32),
        compiler_params=pltpu.CompilerParams(dimension_semantics=("parallel",),
                                             vmem_limit_bytes=VMEM_LIMIT),
        name="moe",
    )(h2d, n2, wr_hi, wr_lo, br, wg, wu, wd)


def _t5_bias(dist, rel_bias):
    n = jnp.maximum(dist, 0)
    max_exact = N_BUCKETS // 2
    nf = jnp.maximum(n, 1).astype(F32)
    scaled = (jnp.log(nf / max_exact) / math.log(MAX_DISTANCE / max_exact)
              * (N_BUCKETS - max_exact))
    large = max_exact + jnp.floor(jnp.maximum(scaled, 0.0)).astype(jnp.int32)
    large = jnp.minimum(large, N_BUCKETS - 1)
    bucket = jnp.where(n < max_exact, n, large)
    rb = rel_bias.astype(F32)
    onehot = (bucket[..., None] == jnp.arange(N_BUCKETS)).astype(F32)
    bias = jnp.moveaxis(jnp.dot(onehot, rb, precision=lax.Precision.HIGHEST), -1, 0)
    far = rb[N_BUCKETS - 1].reshape((HEADS,) + (1,) * dist.ndim)
    return jnp.where(dist >= 0, (bias - far) * LOG2E, MASK_VALUE)


def kernel(x_prompt, x_sample, cache_k, cache_v, state_hgrn, page_table, meta_tokens, rel_bias,
           hg_lb_logits, norm1_w, w_in, hg_onorm_w, w_hg_out, q_norm_w, k_norm_w, da_lambda,
           da_onorm_w, w_da_out, w_o, norm2_w, w_router_group, b_router_group,
           w_router_expert, b_router_expert, w_gate, w_up, w_down):
    bsz, seq, d = x_prompt.shape
    nb, n_tok, _ = x_sample.shape
    depth = w_in.shape[0]
    assert depth == 1 and hg_lb_logits.shape[0] == 2
    n_pages = page_table.shape[1]
    psize = cache_k.shape[2]
    past_len = n_pages * psize
    blk = ATT_BLOCK
    assert blk >= MAX_DISTANCE and psize >= MAX_DISTANCE and n_tok <= 8
    lam_init = 0.8 - 0.6 * math.exp(-0.3 * 0)
    w_attn = WIDTH

    assert w_in.shape[2] == IN_A_COLS + 2 * d
    w_in_bf = w_in[0].astype(BF16)
    n1 = norm1_w[0].reshape(1, d)
    n2 = norm2_w[0].reshape(1, d)
    qw = (jnp.tile(q_norm_w[0], 2 * HEADS) * (DA_DH ** -0.5 * LOG2E)).reshape(1, w_attn)
    kw = jnp.tile(k_norm_w[0], 2 * HEADS).reshape(1, w_attn)
    gi = jnp.arange(w_attn) // DA_DH
    gmat = jnp.where(gi[:, None] == gi[None, :], 1.0 / DA_DH, 0.0).astype(BF16)
    hg_onw = jnp.tile(hg_onorm_w[0], HEADS).reshape(1, w_attn)
    da_onw = da_onorm_w[0].reshape(1, HEAD_W)
    lam_p = da_lambda[0].astype(F32)
    w_ha = w_hg_out[0].astype(BF16)
    w_da = w_da_out[0].astype(BF16)
    w_o0 = w_o[0].astype(BF16)
    wr = jnp.concatenate([w_router_group[0], w_router_expert[0],
                          jnp.zeros((d, 128 - N_GROUPS - N_EXPERTS), F32)], axis=1)
    wr_hi = wr.astype(BF16)
    wr_lo = (wr - wr_hi.astype(F32)).astype(BF16)
    br = jnp.concatenate([b_router_group[0], b_router_expert[0],
                          jnp.zeros((128 - N_GROUPS - N_EXPERTS,), F32)]).reshape(1, 128)
    wg = w_gate[0].astype(BF16)
    wu = w_up[0].astype(BF16)
    wd = w_down[0].astype(BF16)

    n_small = N_META + nb * n_tok
    x_small = jnp.concatenate([meta_tokens.astype(F32), x_sample.reshape(nb * n_tok, d)], axis=0)
    zh_s, q_s, kf_s, kb_s, vf_s, vb_s = _inproj_small(x_small, n1, w_in_bf, gmat, qw, kw)

    _, s_meta = _hgrn(zh_s[:N_META][None], hg_lb_logits, hg_onw,
                      jnp.zeros((1, HEADS, HEAD_W, HEAD_W), F32), shared_state=False, out_dtype=F32)
    oa_s, s_sample = _hgrn(zh_s[N_META:].reshape(nb, n_tok, HG_COLS), hg_lb_logits, hg_onw,
                           state_hgrn[0], shared_state=False, out_dtype=F32)

    x_p = x_prompt.reshape(bsz * seq, d)
    zh_p, qt_p, kf_p, kb_p, vf_p, vt_p = _inproj_prompt(
        x_p, n1, w_in_bf, gmat, qw, kw, kf_s[:N_META].reshape(N_META * HEADS, HEAD_W),
        vf_s[:N_META].reshape(N_META * HEADS, HEAD_W), seq, TOKEN_TILE)
    oa_p, s_prompt = _hgrn(zh_p.reshape(bsz, seq, HG_COLS), hg_lb_logits, hg_onw, s_meta,
                           shared_state=True, out_dtype=BF16)

    r = jnp.arange(blk)
    tb = jnp.stack([_t5_bias(r[None, :] - r[:, None], rel_bias),
                    _t5_bias(blk + r[None, :] - r[:, None], rel_bias)])
    tb = jnp.concatenate([tb, tb], axis=3)
    mrow = jnp.arange(128)
    t_meta0 = _t5_bias(N_META + r[None, :] - mrow[:, None], rel_bias)
    t_meta = jnp.stack([t_meta0, jnp.zeros_like(t_meta0)])
    t_meta = jnp.where(mrow[:, None] < N_META, t_meta, MASK_VALUE)
    t_meta = jnp.concatenate([t_meta, t_meta], axis=3)
    pad_meta = lambda a: jnp.concatenate([a, jnp.zeros((128 - N_META, w_attn), a.dtype)], axis=0)
    vmt = pad_meta(vb_s[:N_META]).T.reshape(HEADS, HEAD_W, 128)
    vmt = jnp.concatenate([vmt, jnp.ones((HEADS, VT_PAD, 128), BF16)], axis=1).reshape(VT_ROWS, 128)
    ob_p = _pattn(qt_p, kb_p.reshape(bsz, seq, w_attn), vt_p, pad_meta(kb_s[:N_META]),
                  vmt, tb, t_meta, lam_p, da_onw.reshape(HEAD_W, 1), lam_init)

    q_tok = q_s[N_META:].reshape(nb, n_tok, HEADS, 1, 2, DA_DH)
    q_maps = q_tok * jnp.eye(2, dtype=BF16).reshape(1, 1, 1, 2, 2, 1)
    n_rows = 2 * HEADS * n_tok
    qall = jnp.transpose(q_maps, (0, 2, 3, 1, 4, 5)).reshape(nb, n_rows, HEAD_W)
    prow = psize * HEADS
    pad_new = lambda a: jnp.concatenate(
        [a.reshape(nb, n_tok * HEADS, HEAD_W),
         jnp.zeros((nb, prow - n_tok * HEADS, HEAD_W), a.dtype)], axis=1)
    s_idx = jnp.tile(jnp.arange(n_tok), 2 * HEADS)
    head_of_row = jnp.repeat(jnp.arange(HEADS), 2 * n_tok)
    key_r = jnp.arange(prow) // HEADS
    own_head = head_of_row[:, None] == (jnp.arange(prow) % HEADS)[None, :]
    dist_last = (past_len + s_idx[:, None]) - (past_len - psize + key_r[None, :])
    dist_new = jnp.where(key_r[None, :] < n_tok, s_idx[:, None] - key_r[None, :], -1)
    pick = lambda t: jnp.where(own_head, t[head_of_row, jnp.arange(n_rows)], MASK_VALUE)
    t_last = pick(_t5_bias(dist_last, rel_bias))
    t_new = pick(_t5_bias(dist_new, rel_bias))
    t_mask = jnp.where(own_head, 0.0, MASK_VALUE).astype(F32)
    ob_s = _sattn(page_table, qall, pad_new(kb_s[N_META:]), pad_new(vb_s[N_META:]), t_last, t_new,
                  t_mask, lam_p, da_onw, cache_k.reshape(-1, HEAD_W), cache_v.reshape(-1, HEAD_W),
                  lam_init, n_tok)

    h_p = _merge(x_p, oa_p.reshape(bsz * seq, w_attn), ob_p.reshape(bsz * seq, w_attn),
                 n1, w_in_bf, w_ha, w_da, w_o0, TOKEN_TILE)
    y_p = _moe(h_p, n2, wr_hi, wr_lo, br, wg, wu, wd, TOKEN_TILE)
    x_s = x_small[N_META:]
    h_s = _merge(x_s, oa_s.reshape(nb * n_tok, w_attn).astype(BF16),
                 ob_s.reshape(nb * n_tok, w_attn).astype(BF16), n1, w_in_bf, w_ha, w_da, w_o0,
                 nb * n_tok)
    y_s = _moe(h_s, n2, wr_hi, wr_lo, br, wg, wu, wd, nb * n_tok)

    return (y_p.reshape(bsz, seq, d),
            y_s.reshape(nb, n_tok, d),
            kf_p.reshape(1, bsz, seq + N_META, HEADS, HEAD_W),
            vf_p.reshape(1, bsz, seq + N_META, HEADS, HEAD_W),
            s_prompt[None],
            kf_s[N_META:].reshape(1, nb, n_tok, HEADS, HEAD_W),
            vf_s[N_META:].reshape(1, nb, n_tok, HEADS, HEAD_W),
            s_sample[None])
```

```python
import functools
import math

import jax
import jax.numpy as jnp
from jax import lax
from jax.experimental import pallas as pl
from jax.experimental.pallas import tpu as pltpu

F32 = jnp.float32
BF16 = jnp.bfloat16

EPS = 1e-6
N_META = 16
HEADS = 4
HEAD_W = 128
DA_DH = 64
N_BUCKETS = 32
MAX_DISTANCE = 128
N_GROUPS = 4
EXPERTS_PER_GROUP = 4
N_EXPERTS = N_GROUPS * EXPERTS_PER_GROUP
MASK_VALUE = -1e30
HG_CHUNK = 128
HG_STEP_CHUNKS = 8
HG_BASE = 8
ATT_BLOCK = 256
VT_PAD = 16
VT_ROWS = HEADS * (HEAD_W + VT_PAD)
FAR_UNROLL = 8
PAGES_PER_STEP = 16
PAGE_SLOTS = 3
WIDTH = HEADS * HEAD_W
HG_COLS = 4 * WIDTH
IN_A_COLS = HG_COLS + 3 * WIDTH
TOKEN_TILE = 512
LOG2E = math.log2(math.e)
VMEM_LIMIT = 56 * 1024 * 1024


def _dot(a, b):
    return jnp.dot(a, b, preferred_element_type=F32)


def _dot_nt(a, b):
    return lax.dot_general(a, b, (((1,), (1,)), ((), ())), preferred_element_type=F32)


def _dot_tn(a, b):
    return lax.dot_general(a, b, (((0,), (0,)), ((), ())), preferred_element_type=F32)


def _sigmoid(x):
    return 0.5 + 0.5 * jnp.tanh(0.5 * x)


def _silu(x):
    y = 0.5 * x
    return y + y * jnp.tanh(y)


def _rms_rows(x, w):
    return x * lax.rsqrt(jnp.mean(x * x, axis=-1, keepdims=True) + EPS) * w


def _const_spec(shape):
    zeros = (0,) * len(shape)
    return pl.BlockSpec(shape, lambda *_: zeros)


def _inproj_values(x_ref, n1_ref, w_ref, g_ref, qw_ref, kw_ref):
    xn = _rms_rows(x_ref[...], n1_ref[...]).astype(BF16)
    g = g_ref[...]

    def group_norm(z, w):
        sq = z * z
        hi = sq.astype(BF16)
        lo = (sq - hi.astype(F32)).astype(BF16)
        ms = _dot(hi, g) + _dot(lo, g)
        return z * lax.rsqrt(ms + EPS) * w

    q0, k0, v0 = HG_COLS, HG_COLS + WIDTH, HG_COLS + 2 * WIDTH
    zh = _dot(xn, w_ref[:, 0:HG_COLS])
    qn = group_norm(_dot(xn, w_ref[:, q0:k0]), qw_ref[...])
    kn = group_norm(_dot(xn, w_ref[:, k0:v0]), kw_ref[...])
    zv = _dot(xn, w_ref[:, v0:IN_A_COLS])
    return zh, qn, kn, zv


def _inproj_small_kernel(x_ref, n1_ref, w_ref, g_ref, qw_ref, kw_ref,
                         zh_ref, q_ref, kf_ref, kb_ref, vf_ref, vb_ref):
    zh, qn, kn, zv = _inproj_values(x_ref, n1_ref, w_ref, g_ref, qw_ref, kw_ref)
    zh_ref[...] = zh
    q_ref[...] = qn.astype(BF16)
    kf_ref[...] = kn
    kb_ref[...] = kn.astype(BF16)
    vf_ref[...] = zv
    vb_ref[...] = zv.astype(BF16)


def _inproj_prompt_kernel(x_ref, n1_ref, w_ref, g_ref, qw_ref, kw_ref, km_ref, vm_ref,
                          zh_ref, qt_ref, kf_hbm, kb_ref, vf_hbm, vt_ref,
                          kst, vst, sem, msem, *, seq):
    zh, qn, kn, zv = _inproj_values(x_ref, n1_ref, w_ref, g_ref, qw_ref, kw_ref)
    tm = zh.shape[0]
    tiles = seq // tm
    i = pl.program_id(0)
    n_steps = pl.num_programs(0)
    slot = i % 2

    def batch_row0(step):
        return (step // tiles) * (N_META + seq) * HEADS

    def tile_copies(step, from_slot):
        row0 = batch_row0(step) + (N_META + (step % tiles) * tm) * HEADS
        rows = pl.ds(pl.multiple_of(row0, 8 * HEADS), tm * HEADS)
        return [pltpu.make_async_copy(kst.at[from_slot], kf_hbm.at[rows, :], sem.at[from_slot, 0]),
                pltpu.make_async_copy(vst.at[from_slot], vf_hbm.at[rows, :], sem.at[from_slot, 1])]

    zh_ref[...] = zh
    kb_ref[...] = kn.astype(BF16)
    qt = qn.T
    vt = zv.T
    for c in range(tm // ATT_BLOCK):
        cols = slice(c * ATT_BLOCK, (c + 1) * ATT_BLOCK)
        qt_ref[c] = qt[:, cols].astype(BF16)
        for h in range(HEADS):
            r0 = h * (HEAD_W + VT_PAD)
            vt_ref[c, r0:r0 + HEAD_W, :] = vt[h * HEAD_W:(h + 1) * HEAD_W, cols].astype(BF16)
            vt_ref[c, r0 + HEAD_W:r0 + HEAD_W + VT_PAD, :] = jnp.ones((VT_PAD, ATT_BLOCK), BF16)

    @pl.when(i >= 2)
    def _():
        for cp in tile_copies(i - 2, slot):
            cp.wait()

    for h in range(HEADS):
        hs = slice(h * HEAD_W, (h + 1) * HEAD_W)
        kst[slot, pl.ds(h, tm, stride=HEADS), :] = kn[:, hs]
        vst[slot, pl.ds(h, tm, stride=HEADS), :] = zv[:, hs]
    for cp in tile_copies(i, slot):
        cp.start()

    @pl.when(i % tiles == 0)
    def _():
        rows = pl.ds(pl.multiple_of(batch_row0(i), 8 * HEADS), N_META * HEADS)
        meta = [pltpu.make_async_copy(km_ref, kf_hbm.at[rows, :], msem.at[0]),
                pltpu.make_async_copy(vm_ref, vf_hbm.at[rows, :], msem.at[1])]
        for cp in meta:
            cp.start()
        for cp in meta:
            cp.wait()

    @pl.when(i == n_steps - 1)
    def _():
        @pl.when(i >= 1)
        def _():
            for cp in tile_copies(i - 1, 1 - slot):
                cp.wait()
        for cp in tile_copies(i, slot):
            cp.wait()


def _w_in_spec(d):
    return pl.BlockSpec((d, IN_A_COLS), lambda i: (0, 0))


def _inproj_small(x2d, n1, w_in_bf, gmat, qw, kw):
    n, d = x2d.shape
    full = lambda width: _const_spec((n, width))
    return pl.pallas_call(
        _inproj_small_kernel,
        grid=(1,),
        in_specs=[full(d), _const_spec((1, d)), _w_in_spec(d), _const_spec(gmat.shape),
                  _const_spec((1, WIDTH)), _const_spec((1, WIDTH))],
        out_specs=[full(HG_COLS)] + [full(WIDTH)] * 5,
        out_shape=[jax.ShapeDtypeStruct((n, HG_COLS), F32), jax.ShapeDtypeStruct((n, WIDTH), BF16),
                   jax.ShapeDtypeStruct((n, WIDTH), F32), jax.ShapeDtypeStruct((n, WIDTH), BF16),
                   jax.ShapeDtypeStruct((n, WIDTH), F32), jax.ShapeDtypeStruct((n, WIDTH), BF16)],
        compiler_params=pltpu.CompilerParams(dimension_semantics=("arbitrary",),
                                             vmem_limit_bytes=VMEM_LIMIT),
        name="inproj_small",
    )(x2d, n1, w_in_bf, gmat, qw, kw)


def _inproj_prompt(x2d, n1, w_in_bf, gmat, qw, kw, k_meta, v_meta, seq, tm):
    n, d = x2d.shape
    cache_shape = jax.ShapeDtypeStruct(((n // seq) * (N_META + seq) * HEADS, HEAD_W), F32)
    assert seq % tm == 0 and tm % ATT_BLOCK == 0
    per = tm // ATT_BLOCK
    row = lambda width: pl.BlockSpec((tm, width), lambda i: (i, 0))
    t_spec = lambda rows: pl.BlockSpec((per, rows, ATT_BLOCK), lambda i: (i, 0, 0))
    t_shape = lambda rows: jax.ShapeDtypeStruct((n // ATT_BLOCK, rows, ATT_BLOCK), BF16)
    hbm = pl.BlockSpec(memory_space=pl.ANY)
    return pl.pallas_call(
        functools.partial(_inproj_prompt_kernel, seq=seq),
        grid=(n // tm,),
        in_specs=[row(d), _const_spec((1, d)), _w_in_spec(d), _const_spec(gmat.shape),
                  _const_spec((1, WIDTH)), _const_spec((1, WIDTH)),
                  _const_spec(k_meta.shape), _const_spec(v_meta.shape)],
        out_specs=[row(HG_COLS), t_spec(WIDTH), hbm, row(WIDTH), hbm, t_spec(VT_ROWS)],
        out_shape=[jax.ShapeDtypeStruct((n, HG_COLS), F32), t_shape(WIDTH), cache_shape,
                   jax.ShapeDtypeStruct((n, WIDTH), BF16), cache_shape, t_shape(VT_ROWS)],
        scratch_shapes=[pltpu.VMEM((2, tm * HEADS, HEAD_W), F32),
                        pltpu.VMEM((2, tm * HEADS, HEAD_W), F32),
                        pltpu.SemaphoreType.DMA((2, 2)), pltpu.SemaphoreType.DMA((2,))],
        compiler_params=pltpu.CompilerParams(dimension_semantics=("arbitrary",),
                                             vmem_limit_bytes=VMEM_LIMIT),
        name="inproj",
    )(x2d, n1, w_in_bf, gmat, qw, kw, k_meta, v_meta)


def _hgrn_kernel(z_ref, lbl_ref, onw_ref, s0_ref, o_ref, sout_ref, st_scr, *, rows):
    C = HG_CHUNK
    W = HEADS * HEAD_W
    n_sub = max(rows // C, 1)
    n_valid = min(rows, C)
    c = pl.program_id(1)
    heads = [slice(h * HEAD_W, (h + 1) * HEAD_W) for h in range(HEADS)]

    @pl.when(c == 0)
    def _():
        for h in range(HEADS):
            st_scr[h] = s0_ref[0, h].T

    lg = lbl_ref[...]
    e = jnp.exp(lg - jnp.max(lg, axis=0, keepdims=True))
    lb = e[0:1] / jnp.sum(e, axis=0, keepdims=True)
    half = 0.5 * (1.0 - lb)
    onw = onw_ref[...]
    row = lax.broadcasted_iota(jnp.int32, (C, C), 0)
    col = lax.broadcasted_iota(jnp.int32, (C, C), 1)
    tri = jnp.where(col <= row, 1.0, 0.0).astype(BF16)

    def shr(x, width):
        return lax.shift_right_logical(x, int(math.log2(width)))

    def front(z, n_valid):
        hq, hf, hi, hg = z[:, 0:W], z[:, W:2 * W], z[:, 2 * W:3 * W], z[:, 3 * W:4 * W]
        q = _silu(hq)
        f = (lb + half) + half * jnp.tanh(0.5 * hf)
        k = 1.0 - f
        logf = jnp.log2(f)
        if n_valid < C:
            valid = lax.broadcasted_iota(jnp.int32, (C, W), 0) < n_valid
            k = jnp.where(valid, k, 0.0)
            logf = jnp.where(valid, logf, 0.0)
        x1 = logf.astype(BF16)
        r1 = logf - x1.astype(F32)
        x2 = r1.astype(BF16)
        x3 = (r1 - x2.astype(F32)).astype(BF16)
        b = _dot(tri, x1) + _dot(tri, x2) + _dot(tri, x3)
        return q, k, hi.astype(BF16), b, _silu(hg)

    def intra(q, k, b):
        b_last = b[C - 1:C]

        def block_row(width, r):
            b3 = b.reshape(C // width, width, W)
            return jnp.broadcast_to(b3[:, r:r + 1, :], b3.shape).reshape(C, W)

        def next_block_first(width):
            b3 = b.reshape(C // width, width, W)
            nxt = jnp.concatenate([b3[1:, 0:1, :], b_last[None]], axis=0)
            return jnp.broadcast_to(nxt, b3.shape).reshape(C, W)

        mid = block_row(HG_BASE, HG_BASE // 2)
        qe = (q * jnp.exp2(b - mid)).astype(BF16)
        ke = (k * jnp.exp2(mid - b)).astype(BF16)
        mask = (shr(row, HG_BASE) == shr(col, HG_BASE)) & (col <= row)
        scores = [jnp.where(mask, _dot_nt(qe[:, hs], ke[:, hs]), 0.0) for hs in heads]
        width = HG_BASE
        while width < n_valid:
            qe = (q * jnp.exp2(b - block_row(width, 0))).astype(BF16)
            ke = (k * jnp.exp2(next_block_first(width) - b)).astype(BF16)
            mask = ((shr(row, width) == shr(col, width) + 1)
                    & (shr(row, 2 * width) == shr(col, 2 * width)))
            scores = [jnp.where(mask, _dot_nt(qe[:, hs], ke[:, hs]), sc)
                      for hs, sc in zip(heads, scores)]
            width *= 2
        qb = (q * jnp.exp2(b)).astype(BF16)
        kd = (k * jnp.exp2(b_last - b)).astype(BF16)
        return [sc.astype(BF16) for sc in scores], qb, kd, jnp.exp2(b_last)

    fronts = []
    for u in range(n_sub):
        z = z_ref[0, u * n_valid:(u + 1) * n_valid, :]
        if n_valid < C:
            z = jnp.concatenate([z, jnp.zeros((C - n_valid, z.shape[1]), F32)], axis=0)
        fronts.append(front(z, n_valid))
    intras = [intra(q, k, b) for q, k, _, b, _ in fronts]
    for u in range(n_sub):
        _, _, v_bf, _, gate = fronts[u]
        scores, qb, kd, decay = intras[u]
        outs = []
        for h, hs in enumerate(heads):
            st = st_scr[h]
            outs.append(_dot_nt(qb[:, hs], st.astype(BF16)) + _dot(scores[h], v_bf[:, hs]))
            st_scr[h] = st * decay[:, hs] + _dot_tn(v_bf[:, hs], kd[:, hs])
        for h, hs in enumerate(heads):
            on = _rms_rows(outs[h], onw[:, hs]) * gate[:, hs]
            o_ref[0, u * n_valid:(u + 1) * n_valid, hs] = on[:n_valid].astype(o_ref.dtype)

    @pl.when(c == pl.num_programs(1) - 1)
    def _():
        for h in range(HEADS):
            sout_ref[0, h] = st_scr[h].T


def _hgrn(zh, lbl, onw, s0, *, shared_state, out_dtype):
    nseq, t, _ = zh.shape
    rows = min(t, HG_STEP_CHUNKS * HG_CHUNK)
    assert t % rows == 0 and rows % 8 == 0 and (rows <= HG_CHUNK or rows % HG_CHUNK == 0)
    s_map = (lambda b, c: (0, 0, 0, 0)) if shared_state else (lambda b, c: (b, 0, 0, 0))
    return pl.pallas_call(
        functools.partial(_hgrn_kernel, rows=rows),
        grid=(nseq, t // rows),
        in_specs=[pl.BlockSpec((1, rows, HG_COLS), lambda b, c: (b, c, 0)),
                  _const_spec(lbl.shape), _const_spec((1, WIDTH)),
                  pl.BlockSpec((1, HEADS, HEAD_W, HEAD_W), s_map)],
        out_specs=[pl.BlockSpec((1, rows, WIDTH), lambda b, c: (b, c, 0)),
                   pl.BlockSpec((1, HEADS, HEAD_W, HEAD_W), lambda b, c: (b, 0, 0, 0))],
        out_shape=[jax.ShapeDtypeStruct((nseq, t, WIDTH), out_dtype),
                   jax.ShapeDtypeStruct((nseq, HEADS, HEAD_W, HEAD_W), F32)],
        scratch_shapes=[pltpu.VMEM((HEADS, HEAD_W, HEAD_W), F32)],
        compiler_params=pltpu.CompilerParams(dimension_semantics=("parallel", "arbitrary"),
                                             vmem_limit_bytes=VMEM_LIMIT),
        name="hgrn",
    )(zh, lbl, onw, s0)


def _lambda_value(lp, lam_init):
    s1 = jnp.sum(lp[0:1] * lp[1:2], axis=1, keepdims=True)
    s2 = jnp.sum(lp[2:3] * lp[3:4], axis=1, keepdims=True)
    return jnp.exp(s1) - jnp.exp(s2) + lam_init


def _softmax_update(s, v_bf, m_ref, l_ref, acc_ref, first):
    m_cur = jnp.max(s, axis=1, keepdims=True)
    if first:
        m_new = m_cur
        p = jnp.exp2(s - m_new)
        l_ref[...] = jnp.sum(p, axis=1, keepdims=True)
        acc_ref[...] = _dot(p.astype(BF16), v_bf)
    else:
        m_prev = m_ref[...]
        m_new = jnp.maximum(m_prev, m_cur)
        alpha = jnp.exp2(m_prev - m_new)
        p = jnp.exp2(s - m_new)
        l_ref[...] = alpha * l_ref[...] + jnp.sum(p, axis=1, keepdims=True)
        acc_ref[...] = alpha * acc_ref[...] + _dot(p.astype(BF16), v_bf)
    m_ref[...] = m_new


def _pattn_kernel(qt_ref, k_ref, vt_ref, km_ref, vmt_ref, tb_ref, tm_ref, lam_ref, onw_ref, o_ref,
                  qz_scr, m_scr, acc_scr, s_scr, *, lam_init):
    blk = ATT_BLOCK
    i = pl.program_id(1)
    sub = lax.broadcasted_iota(jnp.int32, (HEAD_W, blk), 0)
    for h in range(HEADS):
        qh = qt_ref[0, h * HEAD_W:(h + 1) * HEAD_W, :]
        zero = jnp.zeros_like(qh)
        qz_scr[h] = jnp.concatenate([jnp.where(sub < DA_DH, qh, zero),
                                     jnp.where(sub >= DA_DH, qh, zero)], axis=1)

    hv = lambda h: slice(h * (HEAD_W + VT_PAD), (h + 1) * (HEAD_W + VT_PAD))

    meta_sel = jnp.minimum(i, 1)
    meta_keys = "meta"

    def key_blocks(blocks, first=False, last=False):
        def qk_raw(j, h):
            hs = slice(h * HEAD_W, (h + 1) * HEAD_W)
            if j is meta_keys:
                return _dot(km_ref[:, hs], qz_scr[h])
            off = pl.multiple_of(j * blk, blk)
            return _dot(k_ref[0, pl.ds(off, blk), hs], qz_scr[h])

        def biased(s, j, table, h):
            if j is meta_keys:
                return s + tm_ref[meta_sel, h]
            return s if table is None else s + tb_ref[table, h]

        def soft(j, h, s):
            m_cur = jnp.max(s, axis=0, keepdims=True)
            if j is meta_keys:
                m_scr[h] = m_cur
                return jnp.exp2(s - m_cur).astype(BF16), None
            m_prev = m_scr[h]
            m_new = jnp.maximum(m_prev, m_cur)
            m_scr[h] = m_new
            return jnp.exp2(s - m_new).astype(BF16), jnp.exp2(m_prev - m_new)

        def pv(j, h, p, alpha):
            if j is meta_keys:
                acc_scr[h] = _dot(vmt_ref[hv(h), :], p)
            else:
                acc_scr[h] = alpha * acc_scr[h] + _dot(vt_ref[j, hv(h), :], p)

        stages = [(j, table, h) for j, table in blocks for h in range(HEADS)]
        j_after = 0 if blocks[-1][0] is meta_keys else blocks[-1][0] + 1
        s_cur = biased(qk_raw(stages[0][0], 0) if first else s_scr[...], *stages[0])
        pend = None
        for n, (j, table, h) in enumerate(stages):
            if n + 1 < len(stages):
                jn, tn, hn = stages[n + 1]
                s_next = biased(qk_raw(jn, hn), jn, tn, hn)
            else:
                s_next = None
                if not last:
                    s_scr[...] = qk_raw(j_after, 0)
            if pend is not None:
                pv(stages[n - 1][0], stages[n - 1][2], *pend)
            pend = soft(j, h, s_cur)
            s_cur = s_next
        pv(stages[-1][0], stages[-1][2], *pend)

    key_blocks([(meta_keys, None)], first=True)
    n_far = jnp.maximum(i - 1, 0)
    done = 0
    width = 1
    while width < FAR_UNROLL:
        start = done

        @pl.when((n_far & width) != 0)
        def _(start=start, width=width):
            key_blocks([(start + u, None) for u in range(width)])

        done = done + (n_far & width)
        width *= 2
    n_rem = done

    def far_body(t, carry):
        j = n_rem + FAR_UNROLL * t
        key_blocks([(j + u, None) for u in range(FAR_UNROLL)])
        return carry

    lax.fori_loop(0, n_far // FAR_UNROLL, far_body, 0)

    @pl.when(i >= 1)
    def _():
        key_blocks([(i - 1, 1), (i, 0)], last=True)

    @pl.when(i == 0)
    def _():
        key_blocks([(0, 0)], last=True)

    lam = _lambda_value(lam_ref[...], lam_init)
    onw = onw_ref[...]
    for h in range(HEADS):
        acc = acc_scr[h]
        a = acc[:HEAD_W] / acc[HEAD_W:HEAD_W + 1]
        ot = a[:, :blk] - lam * a[:, blk:]
        ot = ot * lax.rsqrt(jnp.mean(ot * ot, axis=0, keepdims=True) + EPS) * onw * (1.0 - lam_init)
        o_ref[0, :, h * HEAD_W:(h + 1) * HEAD_W] = ot.T.astype(o_ref.dtype)


def _pattn(qt, k, vt, km, vmt, tb, tmeta, lam_p, onw_col, lam_init):
    bsz, t, w = k.shape
    blk = ATT_BLOCK
    nblk = t // blk
    assert t % blk == 0
    return pl.pallas_call(
        functools.partial(_pattn_kernel, lam_init=lam_init),
        grid=(bsz, nblk),
        in_specs=[pl.BlockSpec((1, w, blk), lambda b, i: (b * nblk + i, 0, 0)),
                  pl.BlockSpec((1, t, w), lambda b, i: (b, 0, 0)),
                  pl.BlockSpec((nblk, VT_ROWS, blk), lambda b, i: (b, 0, 0)),
                  _const_spec(km.shape), _const_spec(vmt.shape), _const_spec(tb.shape),
                  _const_spec(tmeta.shape), _const_spec(lam_p.shape), _const_spec(onw_col.shape)],
        out_specs=pl.BlockSpec((1, blk, w), lambda b, i: (b, i, 0)),
        out_shape=jax.ShapeDtypeStruct((bsz, t, w), BF16),
        scratch_shapes=[pltpu.VMEM((HEADS, HEAD_W, 2 * blk), BF16),
                        pltpu.VMEM((HEADS, 1, 2 * blk), F32),
                        pltpu.VMEM((HEADS, HEAD_W + VT_PAD, 2 * blk), F32),
                        pltpu.VMEM((blk, 2 * blk), F32)],
        compiler_params=pltpu.CompilerParams(dimension_semantics=("parallel", "arbitrary"),
                                             vmem_limit_bytes=VMEM_LIMIT),
        name="prompt_attn",
    )(qt, k, vt, km, vmt, tb, tmeta, lam_p, onw_col)


def _sattn_kernel(pt_ref, q_ref, kn_ref, vn_ref, tl_ref, tn_ref, tmask_ref, lam_ref, onw_ref,
                  ck_hbm, cv_hbm, o_ref, kbuf, vbuf, sem, m_scr, l_scr, acc_scr,
                  *, lam_init, n_tok, n_pages):
    pg = PAGES_PER_STEP
    prow = kbuf.shape[2]
    b, g = pl.program_id(0), pl.program_id(1)
    n_steps = pl.num_programs(1)
    t = b * n_steps + g
    n_total = pl.num_programs(0) * n_steps
    slot = t % PAGE_SLOTS

    def page_copies(step, to_slot):
        base = (step // n_steps) * n_pages + (step % n_steps) * pg
        copies = []
        for u in range(pg):
            rows = pl.ds(pl.multiple_of(pt_ref[base + u] * prow, prow), prow)
            copies.append(pltpu.make_async_copy(ck_hbm.at[rows, :], kbuf.at[to_slot, u],
                                                sem.at[to_slot, 0]))
            copies.append(pltpu.make_async_copy(cv_hbm.at[rows, :], vbuf.at[to_slot, u],
                                                sem.at[to_slot, 1]))
        return copies

    ahead = PAGE_SLOTS - 1

    @pl.when(t == 0)
    def _():
        for d in range(ahead):
            @pl.when(d < n_total)
            def _(d=d):
                for cp in page_copies(d, d):
                    cp.start()

    @pl.when(t + ahead < n_total)
    def _():
        for cp in page_copies(t + ahead, (t + ahead) % PAGE_SLOTS):
            cp.start()

    q = q_ref[0]

    @pl.when(g == 0)
    def _():
        s = _dot_nt(q, kn_ref[0]) + tn_ref[...]
        _softmax_update(s, vn_ref[0], m_scr, l_scr, acc_scr, True)

    for cp in page_copies(t, slot):
        cp.wait()

    is_last = g == n_steps - 1
    tmask = tmask_ref[...]
    ss = []
    for u in range(pg):
        s = _dot_nt(q, kbuf[slot, u].astype(BF16))
        ss.append(s + (jnp.where(is_last, tl_ref[...], tmask) if u == pg - 1 else tmask))
    m_prev = m_scr[...]
    m_new = m_prev
    for s in ss:
        m_new = jnp.maximum(m_new, jnp.max(s, axis=1, keepdims=True))
    alpha = jnp.exp2(m_prev - m_new)
    l_new = alpha * l_scr[...]
    acc = alpha * acc_scr[...]
    for u in range(pg):
        p = jnp.exp2(ss[u] - m_new)
        l_new = l_new + jnp.sum(p, axis=1, keepdims=True)
        acc = acc + _dot(p.astype(BF16), vbuf[slot, u].astype(BF16))
    m_scr[...] = m_new
    l_scr[...] = l_new
    acc_scr[...] = acc

    @pl.when(is_last)
    def _():
        lam = _lambda_value(lam_ref[...], lam_init)
        onw = onw_ref[...]
        a = acc_scr[...] / l_scr[...]
        for h in range(HEADS):
            r1 = (2 * h) * n_tok
            r2 = (2 * h + 1) * n_tok
            o = a[r1:r1 + n_tok] - lam * a[r2:r2 + n_tok]
            o_ref[0, :, h * HEAD_W:(h + 1) * HEAD_W] = (
                _rms_rows(o, onw) * (1.0 - lam_init)).astype(o_ref.dtype)


def _sattn(page_table, qall, knew, vnew, tlast, tnew, tmask, lam_p, onw, ck2, cv2, lam_init, n_tok):
    nb, n_pages = page_table.shape
    pg = PAGES_PER_STEP
    assert n_pages % pg == 0
    nrow = qall.shape[1]
    prow = knew.shape[1]
    pt_flat = page_table.reshape(-1)
    per_b = lambda shape: pl.BlockSpec((1,) + shape, lambda b, g, pt: (b, 0, 0))
    const = lambda shape: pl.BlockSpec(shape, lambda b, g, pt: (0,) * len(shape))
    hbm = pl.BlockSpec(memory_space=pl.ANY)
    grid_spec = pltpu.PrefetchScalarGridSpec(
        num_scalar_prefetch=1,
        grid=(nb, n_pages // pg),
        in_specs=[per_b((nrow, HEAD_W)), per_b((prow, HEAD_W)), per_b((prow, HEAD_W)),
                  const(tlast.shape), const(tnew.shape), const(tmask.shape),
                  const(lam_p.shape), const(onw.shape), hbm, hbm],
        out_specs=per_b((n_tok, HEADS * HEAD_W)),
        scratch_shapes=[pltpu.VMEM((PAGE_SLOTS, pg, prow, HEAD_W), F32),
                        pltpu.VMEM((PAGE_SLOTS, pg, prow, HEAD_W), F32),
                        pltpu.SemaphoreType.DMA((PAGE_SLOTS, 2)),
                        pltpu.VMEM((nrow, 1), F32), pltpu.VMEM((nrow, 1), F32),
                        pltpu.VMEM((nrow, HEAD_W), F32)],
    )
    return pl.pallas_call(
        functools.partial(_sattn_kernel, lam_init=lam_init, n_tok=n_tok, n_pages=n_pages),
        grid_spec=grid_spec,
        out_shape=jax.ShapeDtypeStruct((nb, n_tok, HEADS * HEAD_W), F32),
        compiler_params=pltpu.CompilerParams(dimension_semantics=("arbitrary", "arbitrary"),
                                             vmem_limit_bytes=VMEM_LIMIT),
        name="sample_attn",
    )(pt_flat, qall, knew, vnew, tlast, tnew, tmask, lam_p, onw, ck2, cv2)


def _merge_kernel(x_ref, oa_ref, ob_ref, n1_ref, wg_ref, wha_ref, wda_ref, wo_ref, h_ref):
    x = x_ref[...]
    d = x.shape[1]
    xn = _rms_rows(x, n1_ref[...]).astype(BF16)
    gates = _dot(xn, wg_ref[...])
    ya = _dot(oa_ref[...], wha_ref[...])
    yb = _dot(ob_ref[...], wda_ref[...])
    merged = _sigmoid(gates[:, :d]) * ya + _sigmoid(gates[:, d:]) * yb
    h_ref[...] = x + _dot(merged.astype(BF16), wo_ref[...])


def _merge(x2d, oa, ob, n1, w_in_bf, w_ha, w_da, w_o, tm):
    n, d = x2d.shape
    row = lambda width: pl.BlockSpec((tm, width), lambda i: (i, 0))
    gate_spec = pl.BlockSpec((pl.Element(d), pl.Element(2 * d)), lambda i: (0, IN_A_COLS))
    return pl.pallas_call(
        _merge_kernel,
        grid=(n // tm,),
        in_specs=[row(d), row(WIDTH), row(WIDTH), _const_spec((1, d)), gate_spec,
                  _const_spec(w_ha.shape), _const_spec(w_da.shape), _const_spec(w_o.shape)],
        out_specs=row(d),
        out_shape=jax.ShapeDtypeStruct((n, d), F32),
        compiler_params=pltpu.CompilerParams(dimension_semantics=("parallel",),
                                             vmem_limit_bytes=VMEM_LIMIT),
        name="merge",
    )(x2d, oa, ob, n1, w_in_bf, w_ha, w_da, w_o)


def _moe_kernel(h_ref, n2_ref, wr_hi_ref, wr_lo_ref, br_ref, wg_ref, wu_ref, wd_ref, y_ref):
    h = h_ref[...]
    hn = _rms_rows(h, n2_ref[...])
    hn_hi = hn.astype(BF16)
    hn_lo = (hn - hn_hi.astype(F32)).astype(BF16)
    logits = (_dot(hn_hi, wr_hi_ref[...]) + _dot(hn_lo, wr_hi_ref[...])
              + _dot(hn_hi, wr_lo_ref[...]) + br_ref[...])
    lane = lax.broadcasted_iota(jnp.int32, logits.shape, 1)
    big = jnp.int32(1 << 20)
    ninf = -jnp.inf

    def first_lane(mask):
        return jnp.min(jnp.where(mask, lane, big), axis=1, keepdims=True)

    is_g = lane < N_GROUPS
    g_max = jnp.max(jnp.where(is_g, logits, ninf), axis=1, keepdims=True)
    g_idx = first_lane(is_g & (logits == g_max))
    g_sum = jnp.sum(jnp.where(is_g, jnp.exp(logits - g_max), 0.0), axis=1, keepdims=True)
    p_g = 1.0 / g_sum
    e_lane = lane - N_GROUPS
    in_group = ((e_lane >= 0) & (e_lane < N_EXPERTS)
                & (lax.shift_right_arithmetic(e_lane, 2) == g_idx))
    e_max = jnp.max(jnp.where(in_group, logits, ninf), axis=1, keepdims=True)
    e_exp = jnp.where(in_group, jnp.exp(logits - e_max), 0.0)
    e_prob = e_exp / jnp.sum(e_exp, axis=1, keepdims=True)
    p1 = jnp.max(jnp.where(in_group, e_prob, -1.0), axis=1, keepdims=True)
    i1 = first_lane(in_group & (e_prob == p1))
    rest = in_group & (lane != i1)
    p2 = jnp.max(jnp.where(rest, e_prob, -1.0), axis=1, keepdims=True)
    i2 = first_lane(rest & (e_prob == p2))
    denom = p1 + p2
    combine = (jnp.where(lane == i1, p_g * p1 / denom, 0.0)
               + jnp.where(lane == i2, p_g * p2 / denom, 0.0))

    acc = jnp.zeros_like(h)
    for ex in range(N_EXPERTS):
        gt = _dot(hn_hi, wg_ref[ex])
        up = _dot(hn_hi, wu_ref[ex])
        hh = _silu(gt) * up * combine[:, N_GROUPS + ex:N_GROUPS + ex + 1]
        acc = acc + _dot(hh.astype(BF16), wd_ref[ex])
    y_ref[...] = h + acc


def _moe(h2d, n2, wr_hi, wr_lo, br, wg, wu, wd, tm):
    n, d = h2d.shape
    row = pl.BlockSpec((tm, d), lambda i: (i, 0))
    single = lambda shape: pl.BlockSpec(shape, lambda i: (0,) * len(shape),
                                        pipeline_mode=pl.Buffered(1))
    return pl.pallas_call(
        _moe_kernel,
        grid=(n // tm,),
        in_specs=[row, _const_spec((1, d)), _const_spec(wr_hi.shape), _const_spec(wr_lo.shape),
                  _const_spec(br.shape), single(wg.shape), single(wu.shape), single(wd.shape)],
        out_specs=row,
        out_shape=jax.ShapeDtypeStruct((n, d), F32),
        compiler_params=pltpu.CompilerParams(dimension_semantics=("parallel",),
                                             vmem_limit_bytes=VMEM_LIMIT),
        name="moe",
    )(h2d, n2, wr_hi, wr_lo, br, wg, wu, wd)


def _t5_bias(dist, rel_bias):
    n = jnp.maximum(dist, 0)
    max_exact = N_BUCKETS // 2
    nf = jnp.maximum(n, 1).astype(F32)
    scaled = (jnp.log(nf / max_exact) / math.log(MAX_DISTANCE / max_exact)
              * (N_BUCKETS - max_exact))
    large = max_exact + jnp.floor(jnp.maximum(scaled, 0.0)).astype(jnp.int32)
    large = jnp.minimum(large, N_BUCKETS - 1)
    bucket = jnp.where(n < max_exact, n, large)
    rb = rel_bias.astype(F32)
    onehot = (bucket[..., None] == jnp.arange(N_BUCKETS)).astype(F32)
    bias = jnp.moveaxis(jnp.dot(onehot, rb, precision=lax.Precision.HIGHEST), -1, 0)
    far = rb[N_BUCKETS - 1].reshape((HEADS,) + (1,) * dist.ndim)
    return jnp.where(dist >= 0, (bias - far) * LOG2E, MASK_VALUE)


def kernel(x_prompt, x_sample, cache_k, cache_v, state_hgrn, page_table, meta_tokens, rel_bias,
           hg_lb_logits, norm1_w, w_in, hg_onorm_w, w_hg_out, q_norm_w, k_norm_w, da_lambda,
           da_onorm_w, w_da_out, w_o, norm2_w, w_router_group, b_router_group,
           w_router_expert, b_router_expert, w_gate, w_up, w_down):
    bsz, seq, d = x_prompt.shape
    nb, n_tok, _ = x_sample.shape
    depth = w_in.shape[0]
    assert depth == 1 and hg_lb_logits.shape[0] == 2
    n_pages = page_table.shape[1]
    psize = cache_k.shape[2]
    past_len = n_pages * psize
    blk = ATT_BLOCK
    assert blk >= MAX_DISTANCE and psize >= MAX_DISTANCE and n_tok <= 8
    lam_init = 0.8 - 0.6 * math.exp(-0.3 * 0)
    w_attn = WIDTH

    assert w_in.shape[2] == IN_A_COLS + 2 * d
    w_in_bf = w_in[0].astype(BF16)
    n1 = norm1_w[0].reshape(1, d)
    n2 = norm2_w[0].reshape(1, d)
    qw = (jnp.tile(q_norm_w[0], 2 * HEADS) * (DA_DH ** -0.5 * LOG2E)).reshape(1, w_attn)
    kw = jnp.tile(k_norm_w[0], 2 * HEADS).reshape(1, w_attn)
    gi = jnp.arange(w_attn) // DA_DH
    gmat = jnp.where(gi[:, None] == gi[None, :], 1.0 / DA_DH, 0.0).astype(BF16)
    hg_onw = jnp.tile(hg_onorm_w[0], HEADS).reshape(1, w_attn)
    da_onw = da_onorm_w[0].reshape(1, HEAD_W)
    lam_p = da_lambda[0].astype(F32)
    w_ha = w_hg_out[0].astype(BF16)
    w_da = w_da_out[0].astype(BF16)
    w_o0 = w_o[0].astype(BF16)
    wr = jnp.concatenate([w_router_group[0], w_router_expert[0],
                          jnp.zeros((d, 128 - N_GROUPS - N_EXPERTS), F32)], axis=1)
    wr_hi = wr.astype(BF16)
    wr_lo = (wr - wr_hi.astype(F32)).astype(BF16)
    br = jnp.concatenate([b_router_group[0], b_router_expert[0],
                          jnp.zeros((128 - N_GROUPS - N_EXPERTS,), F32)]).reshape(1, 128)
    wg = w_gate[0].astype(BF16)
    wu = w_up[0].astype(BF16)
    wd = w_down[0].astype(BF16)

    n_small = N_META + nb * n_tok
    x_small = jnp.concatenate([meta_tokens.astype(F32), x_sample.reshape(nb * n_tok, d)], axis=0)
    zh_s, q_s, kf_s, kb_s, vf_s, vb_s = _inproj_small(x_small, n1, w_in_bf, gmat, qw, kw)

    _, s_meta = _hgrn(zh_s[:N_META][None], hg_lb_logits, hg_onw,
                      jnp.zeros((1, HEADS, HEAD_W, HEAD_W), F32), shared_state=False, out_dtype=F32)
    oa_s, s_sample = _hgrn(zh_s[N_META:].reshape(nb, n_tok, HG_COLS), hg_lb_logits, hg_onw,
                           state_hgrn[0], shared_state=False, out_dtype=F32)

    x_p = x_prompt.reshape(bsz * seq, d)
    zh_p, qt_p, kf_p, kb_p, vf_p, vt_p = _inproj_prompt(
        x_p, n1, w_in_bf, gmat, qw, kw, kf_s[:N_META].reshape(N_META * HEADS, HEAD_W),
        vf_s[:N_META].reshape(N_META * HEADS, HEAD_W), seq, TOKEN_TILE)
    oa_p, s_prompt = _hgrn(zh_p.reshape(bsz, seq, HG_COLS), hg_lb_logits, hg_onw, s_meta,
                           shared_state=True, out_dtype=BF16)

    r = jnp.arange(blk)
    tb = jnp.stack([_t5_bias(r[None, :] - r[:, None], rel_bias),
                    _t5_bias(blk + r[None, :] - r[:, None], rel_bias)])
    tb = jnp.concatenate([tb, tb], axis=3)
    mrow = jnp.arange(128)
    t_meta0 = _t5_bias(N_META + r[None, :] - mrow[:, None], rel_bias)
    t_meta = jnp.stack([t_meta0, jnp.zeros_like(t_meta0)])
    t_meta = jnp.where(mrow[:, None] < N_META, t_meta, MASK_VALUE)
    t_meta = jnp.concatenate([t_meta, t_meta], axis=3)
    pad_meta = lambda a: jnp.concatenate([a, jnp.zeros((128 - N_META, w_attn), a.dtype)], axis=0)
    vmt = pad_meta(vb_s[:N_META]).T.reshape(HEADS, HEAD_W, 128)
    vmt = jnp.concatenate([vmt, jnp.ones((HEADS, VT_PAD, 128), BF16)], axis=1).reshape(VT_ROWS, 128)
    ob_p = _pattn(qt_p, kb_p.reshape(bsz, seq, w_attn), vt_p, pad_meta(kb_s[:N_META]),
                  vmt, tb, t_meta, lam_p, da_onw.reshape(HEAD_W, 1), lam_init)

    q_tok = q_s[N_META:].reshape(nb, n_tok, HEADS, 1, 2, DA_DH)
    q_maps = q_tok * jnp.eye(2, dtype=BF16).reshape(1, 1, 1, 2, 2, 1)
    n_rows = 2 * HEADS * n_tok
    qall = jnp.transpose(q_maps, (0, 2, 3, 1, 4, 5)).reshape(nb, n_rows, HEAD_W)
    prow = psize * HEADS
    pad_new = lambda a: jnp.concatenate(
        [a.reshape(nb, n_tok * HEADS, HEAD_W),
         jnp.zeros((nb, prow - n_tok * HEADS, HEAD_W), a.dtype)], axis=1)
    s_idx = jnp.tile(jnp.arange(n_tok), 2 * HEADS)
    head_of_row = jnp.repeat(jnp.arange(HEADS), 2 * n_tok)
    key_r = jnp.arange(prow) // HEADS
    own_head = head_of_row[:, None] == (jnp.arange(prow) % HEADS)[None, :]
    dist_last = (past_len + s_idx[:, None]) - (past_len - psize + key_r[None, :])
    dist_new = jnp.where(key_r[None, :] < n_tok, s_idx[:, None] - key_r[None, :], -1)
    pick = lambda t: jnp.where(own_head, t[head_of_row, jnp.arange(n_rows)], MASK_VALUE)
    t_last = pick(_t5_bias(dist_last, rel_bias))
    t_new = pick(_t5_bias(dist_new, rel_bias))
    t_mask = jnp.where(own_head, 0.0, MASK_VALUE).astype(F32)
    ob_s = _sattn(page_table, qall, pad_new(kb_s[N_META:]), pad_new(vb_s[N_META:]), t_last, t_new,
                  t_mask, lam_p, da_onw, cache_k.reshape(-1, HEAD_W), cache_v.reshape(-1, HEAD_W),
                  lam_init, n_tok)

    h_p = _merge(x_p, oa_p.reshape(bsz * seq, w_attn), ob_p.reshape(bsz * seq, w_attn),
                 n1, w_in_bf, w_ha, w_da, w_o0, TOKEN_TILE)
    y_p = _moe(h_p, n2, wr_hi, wr_lo, br, wg, wu, wd, TOKEN_TILE)
    x_s = x_small[N_META:]
    h_s = _merge(x_s, oa_s.reshape(nb * n_tok, w_attn).astype(BF16),
                 ob_s.reshape(nb * n_tok, w_attn).astype(BF16), n1, w_in_bf, w_ha, w_da, w_o0,
                 nb * n_tok)
    y_s = _moe(h_s, n2, wr_hi, wr_lo, br, wg, wu, wd, nb * n_tok)

    return (y_p.reshape(bsz, seq, d),
            y_s.reshape(nb, n_tok, d),
            kf_p.reshape(1, bsz, seq + N_META, HEADS, HEAD_W),
            vf_p.reshape(1, bsz, seq + N_META, HEADS, HEAD_W),
            s_prompt[None],
            kf_s[N_META:].reshape(1, nb, n_tok, HEADS, HEAD_W),
            vf_s[N_META:].reshape(1, nb, n_tok, HEADS, HEAD_W),
            s_sample[None])
```
